```python
import math
import jax
import jax.numpy as jnp
from jax import lax
import numpy as np

D_MODEL = 4096
BATCH = 2
SEQ = 8192
DEPTH = 4

N_META = 16
BLOCK = 128
N_PAD = (-N_META) % BLOCK
N_BRANCH = 4
BRANCH_WIDTH = D_MODEL // 4
GATE_RANK = 256
RET_DK = 64
RET_DV = 128
RET_HEADS = BRANCH_WIDTH // RET_DV
RWKV_HEAD = 64
RWKV_HEADS = BRANCH_WIDTH // RWKV_HEAD
DECAY_LORA = 64
AAA_LORA = 64
MV_LORA = 32
GATE_LORA = 160
RWKV_GN_EPS = 64e-5
FOX_DH = 128
FOX_HEADS = BRANCH_WIDTH // FOX_DH
S5_GROUP = 16
S5_GROUPS = BRANCH_WIDTH // S5_GROUP
S5_STATE = 64
D_FF = 2 * D_MODEL
N_EXPERTS = 8
TOP_K = 2
D_EXPERT = 3 * D_MODEL // 8
N_DENSE = (DEPTH + 1) // 2
N_MOE = DEPTH // 2
ALPHA = (2.0 * DEPTH) ** 0.25
BETA = (8.0 * DEPTH) ** -0.25
LN_EPS = 1e-5
NEG_INF = -1e30
IN_WIDTHS = (RET_HEADS * RET_DK, RET_HEADS * RET_DK, BRANCH_WIDTH, BRANCH_WIDTH,
             BRANCH_WIDTH,
             FOX_HEADS * FOX_DH, FOX_HEADS * FOX_DH, FOX_HEADS * FOX_DH, FOX_HEADS,
             BRANCH_WIDTH)
IN_WIDTH = sum(IN_WIDTHS)

kernel_name = 'hybrid_retention_rwkv7_fox_s5_moe_trunk'


def _split_points():
    pts, acc = [], 0
    for w in IN_WIDTHS[:-1]:
        acc += w
        pts.append(acc)
    return pts


def _layer_norm(x, g, b):
    xf = x.astype(jnp.float32)
    mu = jnp.mean(xf, axis=-1, keepdims=True)
    var = jnp.mean(jnp.square(xf - mu), axis=-1, keepdims=True)
    return ((xf - mu) * lax.rsqrt(var + LN_EPS) * g + b).astype(x.dtype)


def _head_norm(x, eps):
    mu = jnp.mean(x, axis=-1, keepdims=True)
    var = jnp.mean(jnp.square(x - mu), axis=-1, keepdims=True)
    return (x - mu) * lax.rsqrt(var + eps)


def _pad_front(t):
    return jnp.pad(t, [(0, 0), (N_PAD, 0)] + [(0, 0)] * (t.ndim - 2))


def _rotary(x):
    half = x.shape[-1] // 2
    inv = 1.0 / (10000.0 ** jnp.linspace(0.0, 1.0, half, dtype=jnp.float32))
    ang = jnp.arange(x.shape[1], dtype=jnp.float32)[:, None] * inv[None, :]
    cos = jnp.cos(ang)[None, :, None, :]
    sin = jnp.sin(ang)[None, :, None, :]
    x1, x2 = x[..., :half], x[..., half:]
    return jnp.concatenate([x1 * cos - x2 * sin, x1 * sin + x2 * cos], axis=-1)


def _retention(q, k, v, g):
    f32 = jnp.float32
    bsz, L = q.shape[:2]
    Lp = L + N_PAD
    nc = Lp // BLOCK
    q = _rotary(_pad_front(q.astype(f32).reshape(bsz, L, RET_HEADS, RET_DK)))
    k = _rotary(_pad_front(k.astype(f32).reshape(bsz, L, RET_HEADS, RET_DK))) * RET_DK ** -0.5
    v = _pad_front(v.astype(f32).reshape(bsz, L, RET_HEADS, RET_DV))
    q = q.reshape(bsz, nc, BLOCK, RET_HEADS, RET_DK)
    k = k.reshape(bsz, nc, BLOCK, RET_HEADS, RET_DK)
    v = v.reshape(bsz, nc, BLOCK, RET_HEADS, RET_DV)
    log_gamma = jnp.log(1.0 - 2.0 ** (-5.0 - jnp.arange(RET_HEADS, dtype=f32)))
    pos = jnp.arange(BLOCK, dtype=f32)
    rel = pos[:, None] - pos[None, :]
    intra_decay = jnp.where(rel >= 0, jnp.exp(log_gamma[:, None, None] * jnp.maximum(rel, 0.0)), 0.0)
    scores = jnp.einsum('bnihd,bnjhd->bnhij', q, k) * intra_decay
    y = jnp.einsum('bnhij,bnjhe->bnihe', scores, v)
    k_decay = jnp.exp(log_gamma[:, None] * (BLOCK - 1.0 - pos)[None, :])
    kv = jnp.einsum('bnjhd,hj,bnjhe->nbhde', k, k_decay, v)
    chunk_decay = jnp.exp(log_gamma * BLOCK)[None, :, None, None]

    def step(state, kv_n):
        return state * chunk_decay + kv_n, state

    _, prev = lax.scan(step, jnp.zeros(kv.shape[1:], f32), kv)
    q_decay = jnp.exp(log_gamma[:, None] * (pos + 1.0)[None, :])
    y = y + jnp.einsum('bnihd,hi,nbhde->bnihe', q, q_decay, prev)
    y = _head_norm(y, 1e-6).reshape(bsz, Lp, RET_HEADS * RET_DV)[:, N_PAD:]
    return jax.nn.silu(g.astype(f32)) * y


def _wkv7(r, w, k, v, a, b):
    f32 = jnp.float32
    bsz = r.shape[0]

    def step(S, inp):
        r_t, w_t, k_t, v_t, a_t, b_t = inp
        sa = jnp.einsum('bhij,bhj->bhi', S, a_t)
        S = S * w_t[:, :, None, :] + sa[..., None] * b_t[:, :, None, :] + v_t[..., None] * k_t[:, :, None, :]
        return S, jnp.einsum('bhij,bhj->bhi', S, r_t)

    seq = tuple(t.astype(f32).transpose(1, 0, 2, 3) for t in (r, w, k, v, a, b))
    S0 = jnp.zeros((bsz, RWKV_HEADS, RWKV_HEAD, RWKV_HEAD), f32)
    _, y = lax.scan(step, S0, seq)
    return y.transpose(1, 0, 2, 3)


def _rwkv7(u, v_first, vres, mu, w_rkv, w0, w1, w2, a0, a1, a2, g1, g2, k_k, k_a, r_k, ln_w, ln_b):
    bsz, L, W = u.shape
    u = u.astype(jnp.float32)
    xx = jnp.pad(u, ((0, 0), (1, 0), (0, 0)))[:, :-1] - u
    xr, xw, xk, xv, xa, xg = [u + xx * mu[i] for i in range(6)]
    r = xr @ w_rkv[0]
    k = xk @ w_rkv[1]
    v = xv @ w_rkv[2]
    w_log = -jax.nn.softplus(-(w0 + jnp.tanh(xw @ w1) @ w2)) - 0.5
    decay = jnp.exp(-jnp.exp(w_log))
    if vres is None:
        v_first = v
    else:
        v0, v1, v2 = vres
        v = v + (v_first - v) * jax.nn.sigmoid(v0 + (xv @ v1) @ v2)
    a = jax.nn.sigmoid(a0 + (xa @ a1) @ a2)
    gate = jax.nn.sigmoid(xg @ g1) @ g2

    def heads(t):
        return t.reshape(bsz, L, RWKV_HEADS, RWKV_HEAD)

    kk = heads(k * k_k)
    kk = kk / jnp.maximum(jnp.linalg.norm(kk, axis=-1, keepdims=True), 1e-12)
    k = k * (1.0 + (a - 1.0) * k_a)
    r_h, k_h, v_h, a_h = heads(r), heads(k), heads(v), heads(a)
    y = _wkv7(r_h, heads(decay), k_h, v_h, -kk, kk * a_h)
    y = _head_norm(y, RWKV_GN_EPS).reshape(bsz, L, W) * ln_w + ln_b
    bonus = jnp.sum(r_h * k_h * r_k, axis=-1, keepdims=True) * v_h
    y = y + bonus.reshape(bsz, L, W)
    return y * gate, v_first


def _forgetting_attention(q, k, v, f_logit):
    f32 = jnp.float32
    bsz, L = q.shape[:2]
    Lp = L + N_PAD
    nb = Lp // BLOCK
    q = _pad_front(q.astype(f32).reshape(bsz, L, FOX_HEADS, FOX_DH)) * FOX_DH ** -0.5
    k = _pad_front(k.astype(f32).reshape(bsz, L, FOX_HEADS, FOX_DH))
    v = _pad_front(v.astype(f32).reshape(bsz, L, FOX_HEADS, FOX_DH))
    log_f = _pad_front(jax.nn.log_sigmoid(f_logit.astype(f32)))
    c = jnp.cumsum(log_f, axis=1).transpose(0, 2, 1)
    k_t = k.transpose(0, 2, 1, 3)
    v_t = v.transpose(0, 2, 1, 3)
    q_blocks = q.reshape(bsz, nb, BLOCK, FOX_HEADS, FOX_DH).transpose(1, 0, 3, 2, 4)
    c_blocks = c.reshape(bsz, FOX_HEADS, nb, BLOCK).transpose(2, 0, 1, 3)
    key_pos = jnp.arange(Lp)

    def block(args):
        q_i, c_i, i = args
        q_pos = i * BLOCK + jnp.arange(BLOCK)
        s = jnp.einsum('bhqd,bhkd->bhqk', q_i, k_t) + c_i[..., None] - c[:, :, None, :]
        mask = (key_pos[None, :] <= q_pos[:, None]) & (key_pos[None, :] >= N_PAD)
        p = jax.nn.softmax(jnp.where(mask, s, NEG_INF), axis=-1)
        return jnp.einsum('bhqk,bhkd->bhqd', p, v_t)

    o = lax.map(block, (q_blocks, c_blocks, jnp.arange(nb)))
    o = o.transpose(1, 0, 3, 2, 4).reshape(bsz, Lp, FOX_HEADS * FOX_DH)
    return o[:, N_PAD:]


def _s5(u, a_re, a_im, b_re, b_im, c_re, c_im, d_skip, log_step, w_glu):
    f32 = jnp.float32
    bsz, L, W = u.shape
    uf = u.astype(f32).reshape(bsz, L, S5_GROUPS, S5_GROUP)
    dt = jnp.exp(log_step)[:, None]
    mag = jnp.exp(a_re * dt)
    bar_re = mag * jnp.cos(a_im * dt)
    bar_im = mag * jnp.sin(a_im * dt)
    den = a_re * a_re + a_im * a_im
    num_re = bar_re - 1.0
    coef_re = (num_re * a_re + bar_im * a_im) / den
    coef_im = (bar_im * a_re - num_re * a_im) / den
    bb_re = coef_re[..., None] * b_re - coef_im[..., None] * b_im
    bb_im = coef_re[..., None] * b_im + coef_im[..., None] * b_re
    bu_re = jnp.einsum('blgc,gpc->blgp', uf, bb_re)
    bu_im = jnp.einsum('blgc,gpc->blgp', uf, bb_im)
    ar = jnp.broadcast_to(bar_re, bu_re.shape)
    ai = jnp.broadcast_to(bar_im, bu_im.shape)

    def combine(e1, e2):
        a1r, a1i, b1r, b1i = e1
        a2r, a2i, b2r, b2i = e2
        return (a2r * a1r - a2i * a1i, a2r * a1i + a2i * a1r,
                a2r * b1r - a2i * b1i + b2r, a2r * b1i + a2i * b1r + b2i)

    _, _, h_re, h_im = lax.associative_scan(combine, (ar, ai, bu_re, bu_im), axis=1)
    y = (jnp.einsum('blgp,gcp->blgc', h_re, c_re) - jnp.einsum('blgp,gcp->blgc', h_im, c_im)
         + d_skip.reshape(S5_GROUPS, S5_GROUP) * uf)
    y = jax.nn.gelu(y.reshape(bsz, L, W))
    val, gt = jnp.split(y @ w_glu, 2, axis=-1)
    return val * jax.nn.sigmoid(gt)


def _swiglu(h, w_gate, w_up, w_down):
    return (jax.nn.silu(h @ w_gate) * (h @ w_up)) @ w_down


def _moe(h, w_router, b_router, w_gate, w_up, w_down):
    f32 = jnp.float32
    logits = (h @ w_router).astype(f32) + b_router
    top_vals, top_idx = lax.top_k(logits, TOP_K)
    weights = jax.nn.softmax(top_vals, axis=-1)
    combine = jnp.sum(jax.nn.one_hot(top_idx, N_EXPERTS, dtype=f32) * weights[..., None], axis=-2)
    out = jnp.zeros(h.shape, f32)
    for e in range(N_EXPERTS):
        out = out + combine[..., e:e + 1] * _swiglu(h, w_gate[e], w_up[e], w_down[e])
    return out.astype(h.dtype)


def setup_inputs(seed: int = 0) -> dict:
    key = jax.random.key(seed)
    keys = iter(jax.random.split(key, 64))
    f32 = jnp.float32

    def nrm(shape, scale):
        return jax.random.normal(next(keys), shape, f32) * scale

    def uni(shape, lo, hi):
        return jax.random.uniform(next(keys), shape, f32, lo, hi)

    W = BRANCH_WIDTH
    G, P = S5_GROUPS, S5_STATE
    NR = max(DEPTH - 1, 0)
    return {
        'x': nrm((BATCH, SEQ, D_MODEL), 1.0),
        'meta_tokens': nrm((N_META, D_MODEL), 1.0),
        'w_in': nrm((DEPTH, D_MODEL, IN_WIDTH), D_MODEL ** -0.5),
        'b_forget': uni((DEPTH, FOX_HEADS), 1.0, 5.0),
        'w_gate_down': nrm((DEPTH, D_MODEL, GATE_RANK), D_MODEL ** -0.5),
        'w_gate_up': nrm((DEPTH, N_BRANCH, GATE_RANK, D_MODEL), GATE_RANK ** -0.5),
        'b_gate': nrm((DEPTH, N_BRANCH, D_MODEL), 0.1),
        'w_branch': nrm((DEPTH, N_BRANCH, W, D_MODEL), BETA * W ** -0.5),
        'w_out': nrm((DEPTH, D_MODEL, D_MODEL), BETA * D_MODEL ** -0.5),
        'ln_mix_g': 1.0 + nrm((DEPTH, D_MODEL), 0.02),
        'ln_mix_b': nrm((DEPTH, D_MODEL), 0.02),
        'ln_ffn_g': 1.0 + nrm((DEPTH, D_MODEL), 0.02),
        'ln_ffn_b': nrm((DEPTH, D_MODEL), 0.02),
        'rwkv_mu': uni((DEPTH, 6, W), 0.0, 1.0),
        'rwkv_w_rkv': nrm((DEPTH, 3, W, W), W ** -0.5),
        'rwkv_w0': uni((DEPTH, W), -6.0, -1.0),
        'rwkv_w1': nrm((DEPTH, W, DECAY_LORA), W ** -0.5),
        'rwkv_w2': nrm((DEPTH, DECAY_LORA, W), 0.1 * DECAY_LORA ** -0.5),
        'rwkv_a0': nrm((DEPTH, W), 0.1),
        'rwkv_a1': nrm((DEPTH, W, AAA_LORA), W ** -0.5),
        'rwkv_a2': nrm((DEPTH, AAA_LORA, W), 0.1 * AAA_LORA ** -0.5),
        'rwkv_v0': 1.0 + nrm((NR, W), 0.1),
        'rwkv_v1': nrm((NR, W, MV_LORA), W ** -0.5),
        'rwkv_v2': nrm((NR, MV_LORA, W), 0.1 * MV_LORA ** -0.5),
        'rwkv_g1': nrm((DEPTH, W, GATE_LORA), W ** -0.5),
        'rwkv_g2': nrm((DEPTH, GATE_LORA, W), GATE_LORA ** -0.5),
        'rwkv_k_k': 0.85 + nrm((DEPTH, W), 0.05),
        'rwkv_k_a': 1.0 + nrm((DEPTH, W), 0.05),
        'rwkv_r_k': nrm((DEPTH, RWKV_HEADS, RWKV_HEAD), 0.1),
        'rwkv_ln_w': 1.0 + nrm((DEPTH, W), 0.02),
        'rwkv_ln_b': nrm((DEPTH, W), 0.02),
        's5_a_re': -0.5 * jnp.exp(nrm((DEPTH, G, P), 0.05)),
        's5_a_im': math.pi * jnp.arange(P, dtype=f32) + nrm((DEPTH, G, P), 0.01),
        's5_b_re': nrm((DEPTH, G, P, S5_GROUP), (2.0 * S5_GROUP) ** -0.5),
        's5_b_im': nrm((DEPTH, G, P, S5_GROUP), (2.0 * S5_GROUP) ** -0.5),
        's5_c_re': nrm((DEPTH, G, S5_GROUP, P), (2.0 * P) ** -0.5),
        's5_c_im': nrm((DEPTH, G, S5_GROUP, P), (2.0 * P) ** -0.5),
        's5_d': nrm((DEPTH, W), 1.0),
        's5_log_step': uni((DEPTH, G), math.log(1e-3), math.log(1e-1)),
        's5_w_glu': nrm((DEPTH, W, 2 * W), W ** -0.5),
        'ffn_w_gate': nrm((N_DENSE, D_MODEL, D_FF), D_MODEL ** -0.5),
        'ffn_w_up': nrm((N_DENSE, D_MODEL, D_FF), BETA * D_MODEL ** -0.5),
        'ffn_w_down': nrm((N_DENSE, D_FF, D_MODEL), BETA * D_FF ** -0.5),
        'moe_router': nrm((N_MOE, D_MODEL, N_EXPERTS), D_MODEL ** -0.5),
        'moe_router_b': nrm((N_MOE, N_EXPERTS), 0.01),
        'moe_w_gate': nrm((N_MOE, N_EXPERTS, D_MODEL, D_EXPERT), D_MODEL ** -0.5),
        'moe_w_up': nrm((N_MOE, N_EXPERTS, D_MODEL, D_EXPERT), BETA * D_MODEL ** -0.5),
        'moe_w_down': nrm((N_MOE, N_EXPERTS, D_EXPERT, D_MODEL), BETA * D_EXPERT ** -0.5),
    }


def reference(x, meta_tokens, w_in, b_forget, w_gate_down, w_gate_up, b_gate, w_branch, w_out,
              ln_mix_g, ln_mix_b, ln_ffn_g, ln_ffn_b,
              rwkv_mu, rwkv_w_rkv, rwkv_w0, rwkv_w1, rwkv_w2, rwkv_a0, rwkv_a1, rwkv_a2,
              rwkv_v0, rwkv_v1, rwkv_v2, rwkv_g1, rwkv_g2, rwkv_k_k, rwkv_k_a, rwkv_r_k,
              rwkv_ln_w, rwkv_ln_b,
              s5_a_re, s5_a_im, s5_b_re, s5_b_im, s5_c_re, s5_c_im, s5_d, s5_log_step, s5_w_glu,
              ffn_w_gate, ffn_w_up, ffn_w_down,
              moe_router, moe_router_b, moe_w_gate, moe_w_up, moe_w_down):
    f32 = jnp.float32
    bsz = x.shape[0]
    meta = jnp.broadcast_to(meta_tokens.astype(x.dtype)[None], (bsz, N_META, D_MODEL))
    h = jnp.concatenate([meta, x], axis=1)
    v_first = None
    for layer in range(DEPTH):
        (ret_q, ret_k, ret_v, ret_g, rwkv_u, fox_q, fox_k, fox_v, fox_f, s5_u) = jnp.split(
            h @ w_in[layer], _split_points(), axis=-1)
        o_ret = _retention(ret_q, ret_k, ret_v, ret_g)
        vres = None if layer == 0 else (rwkv_v0[layer - 1], rwkv_v1[layer - 1], rwkv_v2[layer - 1])
        o_rwkv, v_first = _rwkv7(rwkv_u, v_first, vres, rwkv_mu[layer], rwkv_w_rkv[layer],
                                 rwkv_w0[layer], rwkv_w1[layer], rwkv_w2[layer],
                                 rwkv_a0[layer], rwkv_a1[layer], rwkv_a2[layer],
                                 rwkv_g1[layer], rwkv_g2[layer], rwkv_k_k[layer], rwkv_k_a[layer],
                                 rwkv_r_k[layer], rwkv_ln_w[layer], rwkv_ln_b[layer])
        o_fox = _forgetting_attention(fox_q, fox_k, fox_v, fox_f + b_forget[layer])
        o_s5 = _s5(s5_u, s5_a_re[layer], s5_a_im[layer], s5_b_re[layer], s5_b_im[layer],
                   s5_c_re[layer], s5_c_im[layer], s5_d[layer], s5_log_step[layer], s5_w_glu[layer])
        gate_lr = h @ w_gate_down[layer]
        merged = jnp.zeros(h.shape, f32)
        for i, o in enumerate((o_ret, o_rwkv, o_fox, o_s5)):
            gate = jax.nn.sigmoid((gate_lr @ w_gate_up[layer, i]).astype(f32) + b_gate[layer, i])
            merged = merged + gate * (o.astype(h.dtype) @ w_branch[layer, i])
        mix = merged.astype(h.dtype) @ w_out[layer]
        h = _layer_norm(ALPHA * h + mix, ln_mix_g[layer], ln_mix_b[layer])
        j = layer // 2
        if layer % 2 == 0:
            ffn = _swiglu(h, ffn_w_gate[j], ffn_w_up[j], ffn_w_down[j])
        else:
            ffn = _moe(h, moe_router[j], moe_router_b[j], moe_w_gate[j], moe_w_up[j], moe_w_down[j])
        h = _layer_norm(ALPHA * h + ffn.astype(h.dtype), ln_ffn_g[layer], ln_ffn_b[layer])
    return h[:, N_META:]
```

```python
import functools
import math

import numpy as np
import jax
import jax.numpy as jnp
from jax import lax
from jax.experimental import pallas as pl
from jax.experimental.pallas import tpu as pltpu

F32 = jnp.float32
BF16 = jnp.bfloat16

D_MODEL = 4096
DEPTH = 4
N_META = 16
BLOCK = 128
N_PAD = (-N_META) % BLOCK
BRANCH_WIDTH = D_MODEL // 4
GATE_RANK = 256
RET_DK = 64
RET_DV = 128
RET_HEADS = BRANCH_WIDTH // RET_DV
RWKV_HEAD = 64
RWKV_HEADS = BRANCH_WIDTH // RWKV_HEAD
RWKV_GN_EPS = 64e-5
FOX_DH = 128
FOX_HEADS = BRANCH_WIDTH // FOX_DH
S5_GROUP = 16
S5_GROUPS = BRANCH_WIDTH // S5_GROUP
S5_STATE = 64
N_EXPERTS = 8
TOP_K = 2
ALPHA = (2.0 * DEPTH) ** 0.25
LN_EPS = 1e-5
NEG_INF = -1e30

V7X_VMEM_BYTES = 64 * 1024 * 1024
V7X_VMEM_BUDGET = 58 * 1024 * 1024
LANE = 128
RWKV_CHUNK = 64
RWKV_QUAD = 4


def _pick(dim, candidates):
    for c in candidates:
        if dim % c == 0:
            return c
    return dim


def _params(sem, est_bytes):
    limit = int(min(V7X_VMEM_BUDGET, max(est_bytes * 5 // 4 + (2 << 20), 16 << 20)))
    return pltpu.CompilerParams(dimension_semantics=sem, vmem_limit_bytes=limit)


def _mm_kernel(a_ref, b_ref, o_ref, *acc, nk, precision):
    if nk == 1:
        o_ref[...] = jnp.dot(a_ref[...], b_ref[...], preferred_element_type=F32,
                             precision=precision).astype(o_ref.dtype)
        return
    acc_ref, = acc
    k = pl.program_id(2)

    @pl.when(k == 0)
    def _():
        acc_ref[...] = jnp.zeros_like(acc_ref)

    acc_ref[...] += jnp.dot(a_ref[...], b_ref[...], preferred_element_type=F32, precision=precision)

    @pl.when(k == nk - 1)
    def _():
        o_ref[...] = acc_ref[...].astype(o_ref.dtype)


def _mm(a, b, out_dtype=F32, precision=None):
    m, kd = a.shape
    n = b.shape[1]
    tm = _pick(m, (1280, 640, 512, 256, 128))
    tn = _pick(n, (1024, 512, 384, 256, 128))
    tk = kd if kd <= 4096 else _pick(kd, (2048, 1536, 1024, 512))
    if a.dtype == F32:
        tm = _pick(m, (640, 512, 256, 128))
        tn = _pick(n, (512, 384, 256, 128))
        tk = kd if kd <= 2048 else _pick(kd, (2048, 1024, 512))
    nk = kd // tk
    ab, bb, ob = a.dtype.itemsize, b.dtype.itemsize, jnp.dtype(out_dtype).itemsize
    est = 2 * (tm * tk * ab + tk * tn * bb + tm * tn * ob) + tm * tn * 4 * 2
    scratch = [pltpu.VMEM((tm, tn), F32)] if nk > 1 else []
    return pl.pallas_call(
        functools.partial(_mm_kernel, nk=nk, precision=precision),
        name="mm",
        grid=(m // tm, n // tn, nk),
        in_specs=[pl.BlockSpec((tm, tk), lambda i, j, k: (i, k)),
                  pl.BlockSpec((tk, tn), lambda i, j, k: (k, j))],
        out_specs=pl.BlockSpec((tm, tn), lambda i, j, k: (i, j)),
        out_shape=jax.ShapeDtypeStruct((m, n), out_dtype),
        scratch_shapes=scratch,
        compiler_params=_params(("parallel", "parallel", "arbitrary"), est),
    )(a, b)


def _swiglu_kernel(h_ref, wg_ref, wu_ref, *rest, scaled):
    o_ref = rest[-1]
    h = h_ref[...]
    g = jnp.dot(h, wg_ref[...], preferred_element_type=F32)
    u = jnp.dot(h, wu_ref[...], preferred_element_type=F32)
    act = g * jax.nn.sigmoid(g) * u
    if scaled:
        act = act * rest[0][...]
    o_ref[...] = act.astype(o_ref.dtype)


def _swiglu_up(h, wg, wu, scale=None):
    m, d = h.shape
    e, _, f = wg.shape
    tm = _pick(m, (1280, 640, 512, 256, 128))
    tn = _pick(f, (512, 256, 128))
    nf = f // tn
    est = 2 * (tm * d * 2 + 2 * d * tn * 2 + tm * tn * 2) + 3 * tm * tn * 4
    in_specs = [pl.BlockSpec((tm, d), lambda i, ei, j: (i, 0)),
                pl.BlockSpec((None, d, tn), lambda i, ei, j: (ei, 0, j)),
                pl.BlockSpec((None, d, tn), lambda i, ei, j: (ei, 0, j))]
    args = [h, wg, wu]
    if scale is not None:
        in_specs.append(pl.BlockSpec((None, tm, 1), lambda i, ei, j: (ei, i, 0)))
        args.append(scale)
    return pl.pallas_call(
        functools.partial(_swiglu_kernel, scaled=scale is not None),
        name="swiglu_up",
        grid=(m // tm, e, nf),
        in_specs=in_specs,
        out_specs=pl.BlockSpec((tm, tn), lambda i, ei, j: (i, ei * nf + j)),
        out_shape=jax.ShapeDtypeStruct((m, e * f), BF16),
        compiler_params=_params(("parallel", "parallel", "parallel"), est),
    )(*args)


def _res_ln_kernel(y_ref, r_ref, g_ref, b_ref, o_ref, ob_ref, *, tm, mask_pad):
    z = ALPHA * r_ref[...] + y_ref[...]
    mu = jnp.mean(z, axis=-1, keepdims=True)
    zc = z - mu
    var = jnp.mean(zc * zc, axis=-1, keepdims=True)
    out = zc * lax.rsqrt(var + LN_EPS) * g_ref[...] + b_ref[...]
    if mask_pad:
        pos = pl.program_id(1) * tm + lax.broadcasted_iota(jnp.int32, (tm, 1), 0)
        out = jnp.where(pos >= N_PAD, out, 0.0)
    o_ref[...] = out
    ob_ref[...] = out.astype(BF16)


def _res_ln(y, res, g, b, mask_pad):
    bsz, lp, d = res.shape
    tm = _pick(lp, (256, 128, 64, 32, 16, 8))
    est = 2 * (4 * tm * d * 4) + 4 * tm * d * 4
    spec = pl.BlockSpec((None, tm, d), lambda bi, i: (bi, i, 0))
    vspec = pl.BlockSpec((1, d), lambda bi, i: (0, 0))
    return pl.pallas_call(
        functools.partial(_res_ln_kernel, tm=tm, mask_pad=mask_pad),
        name="res_ln",
        grid=(bsz, lp // tm),
        in_specs=[spec, spec, vspec, vspec],
        out_specs=[spec, spec],
        out_shape=[jax.ShapeDtypeStruct((bsz, lp, d), F32), jax.ShapeDtypeStruct((bsz, lp, d), BF16)],
        compiler_params=_params(("parallel", "parallel"), est),
    )(y, res, g.reshape(1, d), b.reshape(1, d))


def _merge_kernel(o0, o1, o2, o3, wb_ref, glr_ref, wgu_ref, bg_ref, out_ref):
    glr = glr_ref[...]
    acc = None
    for i, o_ref in enumerate((o0, o1, o2, o3)):
        gate = jax.nn.sigmoid(jnp.dot(glr, wgu_ref[i], preferred_element_type=F32) + bg_ref[i])
        term = gate * jnp.dot(o_ref[...], wb_ref[i], preferred_element_type=F32)
        acc = term if acc is None else acc + term
    out_ref[...] = acc.astype(out_ref.dtype)


def _merge(outs, w_branch, glr, w_gate_up, b_gate):
    m, w = outs[0].shape
    nb, _, d = w_branch.shape
    r = glr.shape[1]
    tm = _pick(m, (1280, 640, 512, 256, 128))
    tn = _pick(d, (512, 256, 128))
    est = 2 * (nb * tm * w * 2 + nb * w * tn * 2 + tm * r * 2 + nb * r * tn * 2 + tm * tn * 2) + 4 * tm * tn * 4
    ospec = pl.BlockSpec((tm, w), lambda i, j: (i, 0))
    return pl.pallas_call(
        _merge_kernel,
        name="merge",
        grid=(m // tm, d // tn),
        in_specs=[ospec, ospec, ospec, ospec,
                  pl.BlockSpec((nb, w, tn), lambda i, j: (0, 0, j)),
                  pl.BlockSpec((tm, r), lambda i, j: (i, 0)),
                  pl.BlockSpec((nb, r, tn), lambda i, j: (0, 0, j)),
                  pl.BlockSpec((nb, 1, tn), lambda i, j: (0, 0, j))],
        out_specs=pl.BlockSpec((tm, tn), lambda i, j: (i, j)),
        out_shape=jax.ShapeDtypeStruct((m, d), BF16),
        compiler_params=_params(("parallel", "parallel"), est),
    )(*outs, w_branch, glr, w_gate_up, b_gate.reshape(nb, 1, d))


def _ret_consts(lp):
    f32 = F32
    half = RET_DK // 2
    inv = 1.0 / (10000.0 ** jnp.linspace(0.0, 1.0, half, dtype=f32))
    ang = jnp.arange(lp, dtype=f32)[:, None] * inv[None, :]
    cos, sin = jnp.cos(ang), jnp.sin(ang)
    cos_t = jnp.tile(jnp.concatenate([cos, cos], axis=-1), (1, RET_HEADS))
    sin_t = jnp.tile(jnp.concatenate([-sin, sin], axis=-1), (1, RET_HEADS))
    log_gamma = jnp.log(1.0 - 2.0 ** (-5.0 - jnp.arange(RET_HEADS, dtype=f32)))
    pos = jnp.arange(BLOCK, dtype=f32)
    rel = pos[:, None] - pos[None, :]
    dmat = jnp.where(rel >= 0, jnp.exp(log_gamma[:, None, None] * jnp.maximum(rel, 0.0)), 0.0)
    k_decay = jnp.exp(log_gamma[:, None] * (BLOCK - 1.0 - pos)[None, :])
    q_decay = jnp.exp(log_gamma[:, None] * (pos + 1.0)[None, :])
    kd_t = jnp.repeat(k_decay.T, RET_DK, axis=1)
    qd_t = jnp.repeat(q_decay.T, RET_DK, axis=1)
    cd = jnp.exp(log_gamma * BLOCK)
    cd_t = jnp.broadcast_to(cd[:, None, None], (RET_HEADS, 1, RET_DV))
    return cos_t, sin_t, dmat, qd_t, kd_t, cd_t


def _ret_kernel(q_ref, k_ref, v_ref, g_ref, cos_ref, sin_ref, dmat_ref, qd_ref, kd_ref, cd_ref,
                o_ref, st_ref):
    @pl.when(pl.program_id(1) == 0)
    def _():
        st_ref[...] = jnp.zeros_like(st_ref)

    cos = cos_ref[...]
    sin = sin_ref[...]
    width = RET_HEADS * RET_DK
    lane = lax.broadcasted_iota(jnp.int32, (BLOCK, width), 1)
    first = (lane % RET_DK) < (RET_DK // 2)

    def rot(x):
        sw = jnp.where(first, pltpu.roll(x, width - RET_DK // 2, 1), pltpu.roll(x, RET_DK // 2, 1))
        return x * cos + sw * sin

    q = rot(q_ref[...])
    k = rot(k_ref[...]) * RET_DK ** -0.5
    qb = q.astype(BF16)
    kb = k.astype(BF16)
    qs = (q * qd_ref[...]).astype(BF16)
    ks = (k * kd_ref[...]).astype(BF16)
    for h in range(RET_HEADS):
        ksl = slice(h * RET_DK, (h + 1) * RET_DK)
        vsl = slice(h * RET_DV, (h + 1) * RET_DV)
        vh = v_ref[:, vsl].astype(BF16)
        s = lax.dot_general(qb[:, ksl], kb[:, ksl], (((1,), (1,)), ((), ())),
                            preferred_element_type=F32) * dmat_ref[h]
        st = st_ref[h]
        y = (jnp.dot(s.astype(BF16), vh, preferred_element_type=F32)
             + jnp.dot(qs[:, ksl], st.astype(BF16), preferred_element_type=F32))
        st_ref[h] = st * cd_ref[h] + lax.dot_general(ks[:, ksl], vh, (((0,), (0,)), ((), ())),
                                                     preferred_element_type=F32)
        mu = jnp.mean(y, axis=-1, keepdims=True)
        yc = y - mu
        var = jnp.mean(yc * yc, axis=-1, keepdims=True)
        gh = g_ref[:, vsl]
        o_ref[:, vsl] = (gh * jax.nn.sigmoid(gh) * yc * lax.rsqrt(var + 1e-6)).astype(o_ref.dtype)


def _retention(proj):
    bsz, lp = proj.shape[:2]
    kw = RET_HEADS * RET_DK
    cos_t, sin_t, dmat, qd_t, kd_t, cd_t = _ret_consts(lp)
    est = 2 * BLOCK * 4 * (2 * kw + 2 * BRANCH_WIDTH + 4 * kw) + 2 * RET_HEADS * BLOCK * BLOCK * 4 + (8 << 20)
    tab = pl.BlockSpec((BLOCK, kw), lambda b, n: (n, 0))
    ctab = pl.BlockSpec((BLOCK, kw), lambda b, n: (0, 0))
    return pl.pallas_call(
        _ret_kernel,
        name="retention",
        grid=(bsz, lp // BLOCK),
        in_specs=[pl.BlockSpec((None, BLOCK, kw), lambda b, n: (b, n, 0)),
                  pl.BlockSpec((None, BLOCK, kw), lambda b, n: (b, n, 1)),
                  pl.BlockSpec((None, BLOCK, BRANCH_WIDTH), lambda b, n: (b, n, 1)),
                  pl.BlockSpec((None, BLOCK, BRANCH_WIDTH), lambda b, n: (b, n, 2)),
                  tab, tab,
                  pl.BlockSpec((RET_HEADS, BLOCK, BLOCK), lambda b, n: (0, 0, 0)),
                  ctab, ctab,
                  pl.BlockSpec((RET_HEADS, 1, RET_DV), lambda b, n: (0, 0, 0))],
        out_specs=pl.BlockSpec((None, BLOCK, BRANCH_WIDTH), lambda b, n: (b, n, 0)),
        out_shape=jax.ShapeDtypeStruct((bsz, lp, BRANCH_WIDTH), BF16),
        scratch_shapes=[pltpu.VMEM((RET_HEADS, RET_DK, RET_DV), F32)],
        compiler_params=_params(("parallel", "arbitrary"), est),
    )(proj, proj, proj, proj, cos_t, sin_t, dmat, qd_t, kd_t, cd_t)


def _wkv_kernel(r_ref, lw_ref, k_ref, v_ref, a_ref, b_ref, y_ref, ht_ref):
    c = RWKV_CHUNK
    wd = RWKV_QUAD * RWKV_HEAD

    @pl.when(pl.program_id(2) == 0)
    def _():
        ht_ref[...] = jnp.zeros_like(ht_ref)

    lw = lw_ref[...]
    ti = lax.broadcasted_iota(jnp.int32, (c, c), 0)
    tj = lax.broadcasted_iota(jnp.int32, (c, c), 1)
    cum = jnp.dot((ti >= tj).astype(F32), lw, preferred_element_type=F32, precision=lax.Precision.HIGHEST)
    cum_last = cum[c - 1:c, :]
    e_neg = jnp.exp(-cum)
    e_rem = jnp.exp(cum_last - cum)
    k = k_ref[...]
    b = b_ref[...]

    ri = lax.broadcasted_iota(jnp.int32, (wd, wd), 0)
    ci = lax.broadcasted_iota(jnp.int32, (wd, wd), 1)
    same = (ri // c) == (ci // RWKV_HEAD)
    strict = same & ((ci % c) < (ri % c))
    incl = same & ((ci % c) <= (ri % c))

    def expand(x):
        xb = x.astype(BF16)
        return jnp.where(same, jnp.concatenate([xb] * RWKV_QUAD, axis=0), jnp.zeros((), BF16))

    at = expand(a_ref[...] * jnp.exp(cum - lw))
    rt = expand(r_ref[...] * jnp.exp(cum))
    bt = expand(b * e_neg)
    kt = expand(k * e_neg)
    vb = expand(v_ref[...])
    bw = expand(b * e_rem)
    kw = expand(k * e_rem)

    def nt(x, y):
        return lax.dot_general(x, y, (((1,), (1,)), ((), ())), preferred_element_type=F32)

    def tn(x, y):
        return lax.dot_general(x, y, (((0,), (0,)), ((), ())), preferred_element_type=F32)

    def mm(x, y):
        return jnp.dot(x, y, preferred_element_type=F32)

    n_ab = jnp.where(strict, nt(at, bt), 0.0)
    a_ak = jnp.where(strict, nt(at, kt), 0.0).astype(BF16)
    a_rb = jnp.where(incl, nt(rt, bt), 0.0).astype(BF16)
    a_rk = jnp.where(incl, nt(rt, kt), 0.0).astype(BF16)

    ht = ht_ref[...]
    htb = ht.astype(BF16)
    u = nt(at, htb) + mm(a_ak, vb)
    p = n_ab
    steps = int(math.log2(c))
    for i in range(steps):
        pb = p.astype(BF16)
        u = u + mm(pb, u.astype(BF16))
        if i < steps - 1:
            p = mm(pb, pb)
    ub = u.astype(BF16)
    y_bd = nt(rt, htb) + mm(a_rb, ub) + mm(a_rk, vb)
    y = y_bd[0:c]
    for h in range(1, RWKV_QUAD):
        y = y + y_bd[h * c:(h + 1) * c]
    y_ref[...] = y
    ht_ref[...] = ht * jnp.exp(cum_last) + tn(ub, bw) + tn(vb, kw)


def _wkv7(r, lw, k, v, a, b):
    bsz, lp, w = r.shape
    c = RWKV_CHUNK
    wd = RWKV_QUAD * RWKV_HEAD
    spec = pl.BlockSpec((None, c, wd), lambda bi, q, n: (bi, n, q))
    est = 2 * 7 * c * wd * 4 + 40 * wd * wd * 4
    return pl.pallas_call(
        _wkv_kernel,
        name="wkv7",
        grid=(bsz, w // wd, lp // c),
        in_specs=[spec] * 6,
        out_specs=spec,
        out_shape=jax.ShapeDtypeStruct((bsz, lp, w), F32),
        scratch_shapes=[pltpu.VMEM((wd, wd), F32)],
        compiler_params=_params(("parallel", "parallel", "arbitrary"), est),
    )(r, lw, k, v, a, b)


def _fox_kernel(q_ref, k_ref, v_ref, cq_ref, ck_ref, o_ref, *, tq):
    qi = pl.program_id(2)
    q = q_ref[...]
    cq = cq_ref[...]
    qpos = qi * tq + lax.broadcasted_iota(jnp.int32, (tq, tq), 0)
    kiota = lax.broadcasted_iota(jnp.int32, (tq, tq), 1)

    def body(kb, carry):
        m, l, acc = carry
        ks = pl.multiple_of(kb * tq, tq)
        kk = k_ref[pl.ds(ks, tq), :]
        vv = v_ref[pl.ds(ks, tq), :]
        s = lax.dot_general(q, kk, (((1,), (1,)), ((), ())), preferred_element_type=F32)
        s = s * FOX_DH ** -0.5 + cq - ck_ref[kb]
        kpos = ks + kiota
        s = jnp.where((kpos <= qpos) & (kpos >= N_PAD), s, NEG_INF)
        m_new = jnp.maximum(m, jnp.max(s, axis=-1, keepdims=True))
        alpha = jnp.exp(m - m_new)
        p = jnp.exp(s - m_new)
        l = alpha * l + jnp.sum(p, axis=-1, keepdims=True)
        acc = alpha * acc + jnp.dot(p.astype(BF16), vv, preferred_element_type=F32)
        return m_new, l, acc

    init = (jnp.full((tq, 1), NEG_INF, F32), jnp.zeros((tq, 1), F32), jnp.zeros((tq, FOX_DH), F32))
    m, l, acc = lax.fori_loop(0, qi + 1, body, init)
    o_ref[...] = (acc / l).astype(o_ref.dtype)


def _fox(qkv, c):
    bsz, lp = qkv.shape[:2]
    tq = _pick(lp, (640, 512, 256, 128))
    nq = lp // tq
    c_col = c.reshape(bsz, FOX_HEADS, lp, 1)
    c_row = c.reshape(bsz, FOX_HEADS, nq, 1, tq)
    est = 2 * (tq * FOX_DH * 2 * 2 + 2 * lp * FOX_DH * 2 + tq * LANE * 4 + lp * 8 * 4) + 8 * tq * tq * 4
    return pl.pallas_call(
        functools.partial(_fox_kernel, tq=tq),
        name="fox",
        grid=(bsz, FOX_HEADS, nq),
        in_specs=[pl.BlockSpec((None, tq, FOX_DH), lambda b, h, i: (b, i, h)),
                  pl.BlockSpec((None, lp, FOX_DH), lambda b, h, i: (b, 0, FOX_HEADS + h)),
                  pl.BlockSpec((None, lp, FOX_DH), lambda b, h, i: (b, 0, 2 * FOX_HEADS + h)),
                  pl.BlockSpec((None, None, tq, 1), lambda b, h, i: (b, h, i, 0)),
                  pl.BlockSpec((None, None, nq, 1, tq), lambda b, h, i: (b, h, 0, 0, 0))],
        out_specs=pl.BlockSpec((None, tq, FOX_DH), lambda b, h, i: (b, i, h)),
        out_shape=jax.ShapeDtypeStruct((bsz, lp, BRANCH_WIDTH), BF16),
        compiler_params=_params(("parallel", "parallel", "parallel"), est),
    )(qkv, qkv, qkv, c_col, c_row)


S5_ROWS = 16
S5_COLS = S5_GROUPS * S5_STATE // 8


def _s5_scan_kernel(x_ref, a_ref, o_ref, st_ref, *, tc):
    @pl.when(pl.program_id(1) == 0)
    def _():
        st_ref[...] = jnp.zeros_like(st_ref)

    ar = a_ref[0:8]
    ai = a_ref[8:16]

    def body(t, carry):
        hr, hi = carry
        x = x_ref[t]
        nr = ar * hr - ai * hi + x[0:8]
        ni = ar * hi + ai * hr + x[8:16]
        o_ref[t] = jnp.concatenate([nr, ni], axis=0).astype(o_ref.dtype)
        return nr, ni

    hr, hi = lax.fori_loop(0, tc, body, (st_ref[0:8], st_ref[8:16]), unroll=8)
    st_ref[0:8] = hr
    st_ref[8:16] = hi


def _s5_scan(bu, a_bar):
    bsz, lp = bu.shape[:2]
    tc = _pick(lp, (128, 64, 32, 16, 8))
    est = 2 * tc * S5_ROWS * S5_COLS * (4 + 2) + (4 << 20)
    return pl.pallas_call(
        functools.partial(_s5_scan_kernel, tc=tc),
        name="s5_scan",
        grid=(bsz, lp // tc),
        in_specs=[pl.BlockSpec((None, tc, S5_ROWS, S5_COLS), lambda b, n: (b, n, 0, 0)),
                  pl.BlockSpec((S5_ROWS, S5_COLS), lambda b, n: (0, 0))],
        out_specs=pl.BlockSpec((None, tc, S5_ROWS, S5_COLS), lambda b, n: (b, n, 0, 0)),
        out_shape=jax.ShapeDtypeStruct(bu.shape, BF16),
        scratch_shapes=[pltpu.VMEM((S5_ROWS, S5_COLS), F32)],
        compiler_params=_params(("parallel", "arbitrary"), est),
    )(bu, a_bar)


def _s5_weights(a_re, a_im, b_re, b_im, c_re, c_im, log_step):
    g, p, cg = S5_GROUPS, S5_STATE, S5_GROUP
    dt = jnp.exp(log_step)[:, None]
    mag = jnp.exp(a_re * dt)
    bar_re = mag * jnp.cos(a_im * dt)
    bar_im = mag * jnp.sin(a_im * dt)
    den = a_re * a_re + a_im * a_im
    num_re = bar_re - 1.0
    coef_re = (num_re * a_re + bar_im * a_im) / den
    coef_im = (bar_im * a_re - num_re * a_im) / den
    bb_re = coef_re[..., None] * b_re - coef_im[..., None] * b_im
    bb_im = coef_re[..., None] * b_im + coef_im[..., None] * b_re
    eye = jnp.eye(g, dtype=F32)
    bblk = jnp.concatenate([jnp.einsum('gpc,gh->gchp', m, eye).reshape(g * cg, g * p) for m in (bb_re, bb_im)],
                           axis=1)
    cblk = jnp.concatenate([jnp.einsum('gcp,gh->gphc', m, eye).reshape(g * p, g * cg) for m in (c_re, -c_im)],
                           axis=0)
    a_bar = jnp.concatenate([bar_re.reshape(8, S5_COLS), bar_im.reshape(8, S5_COLS)], axis=0)
    return bblk.astype(BF16), cblk.astype(BF16), a_bar


def _pad_cols(w, n):
    return jnp.pad(w, ((0, 0), (0, n - w.shape[1])))


def _pad_rows(w, n):
    return jnp.pad(w, ((0, n - w.shape[0]), (0, 0)))


def _rwkv_branch(u, v_first, vres, mu, w_rkv, w0, w1, w2, a0, a1, a2, g1, g2, k_k, k_a, r_k, ln_w, ln_b):
    bsz, lp, w = u.shape
    t = bsz * lp

    def mm2(x, wt):
        return _mm(x.reshape(t, -1).astype(BF16), wt.astype(BF16)).reshape(bsz, lp, -1)

    def lora(x, wa, wb, act):
        rank = wa.shape[1]
        rp = -(-rank // LANE) * LANE
        mid = act(mm2(x, _pad_cols(wa, rp)))
        return mm2(mid, _pad_rows(wb, rp))

    xx = jnp.pad(u, ((0, 0), (1, 0), (0, 0)))[:, :-1] - u
    xr, xw, xk, xv, xa, xg = [u + xx * mu[i] for i in range(6)]
    r = mm2(xr, w_rkv[0])
    k = mm2(xk, w_rkv[1])
    v = mm2(xv, w_rkv[2])
    w_log = -jax.nn.softplus(-(w0 + lora(xw, w1, w2, jnp.tanh))) - 0.5
    lw = -jnp.exp(w_log)
    if vres is None:
        v_first = v
    else:
        v0, v1, v2 = vres
        v = v + (v_first - v) * jax.nn.sigmoid(v0 + lora(xv, v1, v2, lambda z: z))
    a = jax.nn.sigmoid(a0 + lora(xa, a1, a2, lambda z: z))
    gate = lora(xg, g1, g2, jax.nn.sigmoid)

    def heads(z):
        return z.reshape(bsz, lp, RWKV_HEADS, RWKV_HEAD)

    kk = heads(k * k_k)
    kk = (kk / jnp.maximum(jnp.linalg.norm(kk, axis=-1, keepdims=True), 1e-12)).reshape(bsz, lp, w)
    k = k * (1.0 + (a - 1.0) * k_a)
    y = _wkv7(r, lw, k, v, -kk, kk * a)
    yh = heads(y)
    m = jnp.mean(yh, axis=-1, keepdims=True)
    var = jnp.mean(jnp.square(yh - m), axis=-1, keepdims=True)
    y = ((yh - m) * lax.rsqrt(var + RWKV_GN_EPS)).reshape(bsz, lp, w) * ln_w + ln_b
    bonus = jnp.sum(heads(r) * heads(k) * r_k, axis=-1, keepdims=True) * heads(v)
    y = y + bonus.reshape(bsz, lp, w)
    return (y * gate).astype(BF16), v_first


def _s5_branch(u, a_re, a_im, b_re, b_im, c_re, c_im, d_skip, log_step, w_glu):
    bsz, lp, w = u.shape
    t = bsz * lp
    bblk, cblk, a_bar = _s5_weights(a_re, a_im, b_re, b_im, c_re, c_im, log_step)
    bu = _mm(u.reshape(t, w).astype(BF16), bblk)
    hs = _s5_scan(bu.reshape(bsz, lp, S5_ROWS, S5_COLS), a_bar)
    y = _mm(hs.reshape(t, S5_ROWS * S5_COLS), cblk).reshape(bsz, lp, w) + d_skip * u
    y = jax.nn.gelu(y)
    val, gt = jnp.split(_mm(y.reshape(t, w).astype(BF16), w_glu.astype(BF16)), 2, axis=-1)
    return (val * jax.nn.sigmoid(gt)).astype(BF16).reshape(bsz, lp, w)


def _moe_ffn(h32, hb, w_router, b_router, w_gate, w_up, w_down):
    t, d = hb.shape
    logits = _mm(h32, _pad_cols(w_router, LANE), precision=lax.Precision.HIGHEST)[:, :N_EXPERTS] + b_router
    top_vals, top_idx = lax.top_k(logits, TOP_K)
    weights = jax.nn.softmax(top_vals, axis=-1)
    combine = jnp.sum(jax.nn.one_hot(top_idx, N_EXPERTS, dtype=F32) * weights[..., None], axis=-2)
    act = _swiglu_up(hb, w_gate.astype(BF16), w_up.astype(BF16), combine.T[:, :, None])
    return _mm(act, w_down.reshape(-1, d).astype(BF16))


def kernel(x, meta_tokens, w_in, b_forget, w_gate_down, w_gate_up, b_gate, w_branch, w_out, ln_mix_g, ln_mix_b, ln_ffn_g, ln_ffn_b, rwkv_mu, rwkv_w_rkv, rwkv_w0, rwkv_w1, rwkv_w2, rwkv_a0, rwkv_a1, rwkv_a2, rwkv_v0, rwkv_v1, rwkv_v2, rwkv_g1, rwkv_g2, rwkv_k_k, rwkv_k_a, rwkv_r_k, rwkv_ln_w, rwkv_ln_b, s5_a_re, s5_a_im, s5_b_re, s5_b_im, s5_c_re, s5_c_im, s5_d, s5_log_step, s5_w_glu, ffn_w_gate, ffn_w_up, ffn_w_down, moe_router, moe_router_b, moe_w_gate, moe_w_up, moe_w_down):
    bsz, seq, d = x.shape
    depth = w_in.shape[0]
    w = BRANCH_WIDTH
    lp = N_PAD + N_META + seq
    t = bsz * lp
    meta = jnp.broadcast_to(meta_tokens.astype(x.dtype)[None], (bsz, N_META, d))
    h = jnp.concatenate([jnp.zeros((bsz, N_PAD, d), x.dtype), meta, x], axis=1)
    hb = h.astype(BF16)
    real = (jnp.arange(lp) >= N_PAD)[None, :, None]
    f_lo = 7 * w
    v_first = None
    for layer in range(depth):
        wl = w_in[layer]
        w_main = jnp.concatenate([wl[:, :f_lo], wl[:, f_lo + FOX_HEADS:]], axis=1).astype(BF16)
        w_small = _pad_cols(jnp.concatenate([w_gate_down[layer], wl[:, f_lo:f_lo + FOX_HEADS]], axis=1),
                            GATE_RANK + LANE).astype(BF16)
        hb2 = hb.reshape(t, d)
        proj = _mm(hb2, w_main).reshape(bsz, lp, 8 * w)
        small = _mm(hb2, w_small)
        glr = small[:, :GATE_RANK].astype(BF16)
        f_logit = small[:, GATE_RANK:GATE_RANK + FOX_HEADS].reshape(bsz, lp, FOX_HEADS)

        o_ret = _retention(proj)

        vres = None if layer == 0 else (rwkv_v0[layer - 1], rwkv_v1[layer - 1], rwkv_v2[layer - 1])
        o_rwkv, v_first = _rwkv_branch(proj[..., 3 * w:4 * w], v_first, vres, rwkv_mu[layer], rwkv_w_rkv[layer],
                                       rwkv_w0[layer], rwkv_w1[layer], rwkv_w2[layer],
                                       rwkv_a0[layer], rwkv_a1[layer], rwkv_a2[layer],
                                       rwkv_g1[layer], rwkv_g2[layer], rwkv_k_k[layer], rwkv_k_a[layer],
                                       rwkv_r_k[layer], rwkv_ln_w[layer], rwkv_ln_b[layer])

        log_f = jnp.where(real, jax.nn.log_sigmoid(f_logit + b_forget[layer]), 0.0)
        c = jnp.cumsum(log_f, axis=1).transpose(0, 2, 1)
        o_fox = _fox(proj[..., 4 * w:7 * w].astype(BF16), c)

        o_s5 = _s5_branch(proj[..., 7 * w:8 * w], s5_a_re[layer], s5_a_im[layer], s5_b_re[layer], s5_b_im[layer],
                          s5_c_re[layer], s5_c_im[layer], s5_d[layer], s5_log_step[layer], s5_w_glu[layer])

        merged = _merge([o.reshape(t, w) for o in (o_ret, o_rwkv, o_fox, o_s5)],
                        w_branch[layer].astype(BF16), glr, w_gate_up[layer].astype(BF16), b_gate[layer])
        mix = _mm(merged, w_out[layer].astype(BF16)).reshape(bsz, lp, d)
        h, hb = _res_ln(mix, h, ln_mix_g[layer], ln_mix_b[layer], mask_pad=False)

        j = layer // 2
        hb2 = hb.reshape(t, d)
        if layer % 2 == 0:
            act = _swiglu_up(hb2, ffn_w_gate[j][None].astype(BF16), ffn_w_up[j][None].astype(BF16))
            ffn = _mm(act, ffn_w_down[j].astype(BF16))
        else:
            ffn = _moe_ffn(h.reshape(t, d), hb2, moe_router[j], moe_router_b[j],
                           moe_w_gate[j], moe_w_up[j], moe_w_down[j])
        h, hb = _res_ln(ffn.reshape(bsz, lp, d), h, ln_ffn_g[layer], ln_ffn_b[layer], mask_pad=True)
    return h[:, N_PAD + N_META:]
```

```python
import functools
import math

import jax
import jax.numpy as jnp
from jax import lax
from jax.experimental import pallas as pl
from jax.experimental.pallas import tpu as pltpu

F32 = jnp.float32
BF16 = jnp.bfloat16

D_MODEL = 4096
DEPTH = 4
N_META = 16
BLOCK = 128
N_PAD = (-N_META) % BLOCK
BRANCH_WIDTH = D_MODEL // 4
GATE_RANK = 256
RET_DK = 64
RET_DV = 128
RET_HEADS = BRANCH_WIDTH // RET_DV
RWKV_HEAD = 64
RWKV_HEADS = BRANCH_WIDTH // RWKV_HEAD
RWKV_GN_EPS = 64e-5
FOX_DH = 128
FOX_HEADS = BRANCH_WIDTH // FOX_DH
S5_GROUP = 16
S5_GROUPS = BRANCH_WIDTH // S5_GROUP
S5_STATE = 64
N_EXPERTS = 8
TOP_K = 2
ALPHA = (2.0 * DEPTH) ** 0.25
LN_EPS = 1e-5
NEG_INF = -1e30

V7X_VMEM_BUDGET = 58 * 1024 * 1024
LANE = 128
RWKV_CHUNK = 64
RWKV_QUAD = 4
RWKV_QW = RWKV_QUAD * RWKV_HEAD
FOX_TK = 256
FOX_SUB = 128
FOX_TQ = 768
ROW_TILES = (1408, 1536, 1280, 768, 640, 512, 256, 128)


def _pick(dim, candidates):
    for c in candidates:
        if dim % c == 0:
            return c
    return dim


def _params(sem, est_bytes):
    limit = int(min(V7X_VMEM_BUDGET, max(est_bytes * 5 // 4 + (2 << 20), 16 << 20)))
    return pltpu.CompilerParams(dimension_semantics=sem, vmem_limit_bytes=limit)


def _nbytes(shape, dtype):
    n = jnp.dtype(dtype).itemsize
    for s in shape:
        if s is not None:
            n *= s
    return n


def _mmf_kernel(*refs, nb, nx, nk, ep, precision):
    a_ref = refs[0]
    b_refs = refs[1:1 + nb]
    x_refs = refs[1 + nb:1 + nb + nx]
    o_ref = refs[1 + nb + nx]
    acc_refs = refs[2 + nb + nx:]
    a = a_ref[...]
    if precision is None and a.dtype != BF16:
        a = a.astype(BF16)
    dots = [jnp.dot(a, b[...], preferred_element_type=F32, precision=precision) for b in b_refs]

    def finish(vals):
        o_ref[...] = ep(*vals, *[x[...] for x in x_refs]).astype(o_ref.dtype)

    if nk == 1:
        finish(dots)
        return
    k = pl.program_id(2)

    @pl.when(k == 0)
    def _():
        for acc, d in zip(acc_refs, dots):
            acc[...] = d

    @pl.when(k > 0)
    def _():
        for acc, d in zip(acc_refs, dots):
            acc[...] += d

    @pl.when(k == nk - 1)
    def _():
        finish([acc[...] for acc in acc_refs])


def _mmf(a_spec, b_specs, x_specs, ep, *, m, n, kd, tm, tn, tk, out_dtype, precision=None, name="mm"):
    nk = kd // tk
    specs = [a_spec] + list(b_specs) + list(x_specs)
    est = 2 * sum(_nbytes(s[1], s[0].dtype) for s in specs) + 2 * tm * tn * jnp.dtype(out_dtype).itemsize
    est += (len(b_specs) + 2) * tm * tn * 4
    scratch = [pltpu.VMEM((tm, tn), F32) for _ in b_specs] if nk > 1 else []
    return pl.pallas_call(
        functools.partial(_mmf_kernel, nb=len(b_specs), nx=len(x_specs), nk=nk, ep=ep, precision=precision),
        name=name,
        grid=(m // tm, n // tn, nk),
        in_specs=[pl.BlockSpec(s[1], s[2]) for s in specs],
        out_specs=pl.BlockSpec((tm, tn), lambda i, j, k: (i, j)),
        out_shape=jax.ShapeDtypeStruct((m, n), out_dtype),
        scratch_shapes=scratch,
        compiler_params=_params(("parallel", "parallel", "arbitrary"), est),
    )(*[s[0] for s in specs])


def _ident(x):
    return x


def _mm(a, b, out_dtype=F32, precision=None, ep=_ident, extras=(), a_blk=0, name="mm"):
    m = a.shape[0]
    kd, n = b.shape
    tm = _pick(m, ROW_TILES)
    tn = _pick(n, (1024, 512, 384, 256, 128))
    tk = kd if kd <= 4096 else _pick(kd, (2048, 1536, 1024, 512))
    if precision is not None:
        tm = _pick(m, (640, 512, 256, 128))
        tn = _pick(n, (512, 384, 256, 128))
        tk = kd if kd <= 2048 else _pick(kd, (2048, 1024, 512))
    nk = kd // tk
    ob = jnp.dtype(out_dtype).itemsize

    def need(tn_):
        return (2 * (tm * tk * a.dtype.itemsize + tk * tn_ * b.dtype.itemsize + tm * tn_ * ob)
                + (3 if nk > 1 else 2) * tm * tn_ * 4)

    while need(tn) > V7X_VMEM_BUDGET - (6 << 20) and tn % 256 == 0 and n % (tn // 2) == 0:
        tn //= 2
    a_spec = (a, (tm, tk), lambda i, j, k: (i, a_blk * nk + k))
    b_spec = (b, (tk, tn), lambda i, j, k: (k, j))
    x_specs = []
    for arr, kind, off in extras:
        if kind == "tile":
            x_specs.append((arr, (tm, tn), lambda i, j, k, off=off: (i, j + off)))
        else:
            x_specs.append((arr, (1, tn), lambda i, j, k, off=off: (0, j + off)))
    return _mmf(a_spec, [b_spec], x_specs, ep, m=m, n=n, kd=kd, tm=tm, tn=tn, tk=tk,
                out_dtype=out_dtype, precision=precision, name=name)


def _silu_mul(g, u):
    return g * jax.nn.sigmoid(g) * u


def _silu_mul_scaled(g, u, s):
    return g * jax.nn.sigmoid(g) * u * s


def _swiglu_up(h, wg, wu, scale=None):
    m, d = h.shape
    e, _, f = wg.shape
    tm = _pick(m, ROW_TILES)
    tn = _pick(f, (512, 256, 128))
    nf = f // tn
    wmap = lambda i, j, k: (j // nf, 0, j % nf)
    x_specs, ep = [], _silu_mul
    if scale is not None:
        x_specs = [(scale, (None, tm, 1), lambda i, j, k: (j // nf, i, 0))]
        ep = _silu_mul_scaled
    return _mmf((h, (tm, d), lambda i, j, k: (i, 0)),
                [(wg, (None, d, tn), wmap), (wu, (None, d, tn), wmap)], x_specs, ep,
                m=m, n=e * f, kd=d, tm=tm, tn=tn, tk=d, out_dtype=BF16, name="swiglu_up")


def _glu_ep(val, gt):
    return val * jax.nn.sigmoid(gt)


def _glu(y, w_glu):
    m, kd = y.shape
    n = w_glu.shape[1] // 2
    tm = _pick(m, ROW_TILES)
    tn = _pick(n, (512, 256, 128))
    nj = n // tn
    return _mmf((y, (tm, kd), lambda i, j, k: (i, 0)),
                [(w_glu, (kd, tn), lambda i, j, k: (0, j)), (w_glu, (kd, tn), lambda i, j, k: (0, nj + j))],
                [], _glu_ep, m=m, n=n, kd=kd, tm=tm, tn=tn, tk=kd, out_dtype=BF16, name="s5_glu")


def _res_ln_kernel(y_ref, r_ref, g_ref, b_ref, o_ref, ob_ref, *, tm, mask_pad):
    z = ALPHA * r_ref[...] + y_ref[...]
    mu = jnp.mean(z, axis=-1, keepdims=True)
    zc = z - mu
    var = jnp.mean(zc * zc, axis=-1, keepdims=True)
    out = zc * lax.rsqrt(var + LN_EPS) * g_ref[...] + b_ref[...]
    if mask_pad is not None:
        pos = pl.program_id(1) * tm + lax.broadcasted_iota(jnp.int32, (tm, 1), 0)
        out = jnp.where((pos >= mask_pad[0]) & (pos < mask_pad[1]), out, 0.0)
    o_ref[...] = out
    ob_ref[...] = out.astype(BF16)


def _res_ln(y, res, g, b, mask_pad):
    bsz, lp, d = res.shape
    tm = _pick(lp, (256, 128, 64, 32, 16, 8))
    est = 2 * (4 * tm * d * 4) + 4 * tm * d * 4
    spec = pl.BlockSpec((None, tm, d), lambda bi, i: (bi, i, 0))
    vspec = pl.BlockSpec((1, d), lambda bi, i: (0, 0))
    return pl.pallas_call(
        functools.partial(_res_ln_kernel, tm=tm, mask_pad=mask_pad),
        name="res_ln",
        grid=(bsz, lp // tm),
        in_specs=[spec, spec, vspec, vspec],
        out_specs=[spec, spec],
        out_shape=[jax.ShapeDtypeStruct((bsz, lp, d), F32), jax.ShapeDtypeStruct((bsz, lp, d), BF16)],
        compiler_params=_params(("parallel", "parallel"), est),
    )(y, res, g.reshape(1, d), b.reshape(1, d))


def _merge_kernel(o0, o1, o2, o3, wb_ref, glr_ref, wgu_ref, bg_ref, out_ref):
    glr = glr_ref[...].astype(BF16)
    acc = None
    for i, o_ref in enumerate((o0, o1, o2, o3)):
        gate = jax.nn.sigmoid(jnp.dot(glr, wgu_ref[i], preferred_element_type=F32) + bg_ref[i])
        term = gate * jnp.dot(o_ref[...], wb_ref[i], preferred_element_type=F32)
        acc = term if acc is None else acc + term
    out_ref[...] = acc.astype(out_ref.dtype)


def _merge(outs, w_branch, small, w_gate_up, b_gate):
    m, w = outs[0].shape
    nb, r, d = w_gate_up.shape
    tm = _pick(m, ROW_TILES)
    tn = _pick(d, (512, 256, 128))
    est = 2 * (nb * tm * w * 2 + nb * w * tn * 2 + tm * r * 4 + nb * r * tn * 2 + tm * tn * 2) + 4 * tm * tn * 4
    ospec = pl.BlockSpec((tm, w), lambda i, j: (i, 0))
    return pl.pallas_call(
        _merge_kernel,
        name="merge",
        grid=(m // tm, d // tn),
        in_specs=[ospec, ospec, ospec, ospec,
                  pl.BlockSpec((nb, w, tn), lambda i, j: (0, 0, j)),
                  pl.BlockSpec((tm, r), lambda i, j: (i, 0)),
                  pl.BlockSpec((nb, r, tn), lambda i, j: (0, 0, j)),
                  pl.BlockSpec((nb, 1, tn), lambda i, j: (0, 0, j))],
        out_specs=pl.BlockSpec((tm, tn), lambda i, j: (i, j)),
        out_shape=jax.ShapeDtypeStruct((m, d), BF16),
        compiler_params=_params(("parallel", "parallel"), est),
    )(*outs, w_branch, small, w_gate_up, b_gate.reshape(nb, 1, d))


def _ret_consts(lp):
    f32 = F32
    half = RET_DK // 2
    inv = 1.0 / (10000.0 ** jnp.linspace(0.0, 1.0, half, dtype=f32))
    ang = jnp.arange(lp, dtype=f32)[:, None] * inv[None, :]
    cos, sin = jnp.cos(ang), jnp.sin(ang)
    cos_t = jnp.tile(jnp.concatenate([cos, cos], axis=-1), (1, RET_HEADS))
    sin_t = jnp.tile(jnp.concatenate([-sin, sin], axis=-1), (1, RET_HEADS))
    log_gamma = jnp.log(1.0 - 2.0 ** (-5.0 - jnp.arange(RET_HEADS, dtype=f32)))
    pos = jnp.arange(BLOCK, dtype=f32)
    rel = pos[:, None] - pos[None, :]
    dmat = jnp.where(rel >= 0, jnp.exp(log_gamma[:, None, None] * jnp.maximum(rel, 0.0)), 0.0)
    k_decay = jnp.exp(log_gamma[:, None] * (BLOCK - 1.0 - pos)[None, :])
    q_decay = jnp.exp(log_gamma[:, None] * (pos + 1.0)[None, :])
    kd_t = jnp.repeat(k_decay.T, RET_DK, axis=1)
    qd_t = jnp.repeat(q_decay.T, RET_DK, axis=1)
    cd = jnp.exp(log_gamma * BLOCK)
    cd_t = jnp.broadcast_to(cd[:, None, None], (RET_HEADS, 1, RET_DV))
    return cos_t, sin_t, dmat, qd_t, kd_t, cd_t


def _ret_kernel(q_ref, k_ref, v_ref, g_ref, cos_ref, sin_ref, dmat_ref, qd_ref, kd_ref, cd_ref,
                o_ref, st_ref):
    @pl.when(pl.program_id(1) == 0)
    def _():
        st_ref[...] = jnp.zeros_like(st_ref)

    cos = cos_ref[...]
    sin = sin_ref[...]
    width = RET_HEADS * RET_DK
    lane = lax.broadcasted_iota(jnp.int32, (BLOCK, width), 1)
    first = (lane % RET_DK) < (RET_DK // 2)

    def rot(x):
        sw = jnp.where(first, pltpu.roll(x, width - RET_DK // 2, 1), pltpu.roll(x, RET_DK // 2, 1))
        return x * cos + sw * sin

    q = rot(q_ref[...])
    k = rot(k_ref[...]) * RET_DK ** -0.5
    qb = q.astype(BF16)
    kb = k.astype(BF16)
    qs = (q * qd_ref[...]).astype(BF16)
    ks = (k * kd_ref[...]).astype(BF16)
    for h in range(RET_HEADS):
        ksl = slice(h * RET_DK, (h + 1) * RET_DK)
        vsl = slice(h * RET_DV, (h + 1) * RET_DV)
        vh = v_ref[:, vsl].astype(BF16)
        s = lax.dot_general(qb[:, ksl], kb[:, ksl], (((1,), (1,)), ((), ())),
                            preferred_element_type=F32) * dmat_ref[h]
        st = st_ref[h]
        y = (jnp.dot(s.astype(BF16), vh, preferred_element_type=F32)
             + jnp.dot(qs[:, ksl], st.astype(BF16), preferred_element_type=F32))
        st_ref[h] = st * cd_ref[h] + lax.dot_general(ks[:, ksl], vh, (((0,), (0,)), ((), ())),
                                                     preferred_element_type=F32)
        mu = jnp.mean(y, axis=-1, keepdims=True)
        yc = y - mu
        var = jnp.mean(yc * yc, axis=-1, keepdims=True)
        gh = g_ref[:, vsl]
        o_ref[:, vsl] = (gh * jax.nn.sigmoid(gh) * yc * lax.rsqrt(var + 1e-6)).astype(o_ref.dtype)


def _retention(proj):
    bsz, lp = proj.shape[:2]
    kw = RET_HEADS * RET_DK
    cos_t, sin_t, dmat, qd_t, kd_t, cd_t = _ret_consts(lp)
    est = 2 * BLOCK * 4 * (2 * kw + 2 * BRANCH_WIDTH + 4 * kw) + 2 * RET_HEADS * BLOCK * BLOCK * 4 + (8 << 20)
    tab = pl.BlockSpec((BLOCK, kw), lambda b, n: (n, 0))
    ctab = pl.BlockSpec((BLOCK, kw), lambda b, n: (0, 0))
    return pl.pallas_call(
        _ret_kernel,
        name="retention",
        grid=(bsz, lp // BLOCK),
        in_specs=[pl.BlockSpec((None, BLOCK, kw), lambda b, n: (b, n, 0)),
                  pl.BlockSpec((None, BLOCK, kw), lambda b, n: (b, n, 1)),
                  pl.BlockSpec((None, BLOCK, BRANCH_WIDTH), lambda b, n: (b, n, 1)),
                  pl.BlockSpec((None, BLOCK, BRANCH_WIDTH), lambda b, n: (b, n, 2)),
                  tab, tab,
                  pl.BlockSpec((RET_HEADS, BLOCK, BLOCK), lambda b, n: (0, 0, 0)),
                  ctab, ctab,
                  pl.BlockSpec((RET_HEADS, 1, RET_DV), lambda b, n: (0, 0, 0))],
        out_specs=pl.BlockSpec((None, BLOCK, BRANCH_WIDTH), lambda b, n: (b, n, 0)),
        out_shape=jax.ShapeDtypeStruct((bsz, lp, BRANCH_WIDTH), BF16),
        scratch_shapes=[pltpu.VMEM((RET_HEADS, RET_DK, RET_DV), F32)],
        compiler_params=_params(("parallel", "arbitrary"), est),
    )(proj, proj, proj, proj, cos_t, sin_t, dmat, qd_t, kd_t, cd_t)


def _split_dot(x, ones):
    hi = x.astype(BF16)
    lo = (x - hi.astype(F32)).astype(BF16)
    return (jnp.dot(hi, ones, preferred_element_type=F32) + jnp.dot(lo, ones, preferred_element_type=F32))


_RV_MU, _RV_W0, _RV_A0, _RV_V0, _RV_KK, _RV_KA, _RV_RK = 0, 6, 7, 8, 9, 10, 11
_RV_ROWS = 16


def _rwkv_pre_kernel(*refs, has_vres):
    (u_ref, up_ref, wr_ref, wk_ref, wv_ref, w1_ref, w2_ref, a1_ref, a2_ref, g1_ref, g2_ref,
     ones_ref, vec_ref) = refs[:13]
    pos = 13
    if has_vres:
        vf_ref, v1_ref, v2_ref = refs[13:16]
        pos = 16
    r_out, lw_out, k_out, v_out, a_out, b_out, gate_out, bonus_out = refs[pos:pos + 8]

    def vec(i):
        return vec_ref[i:i + 1, :]

    u = u_ref[...]
    tm = u.shape[0]
    prev_last = jnp.where(pl.program_id(1) == 0, 0.0, up_ref[7:8, :])
    row = lax.broadcasted_iota(jnp.int32, (tm, 1), 0)
    prev = jnp.where(row == 0, prev_last, pltpu.roll(u, 1, 0))
    xx = prev - u

    def mix(i):
        return (u + xx * vec(_RV_MU + i)).astype(BF16)

    def mm(x, w_ref):
        return jnp.dot(x, w_ref[...], preferred_element_type=F32)

    xr, xw, xk, xv, xa, xg = [mix(i) for i in range(6)]
    r = mm(xr, wr_ref)
    k = mm(xk, wk_ref)
    v = mm(xv, wv_ref)
    z = -(vec(_RV_W0) + mm(jnp.tanh(mm(xw, w1_ref)).astype(BF16), w2_ref))
    softplus = jnp.maximum(z, 0.0) + jnp.log(1.0 + jnp.exp(-jnp.abs(z)))
    lw_out[...] = -jnp.exp(-softplus - 0.5)
    if has_vres:
        mv = jax.nn.sigmoid(vec(_RV_V0) + mm(mm(xv, v1_ref).astype(BF16), v2_ref))
        v = v + (vf_ref[...] - v) * mv
    a = jax.nn.sigmoid(vec(_RV_A0) + mm(mm(xa, a1_ref).astype(BF16), a2_ref))
    gate_out[...] = mm(jax.nn.sigmoid(mm(xg, g1_ref)).astype(BF16), g2_ref)

    ones = ones_ref[...]
    kk = k * vec(_RV_KK)
    kk = kk / jnp.maximum(jnp.sqrt(_split_dot(kk * kk, ones)), 1e-12)
    k = k * (1.0 + (a - 1.0) * vec(_RV_KA))
    r_out[...] = r
    k_out[...] = k
    v_out[...] = v
    a_out[...] = -kk
    b_out[...] = kk * a
    bonus_out[...] = _split_dot(r * k * vec(_RV_RK), ones) * v


def _rwkv_pre(proj, u_blk, v_first, vres, mu, w_rkv, w0, w1, w2, a0, a1, a2, g1, g2, k_k, k_a, r_k):
    bsz, lp = proj.shape[:2]
    w = BRANCH_WIDTH
    tm = _pick(lp, (256, 128, 64, 32, 16, 8))
    has_vres = vres is not None

    def lora_pair(wa, wb):
        rp = -(-wa.shape[1] // LANE) * LANE
        return (jnp.pad(wa, ((0, 0), (0, rp - wa.shape[1]))).astype(BF16),
                jnp.pad(wb, ((0, rp - wb.shape[0]), (0, 0))).astype(BF16))

    w1p, w2p = lora_pair(w1, w2)
    a1p, a2p = lora_pair(a1, a2)
    g1p, g2p = lora_pair(g1, g2)
    hid = jnp.arange(w) // RWKV_HEAD
    ones = (hid[:, None] == hid[None, :]).astype(BF16)
    zero = jnp.zeros((w,), F32)
    rows = [mu[i] for i in range(6)] + [w0, a0, vres[0] if has_vres else zero, k_k, k_a, r_k.reshape(w)]
    vec = jnp.stack(rows + [zero] * (_RV_ROWS - len(rows)))

    tile = pl.BlockSpec((None, tm, w), lambda b, i: (b, i, 0))
    const = lambda arr: pl.BlockSpec(arr.shape, lambda b, i: (0,) * arr.ndim)
    weights = [w_rkv[0].astype(BF16), w_rkv[1].astype(BF16), w_rkv[2].astype(BF16),
               w1p, w2p, a1p, a2p, g1p, g2p, ones, vec]
    args = [proj, proj] + weights
    in_specs = [pl.BlockSpec((None, tm, w), lambda b, i: (b, i, u_blk)),
                pl.BlockSpec((None, 8, w), lambda b, i: (b, jnp.maximum(i * (tm // 8) - 1, 0), u_blk))]
    in_specs += [const(x) for x in weights]
    if has_vres:
        v1p, v2p = lora_pair(vres[1], vres[2])
        args += [v_first, v1p, v2p]
        in_specs += [tile, const(v1p), const(v2p)]
    est = 2 * sum(x.size * x.dtype.itemsize for x in weights) + 2 * 11 * tm * w * 4 + 40 * tm * w * 4
    outs = pl.pallas_call(
        functools.partial(_rwkv_pre_kernel, has_vres=has_vres),
        name="rwkv_pre",
        grid=(bsz, lp // tm),
        in_specs=in_specs,
        out_specs=[tile] * 8,
        out_shape=[jax.ShapeDtypeStruct((bsz, lp, w), F32)] * 8,
        compiler_params=_params(("parallel", "parallel"), est),
    )(*args)
    return outs


def _wkv_kernel(r_ref, lw_ref, k_ref, v_ref, a_ref, b_ref, gate_ref, bonus_ref, lnw_ref, lnb_ref,
                o_ref, ht_ref):
    c = RWKV_CHUNK
    wd = RWKV_QW

    @pl.when(pl.program_id(1) == 0)
    def _():
        ht_ref[...] = jnp.zeros_like(ht_ref)

    ti = lax.broadcasted_iota(jnp.int32, (c, c), 0)
    tj = lax.broadcasted_iota(jnp.int32, (c, c), 1)
    cum_all = jnp.dot((ti >= tj).astype(F32), lw_ref[...], preferred_element_type=F32,
                      precision=lax.Precision.HIGHEST)

    ri = lax.broadcasted_iota(jnp.int32, (wd, wd), 0)
    ci = lax.broadcasted_iota(jnp.int32, (wd, wd), 1)
    same = (ri // c) == (ci // RWKV_HEAD)
    strict = same & ((ci % c) < (ri % c))
    incl = same & ((ci % c) <= (ri % c))
    ones_blk = jnp.where(same, 1.0, 0.0).astype(BF16)

    def expand(x):
        xb = x.astype(BF16)
        return jnp.where(same, jnp.concatenate([xb] * RWKV_QUAD, axis=0), jnp.zeros((), BF16))

    def nt(x, y):
        return lax.dot_general(x, y, (((1,), (1,)), ((), ())), preferred_element_type=F32)

    def tn(x, y):
        return lax.dot_general(x, y, (((0,), (0,)), ((), ())), preferred_element_type=F32)

    def mm(x, y):
        return jnp.dot(x, y, preferred_element_type=F32)

    groups = range(r_ref.shape[1] // wd)
    sls = [slice(q * wd, (q + 1) * wd) for q in groups]
    ar, bt, kt, vb, bw, kw, w_last = [], [], [], [], [], [], []
    for sl in sls:
        lw = lw_ref[:, sl]
        cum = cum_all[:, sl]
        cum_last = cum[c - 1:c, :]
        e_neg = jnp.exp(-cum)
        e_rem = jnp.exp(cum_last - cum)
        k = k_ref[:, sl]
        b = b_ref[:, sl]
        ar.append(jnp.concatenate([expand(a_ref[:, sl] * jnp.exp(cum - lw)),
                                   expand(r_ref[:, sl] * jnp.exp(cum))], axis=0))
        bt.append(expand(b * e_neg))
        kt.append(expand(k * e_neg))
        vb.append(expand(v_ref[:, sl]))
        bw.append(expand(b * e_rem))
        kw.append(expand(k * e_rem))
        w_last.append(jnp.exp(cum_last))

    g_b = [nt(ar[q], bt[q]) for q in groups]
    g_k = [nt(ar[q], kt[q]) for q in groups]
    hts = [ht_ref[q] for q in groups]
    g_h = [nt(ar[q], hts[q].astype(BF16)) for q in groups]
    p = [jnp.where(strict, g_b[q][:wd], 0.0) for q in groups]
    a_ak = [jnp.where(strict, g_k[q][:wd], 0.0).astype(BF16) for q in groups]
    a_rb = [jnp.where(incl, g_b[q][wd:], 0.0).astype(BF16) for q in groups]
    a_rk = [jnp.where(incl, g_k[q][wd:], 0.0).astype(BF16) for q in groups]
    u = [g_h[q][:wd] + mm(a_ak[q], vb[q]) for q in groups]
    steps = int(math.log2(c))
    for i in range(steps):
        pb = [x.astype(BF16) for x in p]
        u = [u[q] + mm(pb[q], u[q].astype(BF16)) for q in groups]
        if i < steps - 1:
            p = [mm(pb[q], pb[q]) for q in groups]
    ub = [x.astype(BF16) for x in u]
    y_bd = [g_h[q][wd:] + mm(a_rb[q], ub[q]) + mm(a_rk[q], vb[q]) for q in groups]
    for q in groups:
        ht_ref[q] = hts[q] * w_last[q] + tn(ub[q], bw[q]) + tn(vb[q], kw[q])
    for q in groups:
        sl = sls[q]
        y = y_bd[q][0:c]
        for h in range(1, RWKV_QUAD):
            y = y + y_bd[q][h * c:(h + 1) * c]
        mean = _split_dot(y, ones_blk) * (1.0 / RWKV_HEAD)
        yc = y - mean
        var = _split_dot(yc * yc, ones_blk) * (1.0 / RWKV_HEAD)
        out = yc * lax.rsqrt(var + RWKV_GN_EPS) * lnw_ref[:, sl] + lnb_ref[:, sl] + bonus_ref[:, sl]
        o_ref[:, sl] = (out * gate_ref[:, sl]).astype(o_ref.dtype)


def _wkv7(r, lw, k, v, a, b, gate, bonus, ln_w, ln_b):
    bsz, lp, w = r.shape
    c = RWKV_CHUNK
    spec = pl.BlockSpec((None, c, w), lambda bi, n: (bi, n, 0))
    vspec = pl.BlockSpec((1, w), lambda bi, n: (0, 0))
    est = 2 * 9 * c * w * 4 + 4 * 40 * RWKV_QW * RWKV_QW * 4
    return pl.pallas_call(
        _wkv_kernel,
        name="wkv7",
        grid=(bsz, lp // c),
        in_specs=[spec] * 8 + [vspec, vspec],
        out_specs=spec,
        out_shape=jax.ShapeDtypeStruct((bsz, lp, w), BF16),
        scratch_shapes=[pltpu.VMEM((w // RWKV_QW, RWKV_QW, RWKV_QW), F32)],
        compiler_params=_params(("parallel", "arbitrary"), est),
    )(r, lw, k, v, a, b, gate, bonus, ln_w.reshape(1, w), ln_b.reshape(1, w))


def _fox_kernel(qt_ref, k_ref, vt_ref, ck_ref, o_ref):
    tk, sw = FOX_TK, FOX_SUB
    tq = qt_ref.shape[1]
    nsub = tq // sw
    qi = pl.program_id(2)
    row = lax.broadcasted_iota(jnp.int32, (tk, sw), 0)
    col = lax.broadcasted_iota(jnp.int32, (tk, sw), 1)
    diag_masks = (row <= col, row <= col + sw)

    def scores(kb, r_lo):
        ks = pl.multiple_of(kb * tk, tk)
        return jnp.dot(k_ref[pl.ds(ks, tk), :], qt_ref[:, r_lo * sw:], preferred_element_type=F32)

    def softmax(kb, st, ml, r_lo, masked):
        ml, alphas, ps = list(ml), [], []
        ck = ck_ref[pl.ds(pl.multiple_of(kb * tk, tk), tk), :]
        for r in range(r_lo, nsub):
            s = st[:, (r - r_lo) * sw:(r - r_lo + 1) * sw] - ck
            if masked and r - r_lo < 2:
                s = jnp.where(diag_masks[r - r_lo], s, NEG_INF)
            m, l = ml[r]
            m_new = jnp.maximum(m, jnp.max(s, axis=0, keepdims=True))
            alpha = jnp.exp(m - m_new)
            p = jnp.exp(s - m_new)
            ml[r] = (m_new, alpha * l + jnp.sum(p, axis=0, keepdims=True))
            alphas.append(alpha)
            ps.append(p.astype(BF16))
        return tuple(ml), tuple(alphas), jnp.concatenate(ps, axis=1)

    def apply_pv(kb, pt, alphas, accs, r_lo):
        ks = pl.multiple_of(kb * tk, tk)
        pv = jnp.dot(vt_ref[:, pl.ds(ks, tk)], pt, preferred_element_type=F32)
        accs = list(accs)
        for r in range(r_lo, nsub):
            accs[r] = alphas[r - r_lo] * accs[r] + pv[:, (r - r_lo) * sw:(r - r_lo + 1) * sw]
        return tuple(accs)

    def body(kb, carry):
        st, pt_prev, alphas_prev, ml, accs = carry
        st_next = scores(kb + 1, 0)
        accs = apply_pv(jnp.maximum(kb - 1, 0), pt_prev, alphas_prev, accs, 0)
        ml, alphas, pt = softmax(kb, st, ml, 0, False)
        return st_next, pt, alphas, ml, accs

    per_q = tq // tk
    n_main = qi * per_q
    init = (scores(0, 0), jnp.zeros((tk, tq), BF16),
            tuple(jnp.ones((1, sw), F32) for _ in range(nsub)),
            tuple((jnp.full((1, sw), NEG_INF, F32), jnp.zeros((1, sw), F32)) for _ in range(nsub)),
            tuple(jnp.zeros((FOX_DH, sw), F32) for _ in range(nsub)))
    st, pt_prev, alphas_prev, ml, accs = lax.fori_loop(0, n_main, body, init)
    accs = apply_pv(jnp.maximum(n_main - 1, 0), pt_prev, alphas_prev, accs, 0)
    for d in range(per_q):
        r_lo = d * (tk // sw)
        if d > 0:
            st = scores(n_main + d, r_lo)
        ml, alphas, pt = softmax(n_main + d, st, ml, r_lo, True)
        accs = apply_pv(n_main + d, pt, alphas, accs, r_lo)
    for r in range(nsub):
        o_ref[:, r * sw:(r + 1) * sw] = (accs[r] / ml[r][1]).astype(o_ref.dtype)


def _fox(qkv, c, lp_real):
    bsz, lp = qkv.shape[:2]
    h, dh, w = FOX_HEADS, FOX_DH, BRANCH_WIDTH
    tq = FOX_TQ
    pos = jnp.arange(lp)
    ck = jnp.where((pos >= N_PAD) & (pos < lp_real), c, -NEG_INF)
    ck_rep = jnp.broadcast_to(ck[..., None], (bsz, h, lp, FOX_SUB))
    qt = qkv[..., :w].reshape(bsz, lp, h, dh).transpose(0, 2, 3, 1)
    kk = qkv[..., w:2 * w].reshape(bsz, lp, h, dh).transpose(0, 2, 1, 3)
    vt = qkv[..., 2 * w:].reshape(bsz, lp, h, dh).transpose(0, 2, 3, 1)
    est = 2 * (dh * tq * 2 + lp * dh * 2 + dh * lp * 2 + lp * FOX_SUB * 4 + dh * tq * 2) + (16 << 20)
    out_t = pl.pallas_call(
        _fox_kernel,
        name="fox",
        grid=(bsz, h, lp // tq),
        in_specs=[pl.BlockSpec((None, None, dh, tq), lambda b, hh, i: (b, hh, 0, i)),
                  pl.BlockSpec((None, None, lp, dh), lambda b, hh, i: (b, hh, 0, 0)),
                  pl.BlockSpec((None, None, dh, lp), lambda b, hh, i: (b, hh, 0, 0)),
                  pl.BlockSpec((None, None, lp, FOX_SUB), lambda b, hh, i: (b, hh, 0, 0))],
        out_specs=pl.BlockSpec((None, None, dh, tq), lambda b, hh, i: (b, hh, 0, i)),
        out_shape=jax.ShapeDtypeStruct((bsz, h, dh, lp), BF16),
        compiler_params=_params(("parallel", "parallel", "parallel"), est),
    )(qt, kk, vt, ck_rep)
    return out_t.transpose(0, 3, 1, 2).reshape(bsz, lp, w)


S5_ROWS = 16
S5_COLS = S5_GROUPS * S5_STATE // 8


def _s5_scan_kernel(x_ref, a_ref, o_ref, st_ref, *, tc):
    @pl.when(pl.program_id(1) == 0)
    def _():
        st_ref[...] = jnp.zeros_like(st_ref)

    ar = a_ref[0:8]
    ai = a_ref[8:16]

    def body(t, carry):
        hr, hi = carry
        x = x_ref[t]
        nr = ar * hr - ai * hi + x[0:8]
        ni = ar * hi + ai * hr + x[8:16]
        o_ref[t] = jnp.concatenate([nr, ni], axis=0).astype(o_ref.dtype)
        return nr, ni

    hr, hi = lax.fori_loop(0, tc, body, (st_ref[0:8], st_ref[8:16]), unroll=8)
    st_ref[0:8] = hr
    st_ref[8:16] = hi


def _s5_scan(bu, a_bar):
    bsz, lp = bu.shape[:2]
    tc = _pick(lp, (128, 64, 32, 16, 8))
    est = 2 * tc * S5_ROWS * S5_COLS * (4 + 2) + (4 << 20)
    return pl.pallas_call(
        functools.partial(_s5_scan_kernel, tc=tc),
        name="s5_scan",
        grid=(bsz, lp // tc),
        in_specs=[pl.BlockSpec((None, tc, S5_ROWS, S5_COLS), lambda b, n: (b, n, 0, 0)),
                  pl.BlockSpec((S5_ROWS, S5_COLS), lambda b, n: (0, 0))],
        out_specs=pl.BlockSpec((None, tc, S5_ROWS, S5_COLS), lambda b, n: (b, n, 0, 0)),
        out_shape=jax.ShapeDtypeStruct(bu.shape, BF16),
        scratch_shapes=[pltpu.VMEM((S5_ROWS, S5_COLS), F32)],
        compiler_params=_params(("parallel", "arbitrary"), est),
    )(bu, a_bar)


def _s5_weights(a_re, a_im, b_re, b_im, c_re, c_im, log_step):
    g, p, cg = S5_GROUPS, S5_STATE, S5_GROUP
    dt = jnp.exp(log_step)[:, None]
    mag = jnp.exp(a_re * dt)
    bar_re = mag * jnp.cos(a_im * dt)
    bar_im = mag * jnp.sin(a_im * dt)
    den = a_re * a_re + a_im * a_im
    num_re = bar_re - 1.0
    coef_re = (num_re * a_re + bar_im * a_im) / den
    coef_im = (bar_im * a_re - num_re * a_im) / den
    bb_re = coef_re[..., None] * b_re - coef_im[..., None] * b_im
    bb_im = coef_re[..., None] * b_im + coef_im[..., None] * b_re
    eye = jnp.eye(g, dtype=F32)
    bblk = jnp.concatenate([jnp.einsum('gpc,gh->gchp', m, eye).reshape(g * cg, g * p) for m in (bb_re, bb_im)],
                           axis=1)
    cblk = jnp.concatenate([jnp.einsum('gcp,gh->gphc', m, eye).reshape(g * p, g * cg) for m in (c_re, -c_im)],
                           axis=0)
    a_bar = jnp.concatenate([bar_re.reshape(8, S5_COLS), bar_im.reshape(8, S5_COLS)], axis=0)
    return bblk.astype(BF16), cblk.astype(BF16), a_bar


def _s5_out_ep(y, u, d):
    return jax.nn.gelu(y + d * u)


def _s5_branch(proj2, u_blk, bsz, lp, a_re, a_im, b_re, b_im, c_re, c_im, d_skip, log_step, w_glu):
    w = BRANCH_WIDTH
    t = bsz * lp
    bblk, cblk, a_bar = _s5_weights(a_re, a_im, b_re, b_im, c_re, c_im, log_step)
    bu = _mm(proj2, bblk, a_blk=u_blk, name="s5_bu")
    hs = _s5_scan(bu.reshape(bsz, lp, S5_ROWS, S5_COLS), a_bar)
    y = _mm(hs.reshape(t, S5_ROWS * S5_COLS), cblk, out_dtype=BF16, ep=_s5_out_ep,
            extras=((proj2, "tile", u_blk), (d_skip.reshape(1, w), "row", 0)), name="s5_out")
    return _glu(y, w_glu.astype(BF16))


def _pad_cols(w, n):
    return jnp.pad(w, ((0, 0), (0, n - w.shape[1])))


def _scale_cols(x, s):
    return x * s


def _moe_ffn(h32, hb, w_router, b_router, w_gate, w_up, w_down):
    t, d = hb.shape
    logits = _mm(h32, _pad_cols(w_router, LANE), precision=lax.Precision.HIGHEST,
                 name="router")[:, :N_EXPERTS] + b_router
    top_vals, top_idx = lax.top_k(logits, TOP_K)
    weights = jax.nn.softmax(top_vals, axis=-1)
    combine = jnp.sum(jax.nn.one_hot(top_idx, N_EXPERTS, dtype=F32) * weights[..., None], axis=-2)
    act = _swiglu_up(hb, w_gate.astype(BF16), w_up.astype(BF16), combine.T[:, :, None])
    return _mm(act, w_down.reshape(-1, d).astype(BF16), name="moe_down")


def kernel(x, meta_tokens, w_in, b_forget, w_gate_down, w_gate_up, b_gate, w_branch, w_out, ln_mix_g, ln_mix_b, ln_ffn_g, ln_ffn_b, rwkv_mu, rwkv_w_rkv, rwkv_w0, rwkv_w1, rwkv_w2, rwkv_a0, rwkv_a1, rwkv_a2, rwkv_v0, rwkv_v1, rwkv_v2, rwkv_g1, rwkv_g2, rwkv_k_k, rwkv_k_a, rwkv_r_k, rwkv_ln_w, rwkv_ln_b, s5_a_re, s5_a_im, s5_b_re, s5_b_im, s5_c_re, s5_c_im, s5_d, s5_log_step, s5_w_glu, ffn_w_gate, ffn_w_up, ffn_w_down, moe_router, moe_router_b, moe_w_gate, moe_w_up, moe_w_down):
    bsz, seq, d = x.shape
    depth = w_in.shape[0]
    w = BRANCH_WIDTH
    lp_real = N_PAD + N_META + seq
    lp = -(-lp_real // FOX_TQ) * FOX_TQ
    t = bsz * lp
    meta = jnp.broadcast_to(meta_tokens.astype(x.dtype)[None], (bsz, N_META, d))
    h = jnp.concatenate([jnp.zeros((bsz, N_PAD, d), x.dtype), meta, x,
                         jnp.zeros((bsz, lp - lp_real, d), x.dtype)], axis=1)
    hb = h.astype(BF16)
    pos = jnp.arange(lp)
    real = ((pos >= N_PAD) & (pos < lp_real))[None, :, None]
    f_lo = 7 * w
    q_scale = jnp.concatenate([jnp.full((1, w), FOX_DH ** -0.5, F32), jnp.ones((1, 2 * w), F32)], axis=1)
    v_first = None
    for layer in range(depth):
        wl = w_in[layer]
        w_f32 = jnp.concatenate([wl[:, :4 * w], wl[:, f_lo + FOX_HEADS:]], axis=1).astype(BF16)
        w_fox = wl[:, 4 * w:f_lo].astype(BF16)
        w_small = _pad_cols(jnp.concatenate([w_gate_down[layer], wl[:, f_lo:f_lo + FOX_HEADS]], axis=1),
                            GATE_RANK + LANE).astype(BF16)
        hb2 = hb.reshape(t, d)
        proj2 = _mm(hb2, w_f32, name="proj")
        proj = proj2.reshape(bsz, lp, 5 * w)
        qkv = _mm(hb2, w_fox, out_dtype=BF16, ep=_scale_cols, extras=((q_scale, "row", 0),),
                  name="proj_fox").reshape(bsz, lp, 3 * w)
        small = _mm(hb2, w_small, name="proj_small")
        f_logit = small[:, GATE_RANK:GATE_RANK + FOX_HEADS].reshape(bsz, lp, FOX_HEADS)

        o_ret = _retention(proj)

        vres = None if layer == 0 else (rwkv_v0[layer - 1], rwkv_v1[layer - 1], rwkv_v2[layer - 1])
        r, lw, k, v, a, b, gate, bonus = _rwkv_pre(
            proj, 3, v_first, vres, rwkv_mu[layer], rwkv_w_rkv[layer], rwkv_w0[layer], rwkv_w1[layer],
            rwkv_w2[layer], rwkv_a0[layer], rwkv_a1[layer], rwkv_a2[layer], rwkv_g1[layer], rwkv_g2[layer],
            rwkv_k_k[layer], rwkv_k_a[layer], rwkv_r_k[layer])
        if layer == 0:
            v_first = v
        o_rwkv = _wkv7(r, lw, k, v, a, b, gate, bonus, rwkv_ln_w[layer], rwkv_ln_b[layer])

        log_f = jnp.where(real, jax.nn.log_sigmoid(f_logit + b_forget[layer]), 0.0)
        c = jnp.cumsum(log_f, axis=1).transpose(0, 2, 1)
        o_fox = _fox(qkv, c, lp_real)

        o_s5 = _s5_branch(proj2, 4, bsz, lp, s5_a_re[layer], s5_a_im[layer], s5_b_re[layer], s5_b_im[layer],
                          s5_c_re[layer], s5_c_im[layer], s5_d[layer], s5_log_step[layer], s5_w_glu[layer])

        merged = _merge([o_ret.reshape(t, w), o_rwkv.reshape(t, w), o_fox.reshape(t, w), o_s5],
                        w_branch[layer].astype(BF16), small, w_gate_up[layer].astype(BF16), b_gate[layer])
        mix = _mm(merged, w_out[layer].astype(BF16), name="w_out").reshape(bsz, lp, d)
        h, hb = _res_ln(mix, h, ln_mix_g[layer], ln_mix_b[layer], mask_pad=None)

        j = layer // 2
        hb2 = hb.reshape(t, d)
        if layer % 2 == 0:
            act = _swiglu_up(hb2, ffn_w_gate[j][None].astype(BF16), ffn_w_up[j][None].astype(BF16))
            ffn = _mm(act, ffn_w_down[j].astype(BF16), name="ffn_down")
        else:
            ffn = _moe_ffn(h.reshape(t, d), hb2, moe_router[j], moe_router_b[j],
                           moe_w_gate[j], moe_w_up[j], moe_w_down[j])
        h, hb = _res_ln(ffn.reshape(bsz, lp, d), h, ln_ffn_g[layer], ln_ffn_b[layer],
                        mask_pad=(N_PAD, lp_real))
    return h[:, N_PAD + N_META:lp_real]
```

```python
import functools
import math

import jax
import jax.numpy as jnp
from jax import lax
from jax.experimental import pallas as pl
from jax.experimental.pallas import tpu as pltpu

F32 = jnp.float32
BF16 = jnp.bfloat16

D_MODEL = 4096
DEPTH = 4
N_META = 16
BLOCK = 128
N_PAD = (-N_META) % BLOCK
BRANCH_WIDTH = D_MODEL // 4
GATE_RANK = 256
RET_DK = 64
RET_DV = 128
RET_HEADS = BRANCH_WIDTH // RET_DV
RWKV_HEAD = 64
RWKV_HEADS = BRANCH_WIDTH // RWKV_HEAD
RWKV_GN_EPS = 64e-5
FOX_DH = 128
FOX_HEADS = BRANCH_WIDTH // FOX_DH
S5_GROUP = 16
S5_GROUPS = BRANCH_WIDTH // S5_GROUP
S5_STATE = 64
N_EXPERTS = 8
TOP_K = 2
ALPHA = (2.0 * DEPTH) ** 0.25
LN_EPS = 1e-5
NEG_INF = -1e30
LOG2E = math.log2(math.e)

V7X_VMEM_BUDGET = 58 * 1024 * 1024
LANE = 128
RWKV_CHUNK = 64
RWKV_QUAD = 4
RWKV_QW = RWKV_QUAD * RWKV_HEAD
FOX_TK = 128
FOX_SUB = 128
FOX_TQ = 768
ROW_TILES = (1408, 1536, 1280, 768, 640, 512, 256, 128)


def _pick(dim, candidates):
    for c in candidates:
        if dim % c == 0:
            return c
    return dim


def _params(sem, est_bytes):
    limit = int(min(V7X_VMEM_BUDGET, max(est_bytes * 5 // 4 + (2 << 20), 16 << 20)))
    return pltpu.CompilerParams(dimension_semantics=sem, vmem_limit_bytes=limit)


def _nbytes(shape, dtype):
    n = jnp.dtype(dtype).itemsize
    for s in shape:
        if s is not None:
            n *= s
    return n


def _mmf_kernel(*refs, nb, nx, nk, ep, precision):
    a_ref = refs[0]
    b_refs = refs[1:1 + nb]
    x_refs = refs[1 + nb:1 + nb + nx]
    o_ref = refs[1 + nb + nx]
    acc_refs = refs[2 + nb + nx:]
    a = a_ref[...]
    if precision is None and a.dtype != BF16:
        a = a.astype(BF16)
    dots = [jnp.dot(a, b[...], preferred_element_type=F32, precision=precision) for b in b_refs]

    def finish(vals):
        o_ref[...] = ep(*vals, *[x[...] for x in x_refs]).astype(o_ref.dtype)

    if nk == 1:
        finish(dots)
        return
    k = pl.program_id(2)

    @pl.when(k == 0)
    def _():
        for acc, d in zip(acc_refs, dots):
            acc[...] = d

    @pl.when(k > 0)
    def _():
        for acc, d in zip(acc_refs, dots):
            acc[...] += d

    @pl.when(k == nk - 1)
    def _():
        finish([acc[...] for acc in acc_refs])


def _mmf(a_spec, b_specs, x_specs, ep, *, m, n, kd, tm, tn, tk, out_dtype, precision=None, name="mm"):
    nk = kd // tk
    specs = [a_spec] + list(b_specs) + list(x_specs)
    est = 2 * sum(_nbytes(s[1], s[0].dtype) for s in specs) + 2 * tm * tn * jnp.dtype(out_dtype).itemsize
    est += (len(b_specs) + 2) * tm * tn * 4
    scratch = [pltpu.VMEM((tm, tn), F32) for _ in b_specs] if nk > 1 else []
    return pl.pallas_call(
        functools.partial(_mmf_kernel, nb=len(b_specs), nx=len(x_specs), nk=nk, ep=ep, precision=precision),
        name=name,
        grid=(m // tm, n // tn, nk),
        in_specs=[pl.BlockSpec(s[1], s[2]) for s in specs],
        out_specs=pl.BlockSpec((tm, tn), lambda i, j, k: (i, j)),
        out_shape=jax.ShapeDtypeStruct((m, n), out_dtype),
        scratch_shapes=scratch,
        compiler_params=_params(("parallel", "parallel", "arbitrary"), est),
    )(*[s[0] for s in specs])


def _ident(x):
    return x


def _mm(a, b, out_dtype=F32, precision=None, ep=_ident, extras=(), a_blk=0, name="mm"):
    m = a.shape[0]
    kd, n = b.shape
    tm = _pick(m, ROW_TILES)
    tn = _pick(n, (1024, 512, 384, 256, 128))
    tk = kd if kd <= 4096 else _pick(kd, (2048, 1536, 1024, 512))
    if precision is not None:
        tm = _pick(m, (640, 512, 256, 128))
        tn = _pick(n, (512, 384, 256, 128))
        tk = kd if kd <= 2048 else _pick(kd, (2048, 1024, 512))
    nk = kd // tk
    ob = jnp.dtype(out_dtype).itemsize

    def need(tn_):
        return (2 * (tm * tk * a.dtype.itemsize + tk * tn_ * b.dtype.itemsize + tm * tn_ * ob)
                + (3 if nk > 1 else 2) * tm * tn_ * 4)

    while need(tn) > V7X_VMEM_BUDGET - (6 << 20) and tn % 256 == 0 and n % (tn // 2) == 0:
        tn //= 2
    a_spec = (a, (tm, tk), lambda i, j, k: (i, a_blk * nk + k))
    b_spec = (b, (tk, tn), lambda i, j, k: (k, j))
    x_specs = []
    for arr, kind, off in extras:
        if kind == "tile":
            x_specs.append((arr, (tm, tn), lambda i, j, k, off=off: (i, j + off)))
        else:
            x_specs.append((arr, (1, tn), lambda i, j, k, off=off: (0, j + off)))
    return _mmf(a_spec, [b_spec], x_specs, ep, m=m, n=n, kd=kd, tm=tm, tn=tn, tk=tk,
                out_dtype=out_dtype, precision=precision, name=name)


def _silu_mul(g, u):
    return g * jax.nn.sigmoid(g) * u


def _silu_mul_scaled(g, u, s):
    return g * jax.nn.sigmoid(g) * u * s


def _swiglu_up(h, wg, wu, scale=None):
    m, d = h.shape
    e, _, f = wg.shape
    tm = _pick(m, ROW_TILES)
    tn = _pick(f, (512, 256, 128))
    nf = f // tn
    wmap = lambda i, j, k: (j // nf, 0, j % nf)
    x_specs, ep = [], _silu_mul
    if scale is not None:
        x_specs = [(scale, (None, tm, 1), lambda i, j, k: (j // nf, i, 0))]
        ep = _silu_mul_scaled
    return _mmf((h, (tm, d), lambda i, j, k: (i, 0)),
                [(wg, (None, d, tn), wmap), (wu, (None, d, tn), wmap)], x_specs, ep,
                m=m, n=e * f, kd=d, tm=tm, tn=tn, tk=d, out_dtype=BF16, name="swiglu_up")


def _glu_ep(val, gt):
    return val * jax.nn.sigmoid(gt)


def _glu(y, w_glu):
    m, kd = y.shape
    n = w_glu.shape[1] // 2
    tm = _pick(m, ROW_TILES)
    tn = _pick(n, (512, 256, 128))
    nj = n // tn
    return _mmf((y, (tm, kd), lambda i, j, k: (i, 0)),
                [(w_glu, (kd, tn), lambda i, j, k: (0, j)), (w_glu, (kd, tn), lambda i, j, k: (0, nj + j))],
                [], _glu_ep, m=m, n=n, kd=kd, tm=tm, tn=tn, tk=kd, out_dtype=BF16, name="s5_glu")


def _res_ln_kernel(y_ref, r_ref, g_ref, b_ref, o_ref, ob_ref, *, tm, mask_pad):
    z = ALPHA * r_ref[...] + y_ref[...]
    mu = jnp.mean(z, axis=-1, keepdims=True)
    zc = z - mu
    var = jnp.mean(zc * zc, axis=-1, keepdims=True)
    out = zc * lax.rsqrt(var + LN_EPS) * g_ref[...] + b_ref[...]
    if mask_pad is not None:
        pos = pl.program_id(1) * tm + lax.broadcasted_iota(jnp.int32, (tm, 1), 0)
        out = jnp.where((pos >= mask_pad[0]) & (pos < mask_pad[1]), out, 0.0)
    o_ref[...] = out
    ob_ref[...] = out.astype(BF16)


def _res_ln(y, res, g, b, mask_pad):
    bsz, lp, d = res.shape
    tm = _pick(lp, (256, 128, 64, 32, 16, 8))
    est = 2 * (4 * tm * d * 4) + 4 * tm * d * 4
    spec = pl.BlockSpec((None, tm, d), lambda bi, i: (bi, i, 0))
    vspec = pl.BlockSpec((1, d), lambda bi, i: (0, 0))
    return pl.pallas_call(
        functools.partial(_res_ln_kernel, tm=tm, mask_pad=mask_pad),
        name="res_ln",
        grid=(bsz, lp // tm),
        in_specs=[spec, spec, vspec, vspec],
        out_specs=[spec, spec],
        out_shape=[jax.ShapeDtypeStruct((bsz, lp, d), F32), jax.ShapeDtypeStruct((bsz, lp, d), BF16)],
        compiler_params=_params(("parallel", "parallel"), est),
    )(y, res, g.reshape(1, d), b.reshape(1, d))


def _merge_kernel(o0, o1, o2, o3, wb_ref, glr_ref, wgu_ref, bg_ref, out_ref):
    glr = glr_ref[...].astype(BF16)
    acc = None
    for i, o_ref in enumerate((o0, o1, o2, o3)):
        gate = jax.nn.sigmoid(jnp.dot(glr, wgu_ref[i], preferred_element_type=F32) + bg_ref[i])
        term = gate * jnp.dot(o_ref[...], wb_ref[i], preferred_element_type=F32)
        acc = term if acc is None else acc + term
    out_ref[...] = acc.astype(out_ref.dtype)


def _merge(outs, w_branch, small, w_gate_up, b_gate):
    m, w = outs[0].shape
    nb, r, d = w_gate_up.shape
    tm = _pick(m, ROW_TILES)
    tn = _pick(d, (512, 256, 128))
    est = 2 * (nb * tm * w * 2 + nb * w * tn * 2 + tm * r * 4 + nb * r * tn * 2 + tm * tn * 2) + 4 * tm * tn * 4
    ospec = pl.BlockSpec((tm, w), lambda i, j: (i, 0))
    return pl.pallas_call(
        _merge_kernel,
        name="merge",
        grid=(m // tm, d // tn),
        in_specs=[ospec, ospec, ospec, ospec,
                  pl.BlockSpec((nb, w, tn), lambda i, j: (0, 0, j)),
                  pl.BlockSpec((tm, r), lambda i, j: (i, 0)),
                  pl.BlockSpec((nb, r, tn), lambda i, j: (0, 0, j)),
                  pl.BlockSpec((nb, 1, tn), lambda i, j: (0, 0, j))],
        out_specs=pl.BlockSpec((tm, tn), lambda i, j: (i, j)),
        out_shape=jax.ShapeDtypeStruct((m, d), BF16),
        compiler_params=_params(("parallel", "parallel"), est),
    )(*outs, w_branch, small, w_gate_up, b_gate.reshape(nb, 1, d))


def _ret_consts(lp):
    f32 = F32
    half = RET_DK // 2
    inv = 1.0 / (10000.0 ** jnp.linspace(0.0, 1.0, half, dtype=f32))
    ang = jnp.arange(lp, dtype=f32)[:, None] * inv[None, :]
    cos, sin = jnp.cos(ang), jnp.sin(ang)
    cos_t = jnp.tile(jnp.concatenate([cos, cos], axis=-1), (1, RET_HEADS))
    sin_t = jnp.tile(jnp.concatenate([-sin, sin], axis=-1), (1, RET_HEADS))
    log_gamma = jnp.log(1.0 - 2.0 ** (-5.0 - jnp.arange(RET_HEADS, dtype=f32)))
    pos = jnp.arange(BLOCK, dtype=f32)
    rel = pos[:, None] - pos[None, :]
    dmat = jnp.where(rel >= 0, jnp.exp(log_gamma[:, None, None] * jnp.maximum(rel, 0.0)), 0.0)
    k_decay = jnp.exp(log_gamma[:, None] * (BLOCK - 1.0 - pos)[None, :])
    q_decay = jnp.exp(log_gamma[:, None] * (pos + 1.0)[None, :])
    kd_t = jnp.repeat(k_decay.T, RET_DK, axis=1)
    qd_t = jnp.repeat(q_decay.T, RET_DK, axis=1)
    cd = jnp.exp(log_gamma * BLOCK)
    cd_t = jnp.broadcast_to(cd[:, None, None], (RET_HEADS, 1, RET_DV))
    return cos_t, sin_t, dmat, qd_t, kd_t, cd_t


def _ret_kernel(q_ref, k_ref, v_ref, g_ref, cos_ref, sin_ref, dmat_ref, qd_ref, kd_ref, cd_ref,
                o_ref, st_ref):
    @pl.when(pl.program_id(1) == 0)
    def _():
        st_ref[...] = jnp.zeros_like(st_ref)

    cos = cos_ref[...]
    sin = sin_ref[...]
    width = RET_HEADS * RET_DK
    lane = lax.broadcasted_iota(jnp.int32, (BLOCK, width), 1)
    first = (lane % RET_DK) < (RET_DK // 2)

    def rot(x):
        sw = jnp.where(first, pltpu.roll(x, width - RET_DK // 2, 1), pltpu.roll(x, RET_DK // 2, 1))
        return x * cos + sw * sin

    q = rot(q_ref[...])
    k = rot(k_ref[...]) * RET_DK ** -0.5
    qb = q.astype(BF16)
    kb = k.astype(BF16)
    qs = (q * qd_ref[...]).astype(BF16)
    ks = (k * kd_ref[...]).astype(BF16)
    for h in range(RET_HEADS):
        ksl = slice(h * RET_DK, (h + 1) * RET_DK)
        vsl = slice(h * RET_DV, (h + 1) * RET_DV)
        vh = v_ref[:, vsl].astype(BF16)
        s = lax.dot_general(qb[:, ksl], kb[:, ksl], (((1,), (1,)), ((), ())),
                            preferred_element_type=F32) * dmat_ref[h]
        st = st_ref[h]
        y = (jnp.dot(s.astype(BF16), vh, preferred_element_type=F32)
             + jnp.dot(qs[:, ksl], st.astype(BF16), preferred_element_type=F32))
        st_ref[h] = st * cd_ref[h] + lax.dot_general(ks[:, ksl], vh, (((0,), (0,)), ((), ())),
                                                     preferred_element_type=F32)
        mu = jnp.mean(y, axis=-1, keepdims=True)
        yc = y - mu
        var = jnp.mean(yc * yc, axis=-1, keepdims=True)
        gh = g_ref[:, vsl]
        o_ref[:, vsl] = (gh * jax.nn.sigmoid(gh) * yc * lax.rsqrt(var + 1e-6)).astype(o_ref.dtype)


def _retention(proj):
    bsz, lp = proj.shape[:2]
    kw = RET_HEADS * RET_DK
    cos_t, sin_t, dmat, qd_t, kd_t, cd_t = _ret_consts(lp)
    est = 2 * BLOCK * 4 * (2 * kw + 2 * BRANCH_WIDTH + 4 * kw) + 2 * RET_HEADS * BLOCK * BLOCK * 4 + (8 << 20)
    tab = pl.BlockSpec((BLOCK, kw), lambda b, n: (n, 0))
    ctab = pl.BlockSpec((BLOCK, kw), lambda b, n: (0, 0))
    return pl.pallas_call(
        _ret_kernel,
        name="retention",
        grid=(bsz, lp // BLOCK),
        in_specs=[pl.BlockSpec((None, BLOCK, kw), lambda b, n: (b, n, 0)),
                  pl.BlockSpec((None, BLOCK, kw), lambda b, n: (b, n, 1)),
                  pl.BlockSpec((None, BLOCK, BRANCH_WIDTH), lambda b, n: (b, n, 1)),
                  pl.BlockSpec((None, BLOCK, BRANCH_WIDTH), lambda b, n: (b, n, 2)),
                  tab, tab,
                  pl.BlockSpec((RET_HEADS, BLOCK, BLOCK), lambda b, n: (0, 0, 0)),
                  ctab, ctab,
                  pl.BlockSpec((RET_HEADS, 1, RET_DV), lambda b, n: (0, 0, 0))],
        out_specs=pl.BlockSpec((None, BLOCK, BRANCH_WIDTH), lambda b, n: (b, n, 0)),
        out_shape=jax.ShapeDtypeStruct((bsz, lp, BRANCH_WIDTH), BF16),
        scratch_shapes=[pltpu.VMEM((RET_HEADS, RET_DK, RET_DV), F32)],
        compiler_params=_params(("parallel", "arbitrary"), est),
    )(proj, proj, proj, proj, cos_t, sin_t, dmat, qd_t, kd_t, cd_t)


def _split_dot(x, ones):
    hi = x.astype(BF16)
    lo = (x - hi.astype(F32)).astype(BF16)
    return (jnp.dot(hi, ones, preferred_element_type=F32) + jnp.dot(lo, ones, preferred_element_type=F32))


_RV_MU, _RV_W0, _RV_A0, _RV_V0, _RV_KK, _RV_KA, _RV_RK = 0, 6, 7, 8, 9, 10, 11
_RV_ROWS = 16


def _rwkv_pre_kernel(*refs, has_vres):
    (u_ref, up_ref, wr_ref, wk_ref, wv_ref, w1_ref, w2_ref, a1_ref, a2_ref, g1_ref, g2_ref,
     ones_ref, vec_ref) = refs[:13]
    pos = 13
    if has_vres:
        vf_ref, v1_ref, v2_ref = refs[13:16]
        pos = 16
    r_out, lw_out, k_out, v_out, a_out, b_out, gate_out, bonus_out = refs[pos:pos + 8]

    def vec(i):
        return vec_ref[i:i + 1, :]

    u = u_ref[...]
    tm = u.shape[0]
    prev_last = jnp.where(pl.program_id(1) == 0, 0.0, up_ref[7:8, :])
    row = lax.broadcasted_iota(jnp.int32, (tm, 1), 0)
    prev = jnp.where(row == 0, prev_last, pltpu.roll(u, 1, 0))
    xx = prev - u

    def mix(i):
        return (u + xx * vec(_RV_MU + i)).astype(BF16)

    def mm(x, w_ref):
        return jnp.dot(x, w_ref[...], preferred_element_type=F32)

    xr, xw, xk, xv, xa, xg = [mix(i) for i in range(6)]
    r = mm(xr, wr_ref)
    k = mm(xk, wk_ref)
    v = mm(xv, wv_ref)
    z = -(vec(_RV_W0) + mm(jnp.tanh(mm(xw, w1_ref)).astype(BF16), w2_ref))
    softplus = jnp.maximum(z, 0.0) + jnp.log(1.0 + jnp.exp(-jnp.abs(z)))
    lw_out[...] = -jnp.exp(-softplus - 0.5)
    if has_vres:
        mv = jax.nn.sigmoid(vec(_RV_V0) + mm(mm(xv, v1_ref).astype(BF16), v2_ref))
        v = v + (vf_ref[...] - v) * mv
    a = jax.nn.sigmoid(vec(_RV_A0) + mm(mm(xa, a1_ref).astype(BF16), a2_ref))
    gate_out[...] = mm(jax.nn.sigmoid(mm(xg, g1_ref)).astype(BF16), g2_ref)

    ones = ones_ref[...]
    kk = k * vec(_RV_KK)
    kk = kk / jnp.maximum(jnp.sqrt(_split_dot(kk * kk, ones)), 1e-12)
    k = k * (1.0 + (a - 1.0) * vec(_RV_KA))
    r_out[...] = r
    k_out[...] = k
    v_out[...] = v
    a_out[...] = -kk
    b_out[...] = kk * a
    bonus_out[...] = _split_dot(r * k * vec(_RV_RK), ones) * v


def _rwkv_pre(proj, u_blk, v_first, vres, mu, w_rkv, w0, w1, w2, a0, a1, a2, g1, g2, k_k, k_a, r_k):
    bsz, lp = proj.shape[:2]
    w = BRANCH_WIDTH
    tm = _pick(lp, (256, 128, 64, 32, 16, 8))
    has_vres = vres is not None

    def lora_pair(wa, wb):
        rp = -(-wa.shape[1] // LANE) * LANE
        return (jnp.pad(wa, ((0, 0), (0, rp - wa.shape[1]))).astype(BF16),
                jnp.pad(wb, ((0, rp - wb.shape[0]), (0, 0))).astype(BF16))

    w1p, w2p = lora_pair(w1, w2)
    a1p, a2p = lora_pair(a1, a2)
    g1p, g2p = lora_pair(g1, g2)
    hid = jnp.arange(w) // RWKV_HEAD
    ones = (hid[:, None] == hid[None, :]).astype(BF16)
    zero = jnp.zeros((w,), F32)
    rows = [mu[i] for i in range(6)] + [w0, a0, vres[0] if has_vres else zero, k_k, k_a, r_k.reshape(w)]
    vec = jnp.stack(rows + [zero] * (_RV_ROWS - len(rows)))

    tile = pl.BlockSpec((None, tm, w), lambda b, i: (b, i, 0))
    const = lambda arr: pl.BlockSpec(arr.shape, lambda b, i: (0,) * arr.ndim)
    weights = [w_rkv[0].astype(BF16), w_rkv[1].astype(BF16), w_rkv[2].astype(BF16),
               w1p, w2p, a1p, a2p, g1p, g2p, ones, vec]
    args = [proj, proj] + weights
    in_specs = [pl.BlockSpec((None, tm, w), lambda b, i: (b, i, u_blk)),
                pl.BlockSpec((None, 8, w), lambda b, i: (b, jnp.maximum(i * (tm // 8) - 1, 0), u_blk))]
    in_specs += [const(x) for x in weights]
    if has_vres:
        v1p, v2p = lora_pair(vres[1], vres[2])
        args += [v_first, v1p, v2p]
        in_specs += [tile, const(v1p), const(v2p)]
    est = 2 * sum(x.size * x.dtype.itemsize for x in weights) + 2 * 11 * tm * w * 4 + 40 * tm * w * 4
    outs = pl.pallas_call(
        functools.partial(_rwkv_pre_kernel, has_vres=has_vres),
        name="rwkv_pre",
        grid=(bsz, lp // tm),
        in_specs=in_specs,
        out_specs=[tile] * 8,
        out_shape=[jax.ShapeDtypeStruct((bsz, lp, w), F32)] * 8,
        compiler_params=_params(("parallel", "parallel"), est),
    )(*args)
    return outs


def _wkv_kernel(r_ref, lw_ref, k_ref, v_ref, a_ref, b_ref, gate_ref, bonus_ref, lnw_ref, lnb_ref,
                o_ref, ht_ref):
    c = RWKV_CHUNK
    wd = RWKV_QW

    @pl.when(pl.program_id(1) == 0)
    def _():
        ht_ref[...] = jnp.zeros_like(ht_ref)

    ti = lax.broadcasted_iota(jnp.int32, (c, c), 0)
    tj = lax.broadcasted_iota(jnp.int32, (c, c), 1)
    cum_all = jnp.dot((ti >= tj).astype(F32), lw_ref[...], preferred_element_type=F32,
                      precision=lax.Precision.HIGHEST)

    ri = lax.broadcasted_iota(jnp.int32, (wd, wd), 0)
    ci = lax.broadcasted_iota(jnp.int32, (wd, wd), 1)
    same = (ri // c) == (ci // RWKV_HEAD)
    strict = same & ((ci % c) < (ri % c))
    incl = same & ((ci % c) <= (ri % c))
    ones_blk = jnp.where(same, 1.0, 0.0).astype(BF16)

    def expand(x):
        xb = x.astype(BF16)
        return jnp.where(same, jnp.concatenate([xb] * RWKV_QUAD, axis=0), jnp.zeros((), BF16))

    def nt(x, y):
        return lax.dot_general(x, y, (((1,), (1,)), ((), ())), preferred_element_type=F32)

    def tn(x, y):
        return lax.dot_general(x, y, (((0,), (0,)), ((), ())), preferred_element_type=F32)

    def mm(x, y):
        return jnp.dot(x, y, preferred_element_type=F32)

    groups = range(r_ref.shape[1] // wd)
    sls = [slice(q * wd, (q + 1) * wd) for q in groups]
    ar, bt, kt, vb, bw, kw, w_last = [], [], [], [], [], [], []
    for sl in sls:
        lw = lw_ref[:, sl]
        cum = cum_all[:, sl]
        cum_last = cum[c - 1:c, :]
        e_neg = jnp.exp(-cum)
        e_rem = jnp.exp(cum_last - cum)
        k = k_ref[:, sl]
        b = b_ref[:, sl]
        ar.append(jnp.concatenate([expand(a_ref[:, sl] * jnp.exp(cum - lw)),
                                   expand(r_ref[:, sl] * jnp.exp(cum))], axis=0))
        bt.append(expand(b * e_neg))
        kt.append(expand(k * e_neg))
        vb.append(expand(v_ref[:, sl]))
        bw.append(expand(b * e_rem))
        kw.append(expand(k * e_rem))
        w_last.append(jnp.exp(cum_last))

    g_b = [nt(ar[q], bt[q]) for q in groups]
    g_k = [nt(ar[q], kt[q]) for q in groups]
    hts = [ht_ref[q] for q in groups]
    g_h = [nt(ar[q], hts[q].astype(BF16)) for q in groups]
    p = [jnp.where(strict, g_b[q][:wd], 0.0) for q in groups]
    a_ak = [jnp.where(strict, g_k[q][:wd], 0.0).astype(BF16) for q in groups]
    a_rb = [jnp.where(incl, g_b[q][wd:], 0.0).astype(BF16) for q in groups]
    a_rk = [jnp.where(incl, g_k[q][wd:], 0.0).astype(BF16) for q in groups]
    u = [g_h[q][:wd] + mm(a_ak[q], vb[q]) for q in groups]
    steps = int(math.log2(c))
    for i in range(steps):
        pb = [x.astype(BF16) for x in p]
        u = [u[q] + mm(pb[q], u[q].astype(BF16)) for q in groups]
        if i < steps - 1:
            p = [mm(pb[q], pb[q]) for q in groups]
    ub = [x.astype(BF16) for x in u]
    y_bd = [g_h[q][wd:] + mm(a_rb[q], ub[q]) + mm(a_rk[q], vb[q]) for q in groups]
    for q in groups:
        ht_ref[q] = hts[q] * w_last[q] + tn(ub[q], bw[q]) + tn(vb[q], kw[q])
    for q in groups:
        sl = sls[q]
        y = y_bd[q][0:c]
        for h in range(1, RWKV_QUAD):
            y = y + y_bd[q][h * c:(h + 1) * c]
        mean = _split_dot(y, ones_blk) * (1.0 / RWKV_HEAD)
        yc = y - mean
        var = _split_dot(yc * yc, ones_blk) * (1.0 / RWKV_HEAD)
        out = yc * lax.rsqrt(var + RWKV_GN_EPS) * lnw_ref[:, sl] + lnb_ref[:, sl] + bonus_ref[:, sl]
        o_ref[:, sl] = (out * gate_ref[:, sl]).astype(o_ref.dtype)


def _wkv7(r, lw, k, v, a, b, gate, bonus, ln_w, ln_b):
    bsz, lp, w = r.shape
    c = RWKV_CHUNK
    spec = pl.BlockSpec((None, c, w), lambda bi, n: (bi, n, 0))
    vspec = pl.BlockSpec((1, w), lambda bi, n: (0, 0))
    est = 2 * 9 * c * w * 4 + 4 * 40 * RWKV_QW * RWKV_QW * 4
    return pl.pallas_call(
        _wkv_kernel,
        name="wkv7",
        grid=(bsz, lp // c),
        in_specs=[spec] * 8 + [vspec, vspec],
        out_specs=spec,
        out_shape=jax.ShapeDtypeStruct((bsz, lp, w), BF16),
        scratch_shapes=[pltpu.VMEM((w // RWKV_QW, RWKV_QW, RWKV_QW), F32)],
        compiler_params=_params(("parallel", "arbitrary"), est),
    )(r, lw, k, v, a, b, gate, bonus, ln_w.reshape(1, w), ln_b.reshape(1, w))


def _fox_kernel(qt_ref, k_ref, vt_ref, ck_ref, o_ref, acc_ref, st_a, st_b, pt_a, pt_b):
    tk, sw = FOX_TK, FOX_SUB
    tq = qt_ref.shape[1]
    nsub = tq // sw
    qi = pl.program_id(2)
    tri = (lax.broadcasted_iota(jnp.int32, (tk, sw), 0) <= lax.broadcasted_iota(jnp.int32, (tk, sw), 1))
    ones = tuple(jnp.ones((1, sw), F32) for _ in range(nsub))

    def issue_scores(kb, st_out, first_sub=0):
        ks = pl.multiple_of(kb * tk, tk)
        st_out[:, first_sub * sw:] = jnp.dot(k_ref[pl.ds(ks, tk), :], qt_ref[:, first_sub * sw:],
                                             preferred_element_type=F32)

    def issue_pv(kb, pt_in, alphas, first_sub=0):
        ks = pl.multiple_of(kb * tk, tk)
        pv = jnp.dot(vt_ref[:, pl.ds(ks, tk)], pt_in[:, first_sub * sw:], preferred_element_type=F32)
        for r in range(first_sub, nsub):
            acc_ref[r] = alphas[r] * acc_ref[r] + pv[:, (r - first_sub) * sw:(r - first_sub + 1) * sw]

    def softmax(kb, st_in, pt_out, ml, first_sub=0, tri_sub=None):
        ck = ck_ref[pl.ds(pl.multiple_of(kb * tk, tk), tk), :]
        ml, alphas = list(ml), list(ones)
        for r in range(first_sub, nsub):
            s = st_in[:, r * sw:(r + 1) * sw] - ck
            if r == tri_sub:
                s = jnp.where(tri, s, NEG_INF)
            m, l = ml[r]
            m_new = jnp.maximum(m, jnp.max(s, axis=0, keepdims=True))
            alphas[r] = jnp.exp2(m - m_new)
            p = jnp.exp2(s - m_new)
            ml[r] = (m_new, alphas[r] * l + jnp.sum(p, axis=0, keepdims=True))
            pt_out[:, r * sw:(r + 1) * sw] = p.astype(BF16)
        return tuple(ml), tuple(alphas)

    bufs = ((st_a, pt_a), (st_b, pt_b))

    def step(kb, parity, alphas_prev, ml, first_sub=0, tri_sub=None, prev_first=0, next_first=0):
        (st_cur, pt_cur), (st_nxt, pt_prv) = bufs[parity], bufs[1 - parity]
        if next_first is not None:
            issue_scores(kb + 1, st_nxt, next_first)
        issue_pv(jnp.maximum(kb - 1, 0), pt_prv, alphas_prev, prev_first)
        return softmax(kb, st_cur, pt_cur, ml, first_sub, tri_sub)

    def body(i, carry):
        alphas, ml = carry
        ml, alphas = step(2 * i, 0, alphas, ml)
        ml, alphas = step(2 * i + 1, 1, alphas, ml)
        return alphas, ml

    acc_ref[...] = jnp.zeros_like(acc_ref)
    pt_b[...] = jnp.zeros_like(pt_b)
    issue_scores(0, st_a)
    n_main = qi * (tq // tk)
    init_ml = tuple((jnp.full((1, sw), NEG_INF, F32), jnp.zeros((1, sw), F32)) for _ in range(nsub))
    alphas, ml = lax.fori_loop(0, n_main // 2, body, (ones, init_ml))
    ndiag = tq // tk
    for d in range(ndiag):
        ml, alphas = step(n_main + d, d % 2, alphas, ml, first_sub=d, tri_sub=d, prev_first=max(d - 1, 0),
                          next_first=d + 1 if d + 1 < ndiag else None)
    issue_pv(n_main + ndiag - 1, bufs[(ndiag - 1) % 2][1], alphas, ndiag - 1)
    for r in range(nsub):
        o_ref[:, r * sw:(r + 1) * sw] = (acc_ref[r] / ml[r][1]).astype(o_ref.dtype)


def _fox(qkv, c, lp_real):
    bsz, lp = qkv.shape[:2]
    h, dh, w = FOX_HEADS, FOX_DH, BRANCH_WIDTH
    tq = FOX_TQ
    pos = jnp.arange(lp)
    ck = jnp.where((pos >= N_PAD) & (pos < lp_real), c, -NEG_INF)
    ck_rep = jnp.broadcast_to((ck * LOG2E)[..., None], (bsz, h, lp, FOX_SUB))
    qt = qkv[..., :w].reshape(bsz, lp, h, dh).transpose(0, 2, 3, 1)
    kk = qkv[..., w:2 * w].reshape(bsz, lp, h, dh).transpose(0, 2, 1, 3)
    vt = qkv[..., 2 * w:].reshape(bsz, lp, h, dh).transpose(0, 2, 3, 1)
    est = 2 * (dh * tq * 2 + lp * dh * 2 + dh * lp * 2 + lp * FOX_SUB * 4 + dh * tq * 2) + (16 << 20)
    out_t = pl.pallas_call(
        _fox_kernel,
        name="fox",
        grid=(bsz, h, lp // tq),
        in_specs=[pl.BlockSpec((None, None, dh, tq), lambda b, hh, i: (b, hh, 0, i)),
                  pl.BlockSpec((None, None, lp, dh), lambda b, hh, i: (b, hh, 0, 0)),
                  pl.BlockSpec((None, None, dh, lp), lambda b, hh, i: (b, hh, 0, 0)),
                  pl.BlockSpec((None, None, lp, FOX_SUB), lambda b, hh, i: (b, hh, 0, 0))],
        out_specs=pl.BlockSpec((None, None, dh, tq), lambda b, hh, i: (b, hh, 0, i)),
        out_shape=jax.ShapeDtypeStruct((bsz, h, dh, lp), BF16),
        scratch_shapes=[pltpu.VMEM((tq // FOX_SUB, dh, FOX_SUB), F32),
                        pltpu.VMEM((FOX_TK, tq), F32), pltpu.VMEM((FOX_TK, tq), F32),
                        pltpu.VMEM((FOX_TK, tq), BF16), pltpu.VMEM((FOX_TK, tq), BF16)],
        compiler_params=_params(("parallel", "parallel", "parallel"), est),
    )(qt, kk, vt, ck_rep)
    return out_t.transpose(0, 3, 1, 2).reshape(bsz, lp, w)


S5_ROWS = 16
S5_COLS = S5_GROUPS * S5_STATE // 8


def _s5_scan_kernel(x_ref, a_ref, o_ref, st_ref, *, tc):
    @pl.when(pl.program_id(1) == 0)
    def _():
        st_ref[...] = jnp.zeros_like(st_ref)

    ar = a_ref[0:8]
    ai = a_ref[8:16]

    def body(t, carry):
        hr, hi = carry
        x = x_ref[t]
        nr = ar * hr - ai * hi + x[0:8]
        ni = ar * hi + ai * hr + x[8:16]
        o_ref[t] = jnp.concatenate([nr, ni], axis=0).astype(o_ref.dtype)
        return nr, ni

    hr, hi = lax.fori_loop(0, tc, body, (st_ref[0:8], st_ref[8:16]), unroll=8)
    st_ref[0:8] = hr
    st_ref[8:16] = hi


def _s5_scan(bu, a_bar):
    bsz, lp = bu.shape[:2]
    tc = _pick(lp, (128, 64, 32, 16, 8))
    est = 2 * tc * S5_ROWS * S5_COLS * (4 + 2) + (4 << 20)
    return pl.pallas_call(
        functools.partial(_s5_scan_kernel, tc=tc),
        name="s5_scan",
        grid=(bsz, lp // tc),
        in_specs=[pl.BlockSpec((None, tc, S5_ROWS, S5_COLS), lambda b, n: (b, n, 0, 0)),
                  pl.BlockSpec((S5_ROWS, S5_COLS), lambda b, n: (0, 0))],
        out_specs=pl.BlockSpec((None, tc, S5_ROWS, S5_COLS), lambda b, n: (b, n, 0, 0)),
        out_shape=jax.ShapeDtypeStruct(bu.shape, BF16),
        scratch_shapes=[pltpu.VMEM((S5_ROWS, S5_COLS), F32)],
        compiler_params=_params(("parallel", "arbitrary"), est),
    )(bu, a_bar)


def _s5_weights(a_re, a_im, b_re, b_im, c_re, c_im, log_step):
    g, p, cg = S5_GROUPS, S5_STATE, S5_GROUP
    dt = jnp.exp(log_step)[:, None]
    mag = jnp.exp(a_re * dt)
    bar_re = mag * jnp.cos(a_im * dt)
    bar_im = mag * jnp.sin(a_im * dt)
    den = a_re * a_re + a_im * a_im
    num_re = bar_re - 1.0
    coef_re = (num_re * a_re + bar_im * a_im) / den
    coef_im = (bar_im * a_re - num_re * a_im) / den
    bb_re = coef_re[..., None] * b_re - coef_im[..., None] * b_im
    bb_im = coef_re[..., None] * b_im + coef_im[..., None] * b_re
    eye = jnp.eye(g, dtype=F32)
    bblk = jnp.concatenate([jnp.einsum('gpc,gh->gchp', m, eye).reshape(g * cg, g * p) for m in (bb_re, bb_im)],
                           axis=1)
    cblk = jnp.concatenate([jnp.einsum('gcp,gh->gphc', m, eye).reshape(g * p, g * cg) for m in (c_re, -c_im)],
                           axis=0)
    a_bar = jnp.concatenate([bar_re.reshape(8, S5_COLS), bar_im.reshape(8, S5_COLS)], axis=0)
    return bblk.astype(BF16), cblk.astype(BF16), a_bar


def _s5_out_ep(y, u, d):
    return jax.nn.gelu(y + d * u)


def _s5_branch(proj2, u_blk, bsz, lp, a_re, a_im, b_re, b_im, c_re, c_im, d_skip, log_step, w_glu):
    w = BRANCH_WIDTH
    t = bsz * lp
    bblk, cblk, a_bar = _s5_weights(a_re, a_im, b_re, b_im, c_re, c_im, log_step)
    bu = _mm(proj2, bblk, a_blk=u_blk, name="s5_bu")
    hs = _s5_scan(bu.reshape(bsz, lp, S5_ROWS, S5_COLS), a_bar)
    y = _mm(hs.reshape(t, S5_ROWS * S5_COLS), cblk, out_dtype=BF16, ep=_s5_out_ep,
            extras=((proj2, "tile", u_blk), (d_skip.reshape(1, w), "row", 0)), name="s5_out")
    return _glu(y, w_glu.astype(BF16))


def _pad_cols(w, n):
    return jnp.pad(w, ((0, 0), (0, n - w.shape[1])))


def _scale_cols(x, s):
    return x * s


MOE_TM = 512


def _moe_up_kernel(te_ref, x_ref, wg_ref, wu_ref, s_ref, o_ref, *, ni):
    i = pl.program_id(1)

    @pl.when(i < te_ref[ni])
    def _():
        x = x_ref[...]
        g = jnp.dot(x, wg_ref[...], preferred_element_type=F32)
        u = jnp.dot(x, wu_ref[...], preferred_element_type=F32)
        o_ref[...] = (g * jax.nn.sigmoid(g) * u * s_ref[...]).astype(o_ref.dtype)

    @pl.when(i >= te_ref[ni])
    def _():
        o_ref[...] = jnp.zeros_like(o_ref)


def _moe_down_kernel(te_ref, a_ref, w_ref, o_ref, *, ni):
    i = pl.program_id(1)

    @pl.when(i < te_ref[ni])
    def _():
        o_ref[...] = jnp.dot(a_ref[...], w_ref[...], preferred_element_type=F32).astype(o_ref.dtype)

    @pl.when(i >= te_ref[ni])
    def _():
        o_ref[...] = jnp.zeros_like(o_ref)


def _moe_experts(xs, row_scale, tile_info, w_gate, w_up, w_down):
    r, d = xs.shape
    e, _, f = w_gate.shape
    tm = MOE_TM
    ni = r // tm
    tn = _pick(f, (512, 256, 128))
    wmap = lambda j, i, te: (te[i], 0, j)
    act = pl.pallas_call(
        functools.partial(_moe_up_kernel, ni=ni),
        name="moe_up",
        grid_spec=pltpu.PrefetchScalarGridSpec(
            num_scalar_prefetch=1, grid=(f // tn, ni),
            in_specs=[pl.BlockSpec((tm, d), lambda j, i, te: (i, 0)),
                      pl.BlockSpec((None, d, tn), wmap), pl.BlockSpec((None, d, tn), wmap),
                      pl.BlockSpec((tm, 1), lambda j, i, te: (i, 0))],
            out_specs=pl.BlockSpec((tm, tn), lambda j, i, te: (i, j))),
        out_shape=jax.ShapeDtypeStruct((r, f), BF16),
        compiler_params=_params(("arbitrary", "arbitrary"),
                                2 * (tm * d * 2 + 2 * d * tn * 2 + tm * tn * 2) + 4 * tm * tn * 4),
    )(tile_info, xs, w_gate, w_up, row_scale)
    tn2 = _pick(d, (1024, 512, 256, 128))
    return pl.pallas_call(
        functools.partial(_moe_down_kernel, ni=ni),
        name="moe_down",
        grid_spec=pltpu.PrefetchScalarGridSpec(
            num_scalar_prefetch=1, grid=(d // tn2, ni),
            in_specs=[pl.BlockSpec((tm, f), lambda j, i, te: (i, 0)),
                      pl.BlockSpec((None, f, tn2), wmap)],
            out_specs=pl.BlockSpec((tm, tn2), lambda j, i, te: (i, j))),
        out_shape=jax.ShapeDtypeStruct((r, d), F32),
        compiler_params=_params(("arbitrary", "arbitrary"),
                                2 * (tm * f * 2 + f * tn2 * 2 + tm * tn2 * 4) + 2 * tm * tn2 * 4),
    )(tile_info, act, w_down)


def _moe_ffn(h32, hb, w_router, b_router, w_gate, w_up, w_down):
    t, d = hb.shape
    e = w_gate.shape[0]
    tm = MOE_TM
    logits = _mm(h32, _pad_cols(w_router, LANE), precision=lax.Precision.HIGHEST,
                 name="router")[:, :N_EXPERTS] + b_router
    top_vals, top_idx = lax.top_k(logits, TOP_K)
    weights = jax.nn.softmax(top_vals, axis=-1)

    n_assign = t * TOP_K
    n_rows = n_assign + e * tm
    flat_e = top_idx.reshape(n_assign).astype(jnp.int32)
    order = jnp.argsort(flat_e, stable=True).astype(jnp.int32)
    counts = jnp.sum(jax.nn.one_hot(flat_e, e, dtype=jnp.int32), axis=0)
    padded = (counts + tm - 1) // tm * tm
    group_end = jnp.cumsum(padded)
    sorted_e = flat_e[order]
    rank = jnp.arange(n_assign, dtype=jnp.int32) - (jnp.cumsum(counts) - counts)[sorted_e]
    dest_sorted = (group_end - padded)[sorted_e] + rank
    src_tok = jnp.zeros((n_rows,), jnp.int32).at[dest_sorted].set(order // TOP_K)
    row_scale = jnp.zeros((n_rows,), F32).at[dest_sorted].set(weights.reshape(n_assign)[order])
    dest = jnp.zeros((n_assign,), jnp.int32).at[order].set(dest_sorted)
    tile_start = jnp.arange(n_rows // tm, dtype=jnp.int32) * tm
    tile_e = jnp.minimum(jnp.searchsorted(group_end, tile_start, side="right"), e - 1).astype(jnp.int32)
    tile_info = jnp.concatenate([tile_e, (group_end[-1:] // tm).astype(jnp.int32)])

    xs = jnp.take(hb, src_tok, axis=0)
    y = _moe_experts(xs, row_scale[:, None], tile_info, w_gate.astype(BF16), w_up.astype(BF16),
                     w_down.astype(BF16))
    return jnp.take(y, dest, axis=0).reshape(t, TOP_K, d).sum(axis=1)


def kernel(x, meta_tokens, w_in, b_forget, w_gate_down, w_gate_up, b_gate, w_branch, w_out, ln_mix_g, ln_mix_b, ln_ffn_g, ln_ffn_b, rwkv_mu, rwkv_w_rkv, rwkv_w0, rwkv_w1, rwkv_w2, rwkv_a0, rwkv_a1, rwkv_a2, rwkv_v0, rwkv_v1, rwkv_v2, rwkv_g1, rwkv_g2, rwkv_k_k, rwkv_k_a, rwkv_r_k, rwkv_ln_w, rwkv_ln_b, s5_a_re, s5_a_im, s5_b_re, s5_b_im, s5_c_re, s5_c_im, s5_d, s5_log_step, s5_w_glu, ffn_w_gate, ffn_w_up, ffn_w_down, moe_router, moe_router_b, moe_w_gate, moe_w_up, moe_w_down):
    bsz, seq, d = x.shape
    depth = w_in.shape[0]
    w = BRANCH_WIDTH
    lp_real = N_PAD + N_META + seq
    lp = -(-lp_real // FOX_TQ) * FOX_TQ
    t = bsz * lp
    meta = jnp.broadcast_to(meta_tokens.astype(x.dtype)[None], (bsz, N_META, d))
    h = jnp.concatenate([jnp.zeros((bsz, N_PAD, d), x.dtype), meta, x,
                         jnp.zeros((bsz, lp - lp_real, d), x.dtype)], axis=1)
    hb = h.astype(BF16)
    pos = jnp.arange(lp)
    real = ((pos >= N_PAD) & (pos < lp_real))[None, :, None]
    f_lo = 7 * w
    q_scale = jnp.concatenate([jnp.full((1, w), FOX_DH ** -0.5 * LOG2E, F32), jnp.ones((1, 2 * w), F32)], axis=1)
    v_first = None
    for layer in range(depth):
        wl = w_in[layer]
        w_f32 = jnp.concatenate([wl[:, :4 * w], wl[:, f_lo + FOX_HEADS:]], axis=1).astype(BF16)
        w_fox = wl[:, 4 * w:f_lo].astype(BF16)
        w_small = _pad_cols(jnp.concatenate([w_gate_down[layer], wl[:, f_lo:f_lo + FOX_HEADS]], axis=1),
                            GATE_RANK + LANE).astype(BF16)
        hb2 = hb.reshape(t, d)
        proj2 = _mm(hb2, w_f32, name="proj")
        proj = proj2.reshape(bsz, lp, 5 * w)
        qkv = _mm(hb2, w_fox, out_dtype=BF16, ep=_scale_cols, extras=((q_scale, "row", 0),),
                  name="proj_fox").reshape(bsz, lp, 3 * w)
        small = _mm(hb2, w_small, name="proj_small")
        f_logit = small[:, GATE_RANK:GATE_RANK + FOX_HEADS].reshape(bsz, lp, FOX_HEADS)

        o_ret = _retention(proj)

        vres = None if layer == 0 else (rwkv_v0[layer - 1], rwkv_v1[layer - 1], rwkv_v2[layer - 1])
        r, lw, k, v, a, b, gate, bonus = _rwkv_pre(
            proj, 3, v_first, vres, rwkv_mu[layer], rwkv_w_rkv[layer], rwkv_w0[layer], rwkv_w1[layer],
            rwkv_w2[layer], rwkv_a0[layer], rwkv_a1[layer], rwkv_a2[layer], rwkv_g1[layer], rwkv_g2[layer],
            rwkv_k_k[layer], rwkv_k_a[layer], rwkv_r_k[layer])
        if layer == 0:
            v_first = v
        o_rwkv = _wkv7(r, lw, k, v, a, b, gate, bonus, rwkv_ln_w[layer], rwkv_ln_b[layer])

        log_f = jnp.where(real, jax.nn.log_sigmoid(f_logit + b_forget[layer]), 0.0)
        c = jnp.cumsum(log_f, axis=1).transpose(0, 2, 1)
        o_fox = _fox(qkv, c, lp_real)

        o_s5 = _s5_branch(proj2, 4, bsz, lp, s5_a_re[layer], s5_a_im[layer], s5_b_re[layer], s5_b_im[layer],
                          s5_c_re[layer], s5_c_im[layer], s5_d[layer], s5_log_step[layer], s5_w_glu[layer])

        merged = _merge([o_ret.reshape(t, w), o_rwkv.reshape(t, w), o_fox.reshape(t, w), o_s5],
                        w_branch[layer].astype(BF16), small, w_gate_up[layer].astype(BF16), b_gate[layer])
        mix = _mm(merged, w_out[layer].astype(BF16), name="w_out").reshape(bsz, lp, d)
        h, hb = _res_ln(mix, h, ln_mix_g[layer], ln_mix_b[layer], mask_pad=None)

        j = layer // 2
        hb2 = hb.reshape(t, d)
        if layer % 2 == 0:
            act = _swiglu_up(hb2, ffn_w_gate[j][None].astype(BF16), ffn_w_up[j][None].astype(BF16))
            ffn = _mm(act, ffn_w_down[j].astype(BF16), name="ffn_down")
        else:
            ffn = _moe_ffn(h.reshape(t, d), hb2, moe_router[j], moe_router_b[j],
                           moe_w_gate[j], moe_w_up[j], moe_w_down[j])
        h, hb = _res_ln(ffn.reshape(bsz, lp, d), h, ln_ffn_g[layer], ln_ffn_b[layer],
                        mask_pad=(N_PAD, lp_real))
    return h[:, N_PAD + N_META:lp_real]
```

```python
import functools
import math

import jax
import jax.numpy as jnp
from jax import lax
from jax.experimental import pallas as pl
from jax.experimental.pallas import tpu as pltpu

F32 = jnp.float32
BF16 = jnp.bfloat16

D_MODEL = 4096
DEPTH = 4
N_META = 16
BLOCK = 128
N_PAD = (-N_META) % BLOCK
BRANCH_WIDTH = D_MODEL // 4
GATE_RANK = 256
RET_DK = 64
RET_DV = 128
RET_HEADS = BRANCH_WIDTH // RET_DV
RWKV_HEAD = 64
RWKV_HEADS = BRANCH_WIDTH // RWKV_HEAD
RWKV_GN_EPS = 64e-5
FOX_DH = 128
FOX_HEADS = BRANCH_WIDTH // FOX_DH
S5_GROUP = 16
S5_GROUPS = BRANCH_WIDTH // S5_GROUP
S5_STATE = 64
N_EXPERTS = 8
TOP_K = 2
ALPHA = (2.0 * DEPTH) ** 0.25
LN_EPS = 1e-5
NEG_INF = -1e30
LOG2E = math.log2(math.e)

V7X_VMEM_BUDGET = 58 * 1024 * 1024
LANE = 128
RWKV_CHUNK = 64
RWKV_QUAD = 4
RWKV_QW = RWKV_QUAD * RWKV_HEAD
FOX_TK = 128
FOX_SUB = 128
FOX_TQ = 768
ROW_TILES = (1408, 1536, 1280, 768, 640, 512, 256, 128)


def _pick(dim, candidates):
    for c in candidates:
        if dim % c == 0:
            return c
    return dim


def _params(sem, est_bytes):
    limit = int(min(V7X_VMEM_BUDGET, max(est_bytes * 5 // 4 + (2 << 20), 16 << 20)))
    return pltpu.CompilerParams(dimension_semantics=sem, vmem_limit_bytes=limit)


def _nbytes(shape, dtype):
    n = jnp.dtype(dtype).itemsize
    for s in shape:
        if s is not None:
            n *= s
    return n


def _mmf_kernel(*refs, nb, nx, nk, ep, precision):
    a_ref = refs[0]
    b_refs = refs[1:1 + nb]
    x_refs = refs[1 + nb:1 + nb + nx]
    o_ref = refs[1 + nb + nx]
    acc_refs = refs[2 + nb + nx:]
    a = a_ref[...]
    if precision is None and a.dtype != BF16:
        a = a.astype(BF16)
    dots = [jnp.dot(a, b[...], preferred_element_type=F32, precision=precision) for b in b_refs]

    def finish(vals):
        o_ref[...] = ep(*vals, *[x[...] for x in x_refs]).astype(o_ref.dtype)

    if nk == 1:
        finish(dots)
        return
    k = pl.program_id(2)

    @pl.when(k == 0)
    def _():
        for acc, d in zip(acc_refs, dots):
            acc[...] = d

    @pl.when(k > 0)
    def _():
        for acc, d in zip(acc_refs, dots):
            acc[...] += d

    @pl.when(k == nk - 1)
    def _():
        finish([acc[...] for acc in acc_refs])


def _mmf(a_spec, b_specs, x_specs, ep, *, m, n, kd, tm, tn, tk, out_dtype, precision=None, name="mm"):
    nk = kd // tk
    specs = [a_spec] + list(b_specs) + list(x_specs)
    est = 2 * sum(_nbytes(s[1], s[0].dtype) for s in specs) + 2 * tm * tn * jnp.dtype(out_dtype).itemsize
    est += (len(b_specs) + 2) * tm * tn * 4
    scratch = [pltpu.VMEM((tm, tn), F32) for _ in b_specs] if nk > 1 else []
    return pl.pallas_call(
        functools.partial(_mmf_kernel, nb=len(b_specs), nx=len(x_specs), nk=nk, ep=ep, precision=precision),
        name=name,
        grid=(m // tm, n // tn, nk),
        in_specs=[pl.BlockSpec(s[1], s[2]) for s in specs],
        out_specs=pl.BlockSpec((tm, tn), lambda i, j, k: (i, j)),
        out_shape=jax.ShapeDtypeStruct((m, n), out_dtype),
        scratch_shapes=scratch,
        compiler_params=_params(("parallel", "parallel", "arbitrary"), est),
    )(*[s[0] for s in specs])


def _ident(x):
    return x


def _mm(a, b, out_dtype=F32, precision=None, ep=_ident, extras=(), a_blk=0, name="mm"):
    m = a.shape[0]
    kd, n = b.shape
    tm = _pick(m, ROW_TILES)
    tn = _pick(n, (1024, 512, 384, 256, 128))
    tk = kd if kd <= 4096 else _pick(kd, (2048, 1536, 1024, 512))
    if precision is not None:
        tm = _pick(m, (640, 512, 256, 128))
        tn = _pick(n, (512, 384, 256, 128))
        tk = kd if kd <= 2048 else _pick(kd, (2048, 1024, 512))
    nk = kd // tk
    ob = jnp.dtype(out_dtype).itemsize

    def need(tn_):
        return (2 * (tm * tk * a.dtype.itemsize + tk * tn_ * b.dtype.itemsize + tm * tn_ * ob)
                + (3 if nk > 1 else 2) * tm * tn_ * 4)

    while need(tn) > V7X_VMEM_BUDGET - (6 << 20) and tn % 256 == 0 and n % (tn // 2) == 0:
        tn //= 2
    a_spec = (a, (tm, tk), lambda i, j, k: (i, a_blk * nk + k))
    b_spec = (b, (tk, tn), lambda i, j, k: (k, j))
    x_specs = []
    for arr, kind, off in extras:
        if kind == "tile":
            x_specs.append((arr, (tm, tn), lambda i, j, k, off=off: (i, j + off)))
        else:
            x_specs.append((arr, (1, tn), lambda i, j, k, off=off: (0, j + off)))
    return _mmf(a_spec, [b_spec], x_specs, ep, m=m, n=n, kd=kd, tm=tm, tn=tn, tk=tk,
                out_dtype=out_dtype, precision=precision, name=name)


def _silu_mul(g, u):
    return g * jax.nn.sigmoid(g) * u


def _swiglu_up(h, wg, wu):
    m, d = h.shape
    f = wg.shape[1]
    tm = _pick(m, ROW_TILES)
    tn = _pick(f, (512, 256, 128))
    wmap = lambda i, j, k: (0, j)
    return _mmf((h, (tm, d), lambda i, j, k: (i, 0)),
                [(wg, (d, tn), wmap), (wu, (d, tn), wmap)], [], _silu_mul,
                m=m, n=f, kd=d, tm=tm, tn=tn, tk=d, out_dtype=BF16, name="swiglu_up")


def _glu_ep(val, gt):
    return val * jax.nn.sigmoid(gt)


def _glu(y, w_glu):
    m, kd = y.shape
    n = w_glu.shape[1] // 2
    tm = _pick(m, ROW_TILES)
    tn = _pick(n, (512, 256, 128))
    nj = n // tn
    return _mmf((y, (tm, kd), lambda i, j, k: (i, 0)),
                [(w_glu, (kd, tn), lambda i, j, k: (0, j)), (w_glu, (kd, tn), lambda i, j, k: (0, nj + j))],
                [], _glu_ep, m=m, n=n, kd=kd, tm=tm, tn=tn, tk=kd, out_dtype=BF16, name="s5_glu")


def _res_ln_kernel(*refs, ny, tm, mask_pad):
    y_refs = refs[:ny]
    r_ref, g_ref, b_ref, o_ref, ob_ref = refs[ny:]
    z = ALPHA * r_ref[...] + y_refs[0][...]
    for y_ref in y_refs[1:]:
        z = z + y_ref[...]
    mu = jnp.mean(z, axis=-1, keepdims=True)
    zc = z - mu
    var = jnp.mean(zc * zc, axis=-1, keepdims=True)
    out = zc * lax.rsqrt(var + LN_EPS) * g_ref[...] + b_ref[...]
    if mask_pad is not None:
        pos = pl.program_id(1) * tm + lax.broadcasted_iota(jnp.int32, (tm, 1), 0)
        out = jnp.where((pos >= mask_pad[0]) & (pos < mask_pad[1]), out, 0.0)
    o_ref[...] = out
    ob_ref[...] = out.astype(BF16)


def _res_ln(ys, res, g, b, mask_pad):
    bsz, lp, d = res.shape
    tm = _pick(lp, (256, 128, 64, 32, 16, 8))
    est = 2 * ((4 + len(ys)) * tm * d * 4) + 4 * tm * d * 4
    spec = pl.BlockSpec((None, tm, d), lambda bi, i: (bi, i, 0))
    vspec = pl.BlockSpec((1, d), lambda bi, i: (0, 0))
    return pl.pallas_call(
        functools.partial(_res_ln_kernel, ny=len(ys), tm=tm, mask_pad=mask_pad),
        name="res_ln",
        grid=(bsz, lp // tm),
        in_specs=[spec] * (len(ys) + 1) + [vspec, vspec],
        out_specs=[spec, spec],
        out_shape=[jax.ShapeDtypeStruct((bsz, lp, d), F32), jax.ShapeDtypeStruct((bsz, lp, d), BF16)],
        compiler_params=_params(("parallel", "parallel"), est),
    )(*[y.reshape(bsz, lp, d) for y in ys], res, g.reshape(1, d), b.reshape(1, d))


def _merge_kernel(o0, o1, o2, o3, wb_ref, glr_ref, wgu_ref, bg_ref, out_ref):
    glr = glr_ref[...].astype(BF16)
    acc = None
    for i, o_ref in enumerate((o0, o1, o2, o3)):
        gate = jax.nn.sigmoid(jnp.dot(glr, wgu_ref[i], preferred_element_type=F32) + bg_ref[i])
        term = gate * jnp.dot(o_ref[...], wb_ref[i], preferred_element_type=F32)
        acc = term if acc is None else acc + term
    out_ref[...] = acc.astype(out_ref.dtype)


def _merge(outs, w_branch, small, w_gate_up, b_gate):
    m, w = outs[0].shape
    nb, r, d = w_gate_up.shape
    tm = _pick(m, ROW_TILES)
    tn = _pick(d, (512, 256, 128))
    est = 2 * (nb * tm * w * 2 + nb * w * tn * 2 + tm * r * 4 + nb * r * tn * 2 + tm * tn * 2) + 4 * tm * tn * 4
    ospec = pl.BlockSpec((tm, w), lambda i, j: (i, 0))
    return pl.pallas_call(
        _merge_kernel,
        name="merge",
        grid=(m // tm, d // tn),
        in_specs=[ospec, ospec, ospec, ospec,
                  pl.BlockSpec((nb, w, tn), lambda i, j: (0, 0, j)),
                  pl.BlockSpec((tm, r), lambda i, j: (i, 0)),
                  pl.BlockSpec((nb, r, tn), lambda i, j: (0, 0, j)),
                  pl.BlockSpec((nb, 1, tn), lambda i, j: (0, 0, j))],
        out_specs=pl.BlockSpec((tm, tn), lambda i, j: (i, j)),
        out_shape=jax.ShapeDtypeStruct((m, d), BF16),
        compiler_params=_params(("parallel", "parallel"), est),
    )(*outs, w_branch, small, w_gate_up, b_gate.reshape(nb, 1, d))


def _ret_consts(lp):
    f32 = F32
    half = RET_DK // 2
    inv = 1.0 / (10000.0 ** jnp.linspace(0.0, 1.0, half, dtype=f32))
    ang = jnp.arange(lp, dtype=f32)[:, None] * inv[None, :]
    cos, sin = jnp.cos(ang), jnp.sin(ang)
    cos_t = jnp.tile(jnp.concatenate([cos, cos], axis=-1), (1, RET_HEADS))
    sin_t = jnp.tile(jnp.concatenate([-sin, sin], axis=-1), (1, RET_HEADS))
    log_gamma = jnp.log(1.0 - 2.0 ** (-5.0 - jnp.arange(RET_HEADS, dtype=f32)))
    pos = jnp.arange(BLOCK, dtype=f32)
    rel = pos[:, None] - pos[None, :]
    dmat = jnp.where(rel >= 0, jnp.exp(log_gamma[:, None, None] * jnp.maximum(rel, 0.0)), 0.0)
    k_decay = jnp.exp(log_gamma[:, None] * (BLOCK - 1.0 - pos)[None, :])
    q_decay = jnp.exp(log_gamma[:, None] * (pos + 1.0)[None, :])
    kd_t = jnp.repeat(k_decay.T, RET_DK, axis=1)
    qd_t = jnp.repeat(q_decay.T, RET_DK, axis=1)
    cd = jnp.exp(log_gamma * BLOCK)
    cd_t = jnp.broadcast_to(cd[:, None, None], (RET_HEADS, 1, RET_DV))
    return cos_t, sin_t, dmat, qd_t, kd_t, cd_t


def _ret_kernel(q_ref, k_ref, v_ref, g_ref, cos_ref, sin_ref, dmat_ref, qd_ref, kd_ref, cd_ref,
                o_ref, st_ref):
    @pl.when(pl.program_id(1) == 0)
    def _():
        st_ref[...] = jnp.zeros_like(st_ref)

    cos = cos_ref[...]
    sin = sin_ref[...]
    width = RET_HEADS * RET_DK
    lane = lax.broadcasted_iota(jnp.int32, (BLOCK, width), 1)
    first = (lane % RET_DK) < (RET_DK // 2)

    def rot(x):
        sw = jnp.where(first, pltpu.roll(x, width - RET_DK // 2, 1), pltpu.roll(x, RET_DK // 2, 1))
        return x * cos + sw * sin

    q = rot(q_ref[...])
    k = rot(k_ref[...]) * RET_DK ** -0.5
    qb = q.astype(BF16)
    kb = k.astype(BF16)
    qs = (q * qd_ref[...]).astype(BF16)
    ks = (k * kd_ref[...]).astype(BF16)
    for h in range(RET_HEADS):
        ksl = slice(h * RET_DK, (h + 1) * RET_DK)
        vsl = slice(h * RET_DV, (h + 1) * RET_DV)
        vh = v_ref[:, vsl].astype(BF16)
        s = lax.dot_general(qb[:, ksl], kb[:, ksl], (((1,), (1,)), ((), ())),
                            preferred_element_type=F32) * dmat_ref[h]
        st = st_ref[h]
        y = (jnp.dot(s.astype(BF16), vh, preferred_element_type=F32)
             + jnp.dot(qs[:, ksl], st.astype(BF16), preferred_element_type=F32))
        st_ref[h] = st * cd_ref[h] + lax.dot_general(ks[:, ksl], vh, (((0,), (0,)), ((), ())),
                                                     preferred_element_type=F32)
        mu = jnp.mean(y, axis=-1, keepdims=True)
        yc = y - mu
        var = jnp.mean(yc * yc, axis=-1, keepdims=True)
        gh = g_ref[:, vsl]
        o_ref[:, vsl] = (gh * jax.nn.sigmoid(gh) * yc * lax.rsqrt(var + 1e-6)).astype(o_ref.dtype)


def _retention(proj):
    bsz, lp = proj.shape[:2]
    kw = RET_HEADS * RET_DK
    cos_t, sin_t, dmat, qd_t, kd_t, cd_t = _ret_consts(lp)
    est = 2 * BLOCK * 4 * (2 * kw + 2 * BRANCH_WIDTH + 4 * kw) + 2 * RET_HEADS * BLOCK * BLOCK * 4 + (8 << 20)
    tab = pl.BlockSpec((BLOCK, kw), lambda b, n: (n, 0))
    ctab = pl.BlockSpec((BLOCK, kw), lambda b, n: (0, 0))
    return pl.pallas_call(
        _ret_kernel,
        name="retention",
        grid=(bsz, lp // BLOCK),
        in_specs=[pl.BlockSpec((None, BLOCK, kw), lambda b, n: (b, n, 0)),
                  pl.BlockSpec((None, BLOCK, kw), lambda b, n: (b, n, 1)),
                  pl.BlockSpec((None, BLOCK, BRANCH_WIDTH), lambda b, n: (b, n, 1)),
                  pl.BlockSpec((None, BLOCK, BRANCH_WIDTH), lambda b, n: (b, n, 2)),
                  tab, tab,
                  pl.BlockSpec((RET_HEADS, BLOCK, BLOCK), lambda b, n: (0, 0, 0)),
                  ctab, ctab,
                  pl.BlockSpec((RET_HEADS, 1, RET_DV), lambda b, n: (0, 0, 0))],
        out_specs=pl.BlockSpec((None, BLOCK, BRANCH_WIDTH), lambda b, n: (b, n, 0)),
        out_shape=jax.ShapeDtypeStruct((bsz, lp, BRANCH_WIDTH), BF16),
        scratch_shapes=[pltpu.VMEM((RET_HEADS, RET_DK, RET_DV), F32)],
        compiler_params=_params(("parallel", "arbitrary"), est),
    )(proj, proj, proj, proj, cos_t, sin_t, dmat, qd_t, kd_t, cd_t)


def _split_dot(x, ones):
    hi = x.astype(BF16)
    lo = (x - hi.astype(F32)).astype(BF16)
    return (jnp.dot(hi, ones, preferred_element_type=F32) + jnp.dot(lo, ones, preferred_element_type=F32))


_RV_MU, _RV_W0, _RV_A0, _RV_V0, _RV_KK, _RV_KA, _RV_RK = 0, 6, 7, 8, 9, 10, 11
_RV_ROWS = 16


def _rwkv_pre_kernel(*refs, has_vres):
    (u_ref, up_ref, wr_ref, wk_ref, wv_ref, w1_ref, w2_ref, a1_ref, a2_ref, g1_ref, g2_ref,
     ones_ref, vec_ref) = refs[:13]
    pos = 13
    if has_vres:
        vf_ref, v1_ref, v2_ref = refs[13:16]
        pos = 16
    r_out, lw_out, k_out, v_out, a_out, b_out, gate_out, bonus_out = refs[pos:pos + 8]

    def vec(i):
        return vec_ref[i:i + 1, :]

    u = u_ref[...]
    tm = u.shape[0]
    prev_last = jnp.where(pl.program_id(1) == 0, 0.0, up_ref[7:8, :])
    row = lax.broadcasted_iota(jnp.int32, (tm, 1), 0)
    prev = jnp.where(row == 0, prev_last, pltpu.roll(u, 1, 0))
    xx = prev - u

    def mix(i):
        return (u + xx * vec(_RV_MU + i)).astype(BF16)

    def mm(x, w_ref):
        return jnp.dot(x, w_ref[...], preferred_element_type=F32)

    xr, xw, xk, xv, xa, xg = [mix(i) for i in range(6)]
    r = mm(xr, wr_ref)
    k = mm(xk, wk_ref)
    v = mm(xv, wv_ref)
    z = -(vec(_RV_W0) + mm(jnp.tanh(mm(xw, w1_ref)).astype(BF16), w2_ref))
    softplus = jnp.maximum(z, 0.0) + jnp.log(1.0 + jnp.exp(-jnp.abs(z)))
    lw_out[...] = -jnp.exp(-softplus - 0.5)
    if has_vres:
        mv = jax.nn.sigmoid(vec(_RV_V0) + mm(mm(xv, v1_ref).astype(BF16), v2_ref))
        v = v + (vf_ref[...] - v) * mv
    a = jax.nn.sigmoid(vec(_RV_A0) + mm(mm(xa, a1_ref).astype(BF16), a2_ref))
    gate_out[...] = mm(jax.nn.sigmoid(mm(xg, g1_ref)).astype(BF16), g2_ref)

    ones = ones_ref[...]
    kk = k * vec(_RV_KK)
    kk = kk / jnp.maximum(jnp.sqrt(_split_dot(kk * kk, ones)), 1e-12)
    k = k * (1.0 + (a - 1.0) * vec(_RV_KA))
    r_out[...] = r
    k_out[...] = k
    v_out[...] = v
    a_out[...] = -kk
    b_out[...] = kk * a
    bonus_out[...] = _split_dot(r * k * vec(_RV_RK), ones) * v


def _rwkv_pre(proj, u_blk, v_first, vres, mu, w_rkv, w0, w1, w2, a0, a1, a2, g1, g2, k_k, k_a, r_k):
    bsz, lp = proj.shape[:2]
    w = BRANCH_WIDTH
    tm = _pick(lp, (256, 128, 64, 32, 16, 8))
    has_vres = vres is not None

    def lora_pair(wa, wb):
        rp = -(-wa.shape[1] // LANE) * LANE
        return (jnp.pad(wa, ((0, 0), (0, rp - wa.shape[1]))).astype(BF16),
                jnp.pad(wb, ((0, rp - wb.shape[0]), (0, 0))).astype(BF16))

    w1p, w2p = lora_pair(w1, w2)
    a1p, a2p = lora_pair(a1, a2)
    g1p, g2p = lora_pair(g1, g2)
    hid = jnp.arange(w) // RWKV_HEAD
    ones = (hid[:, None] == hid[None, :]).astype(BF16)
    zero = jnp.zeros((w,), F32)
    rows = [mu[i] for i in range(6)] + [w0, a0, vres[0] if has_vres else zero, k_k, k_a, r_k.reshape(w)]
    vec = jnp.stack(rows + [zero] * (_RV_ROWS - len(rows)))

    tile = pl.BlockSpec((None, tm, w), lambda b, i: (b, i, 0))
    const = lambda arr: pl.BlockSpec(arr.shape, lambda b, i: (0,) * arr.ndim)
    weights = [w_rkv[0].astype(BF16), w_rkv[1].astype(BF16), w_rkv[2].astype(BF16),
               w1p, w2p, a1p, a2p, g1p, g2p, ones, vec]
    args = [proj, proj] + weights
    in_specs = [pl.BlockSpec((None, tm, w), lambda b, i: (b, i, u_blk)),
                pl.BlockSpec((None, 8, w), lambda b, i: (b, jnp.maximum(i * (tm // 8) - 1, 0), u_blk))]
    in_specs += [const(x) for x in weights]
    if has_vres:
        v1p, v2p = lora_pair(vres[1], vres[2])
        args += [v_first, v1p, v2p]
        in_specs += [tile, const(v1p), const(v2p)]
    est = 2 * sum(x.size * x.dtype.itemsize for x in weights) + 2 * 11 * tm * w * 4 + 40 * tm * w * 4
    outs = pl.pallas_call(
        functools.partial(_rwkv_pre_kernel, has_vres=has_vres),
        name="rwkv_pre",
        grid=(bsz, lp // tm),
        in_specs=in_specs,
        out_specs=[tile] * 8,
        out_shape=[jax.ShapeDtypeStruct((bsz, lp, w), F32)] * 8,
        compiler_params=_params(("parallel", "parallel"), est),
    )(*args)
    return outs


def _wkv_kernel(r_ref, lw_ref, k_ref, v_ref, a_ref, b_ref, gate_ref, bonus_ref, lnw_ref, lnb_ref,
                o_ref, ht_ref):
    c = RWKV_CHUNK
    wd = RWKV_QW

    @pl.when(pl.program_id(1) == 0)
    def _():
        ht_ref[...] = jnp.zeros_like(ht_ref)

    ti = lax.broadcasted_iota(jnp.int32, (c, c), 0)
    tj = lax.broadcasted_iota(jnp.int32, (c, c), 1)
    cum_all = jnp.dot((ti >= tj).astype(F32), lw_ref[...], preferred_element_type=F32,
                      precision=lax.Precision.HIGHEST)

    ri = lax.broadcasted_iota(jnp.int32, (wd, wd), 0)
    ci = lax.broadcasted_iota(jnp.int32, (wd, wd), 1)
    same = (ri // c) == (ci // RWKV_HEAD)
    strict = same & ((ci % c) < (ri % c))
    incl = same & ((ci % c) <= (ri % c))
    ones_blk = jnp.where(same, 1.0, 0.0).astype(BF16)

    def expand(x):
        xb = x.astype(BF16)
        return jnp.where(same, jnp.concatenate([xb] * RWKV_QUAD, axis=0), jnp.zeros((), BF16))

    def nt(x, y):
        return lax.dot_general(x, y, (((1,), (1,)), ((), ())), preferred_element_type=F32)

    def tn(x, y):
        return lax.dot_general(x, y, (((0,), (0,)), ((), ())), preferred_element_type=F32)

    def mm(x, y):
        return jnp.dot(x, y, preferred_element_type=F32)

    groups = range(r_ref.shape[1] // wd)
    sls = [slice(q * wd, (q + 1) * wd) for q in groups]
    ar, bt, kt, vb, bw, kw, w_last = [], [], [], [], [], [], []
    for sl in sls:
        lw = lw_ref[:, sl]
        cum = cum_all[:, sl]
        cum_last = cum[c - 1:c, :]
        e_neg = jnp.exp(-cum)
        e_rem = jnp.exp(cum_last - cum)
        k = k_ref[:, sl]
        b = b_ref[:, sl]
        ar.append(jnp.concatenate([expand(a_ref[:, sl] * jnp.exp(cum - lw)),
                                   expand(r_ref[:, sl] * jnp.exp(cum))], axis=0))
        bt.append(expand(b * e_neg))
        kt.append(expand(k * e_neg))
        vb.append(expand(v_ref[:, sl]))
        bw.append(expand(b * e_rem))
        kw.append(expand(k * e_rem))
        w_last.append(jnp.exp(cum_last))

    g_b = [nt(ar[q], bt[q]) for q in groups]
    g_k = [nt(ar[q], kt[q]) for q in groups]
    hts = [ht_ref[q] for q in groups]
    g_h = [nt(ar[q], hts[q].astype(BF16)) for q in groups]
    p = [jnp.where(strict, g_b[q][:wd], 0.0) for q in groups]
    a_ak = [jnp.where(strict, g_k[q][:wd], 0.0).astype(BF16) for q in groups]
    a_rb = [jnp.where(incl, g_b[q][wd:], 0.0).astype(BF16) for q in groups]
    a_rk = [jnp.where(incl, g_k[q][wd:], 0.0).astype(BF16) for q in groups]
    u = [g_h[q][:wd] + mm(a_ak[q], vb[q]) for q in groups]
    steps = int(math.log2(c))
    for i in range(steps):
        pb = [x.astype(BF16) for x in p]
        u = [u[q] + mm(pb[q], u[q].astype(BF16)) for q in groups]
        if i < steps - 1:
            p = [mm(pb[q], pb[q]) for q in groups]
    ub = [x.astype(BF16) for x in u]
    y_bd = [g_h[q][wd:] + mm(a_rb[q], ub[q]) + mm(a_rk[q], vb[q]) for q in groups]
    for q in groups:
        ht_ref[q] = hts[q] * w_last[q] + tn(ub[q], bw[q]) + tn(vb[q], kw[q])
    for q in groups:
        sl = sls[q]
        y = y_bd[q][0:c]
        for h in range(1, RWKV_QUAD):
            y = y + y_bd[q][h * c:(h + 1) * c]
        mean = _split_dot(y, ones_blk) * (1.0 / RWKV_HEAD)
        yc = y - mean
        var = _split_dot(yc * yc, ones_blk) * (1.0 / RWKV_HEAD)
        out = yc * lax.rsqrt(var + RWKV_GN_EPS) * lnw_ref[:, sl] + lnb_ref[:, sl] + bonus_ref[:, sl]
        o_ref[:, sl] = (out * gate_ref[:, sl]).astype(o_ref.dtype)


def _wkv7(r, lw, k, v, a, b, gate, bonus, ln_w, ln_b):
    bsz, lp, w = r.shape
    c = RWKV_CHUNK
    spec = pl.BlockSpec((None, c, w), lambda bi, n: (bi, n, 0))
    vspec = pl.BlockSpec((1, w), lambda bi, n: (0, 0))
    est = 2 * 9 * c * w * 4 + 4 * 40 * RWKV_QW * RWKV_QW * 4
    return pl.pallas_call(
        _wkv_kernel,
        name="wkv7",
        grid=(bsz, lp // c),
        in_specs=[spec] * 8 + [vspec, vspec],
        out_specs=spec,
        out_shape=jax.ShapeDtypeStruct((bsz, lp, w), BF16),
        scratch_shapes=[pltpu.VMEM((w // RWKV_QW, RWKV_QW, RWKV_QW), F32)],
        compiler_params=_params(("parallel", "arbitrary"), est),
    )(r, lw, k, v, a, b, gate, bonus, ln_w.reshape(1, w), ln_b.reshape(1, w))


def _fox_kernel(q_ref, k_ref, v_ref, ck_ref, o_ref, acc_ref, st_a, st_b, pt_a, pt_b, qt_ref):
    tk, sw = FOX_TK, FOX_SUB
    tq = q_ref.shape[0]
    nsub = tq // sw
    qi = pl.program_id(2)
    qt_ref[...] = q_ref[...].astype(F32).T.astype(BF16)
    tri = (lax.broadcasted_iota(jnp.int32, (tk, sw), 0) <= lax.broadcasted_iota(jnp.int32, (tk, sw), 1))
    ones = tuple(jnp.ones((1, sw), F32) for _ in range(nsub))

    def issue_scores(kb, st_out, first_sub=0):
        ks = pl.multiple_of(kb * tk, tk)
        st_out[:, first_sub * sw:] = jnp.dot(k_ref[pl.ds(ks, tk), :], qt_ref[:, first_sub * sw:],
                                             preferred_element_type=F32)

    def issue_pv(kb, pt_in, alphas, first_sub=0):
        ks = pl.multiple_of(kb * tk, tk)
        pv = lax.dot_general(v_ref[pl.ds(ks, tk), :], pt_in[:, first_sub * sw:], (((0,), (0,)), ((), ())),
                             preferred_element_type=F32)
        for r in range(first_sub, nsub):
            acc_ref[r] = alphas[r] * acc_ref[r] + pv[:, (r - first_sub) * sw:(r - first_sub + 1) * sw]

    def softmax(kb, st_in, pt_out, ml, first_sub=0, tri_sub=None):
        ck = jnp.broadcast_to(ck_ref[pl.ds(pl.multiple_of(kb * tk, tk), tk), :], (tk, sw))
        ml, alphas = list(ml), list(ones)
        for r in range(first_sub, nsub):
            s = st_in[:, r * sw:(r + 1) * sw] - ck
            if r == tri_sub:
                s = jnp.where(tri, s, NEG_INF)
            m, l = ml[r]
            m_new = jnp.maximum(m, jnp.max(s, axis=0, keepdims=True))
            alphas[r] = jnp.exp2(m - m_new)
            p = jnp.exp2(s - m_new)
            ml[r] = (m_new, alphas[r] * l + jnp.sum(p, axis=0, keepdims=True))
            pt_out[:, r * sw:(r + 1) * sw] = p.astype(BF16)
        return tuple(ml), tuple(alphas)

    bufs = ((st_a, pt_a), (st_b, pt_b))

    def step(kb, parity, alphas_prev, ml, first_sub=0, tri_sub=None, prev_first=0, next_first=0):
        (st_cur, pt_cur), (st_nxt, pt_prv) = bufs[parity], bufs[1 - parity]
        if next_first is not None:
            issue_scores(kb + 1, st_nxt, next_first)
        issue_pv(jnp.maximum(kb - 1, 0), pt_prv, alphas_prev, prev_first)
        return softmax(kb, st_cur, pt_cur, ml, first_sub, tri_sub)

    def body(i, carry):
        alphas, ml = carry
        ml, alphas = step(2 * i, 0, alphas, ml)
        ml, alphas = step(2 * i + 1, 1, alphas, ml)
        return alphas, ml

    acc_ref[...] = jnp.zeros_like(acc_ref)
    pt_b[...] = jnp.zeros_like(pt_b)
    issue_scores(0, st_a)
    n_main = qi * (tq // tk)
    init_ml = tuple((jnp.full((1, sw), NEG_INF, F32), jnp.zeros((1, sw), F32)) for _ in range(nsub))
    alphas, ml = lax.fori_loop(0, n_main // 2, body, (ones, init_ml))
    ndiag = tq // tk
    for d in range(ndiag):
        ml, alphas = step(n_main + d, d % 2, alphas, ml, first_sub=d, tri_sub=d, prev_first=max(d - 1, 0),
                          next_first=d + 1 if d + 1 < ndiag else None)
    issue_pv(n_main + ndiag - 1, bufs[(ndiag - 1) % 2][1], alphas, ndiag - 1)
    for r in range(nsub):
        o_ref[r * sw:(r + 1) * sw, :] = (acc_ref[r] / ml[r][1]).T.astype(o_ref.dtype)


def _fox(qkv, c, lp_real):
    bsz, lp = qkv.shape[:2]
    h, dh, w = FOX_HEADS, FOX_DH, BRANCH_WIDTH
    tq = FOX_TQ
    pos = jnp.arange(lp)
    ck = (jnp.where((pos >= N_PAD) & (pos < lp_real), c, -NEG_INF) * LOG2E)[..., None]
    est = 2 * (2 * tq * dh * 2 + 2 * lp * dh * 2 + lp * LANE * 4) + (16 << 20)
    return pl.pallas_call(
        _fox_kernel,
        name="fox",
        grid=(bsz, h, lp // tq),
        in_specs=[pl.BlockSpec((None, tq, dh), lambda b, hh, i: (b, i, hh)),
                  pl.BlockSpec((None, lp, dh), lambda b, hh, i: (b, 0, h + hh)),
                  pl.BlockSpec((None, lp, dh), lambda b, hh, i: (b, 0, 2 * h + hh)),
                  pl.BlockSpec((None, None, lp, 1), lambda b, hh, i: (b, hh, 0, 0))],
        out_specs=pl.BlockSpec((None, tq, dh), lambda b, hh, i: (b, i, hh)),
        out_shape=jax.ShapeDtypeStruct((bsz, lp, w), BF16),
        scratch_shapes=[pltpu.VMEM((tq // FOX_SUB, dh, FOX_SUB), F32),
                        pltpu.VMEM((FOX_TK, tq), F32), pltpu.VMEM((FOX_TK, tq), F32),
                        pltpu.VMEM((FOX_TK, tq), BF16), pltpu.VMEM((FOX_TK, tq), BF16),
                        pltpu.VMEM((dh, tq), BF16)],
        compiler_params=_params(("parallel", "parallel", "parallel"), est),
    )(qkv, qkv, qkv, ck)


S5_ROWS = 16
S5_COLS = S5_GROUPS * S5_STATE // 8


def _s5_scan_kernel(x_ref, a_ref, o_ref, st_ref, *, tc):
    @pl.when(pl.program_id(1) == 0)
    def _():
        st_ref[...] = jnp.zeros_like(st_ref)

    ar = a_ref[0:8]
    ai = a_ref[8:16]

    def body(t, carry):
        hr, hi = carry
        x = x_ref[t]
        nr = ar * hr - ai * hi + x[0:8]
        ni = ar * hi + ai * hr + x[8:16]
        o_ref[t] = jnp.concatenate([nr, ni], axis=0).astype(o_ref.dtype)
        return nr, ni

    hr, hi = lax.fori_loop(0, tc, body, (st_ref[0:8], st_ref[8:16]), unroll=8)
    st_ref[0:8] = hr
    st_ref[8:16] = hi


def _s5_scan(bu, a_bar):
    bsz, lp = bu.shape[:2]
    tc = _pick(lp, (128, 64, 32, 16, 8))
    est = 2 * tc * S5_ROWS * S5_COLS * (4 + 2) + (4 << 20)
    return pl.pallas_call(
        functools.partial(_s5_scan_kernel, tc=tc),
        name="s5_scan",
        grid=(bsz, lp // tc),
        in_specs=[pl.BlockSpec((None, tc, S5_ROWS, S5_COLS), lambda b, n: (b, n, 0, 0)),
                  pl.BlockSpec((S5_ROWS, S5_COLS), lambda b, n: (0, 0))],
        out_specs=pl.BlockSpec((None, tc, S5_ROWS, S5_COLS), lambda b, n: (b, n, 0, 0)),
        out_shape=jax.ShapeDtypeStruct(bu.shape, BF16),
        scratch_shapes=[pltpu.VMEM((S5_ROWS, S5_COLS), F32)],
        compiler_params=_params(("parallel", "arbitrary"), est),
    )(bu, a_bar)


def _s5_weights(a_re, a_im, b_re, b_im, c_re, c_im, log_step):
    g, p, cg = S5_GROUPS, S5_STATE, S5_GROUP
    dt = jnp.exp(log_step)[:, None]
    mag = jnp.exp(a_re * dt)
    bar_re = mag * jnp.cos(a_im * dt)
    bar_im = mag * jnp.sin(a_im * dt)
    den = a_re * a_re + a_im * a_im
    num_re = bar_re - 1.0
    coef_re = (num_re * a_re + bar_im * a_im) / den
    coef_im = (bar_im * a_re - num_re * a_im) / den
    bb_re = coef_re[..., None] * b_re - coef_im[..., None] * b_im
    bb_im = coef_re[..., None] * b_im + coef_im[..., None] * b_re
    eye = jnp.eye(g, dtype=F32)
    bblk = jnp.concatenate([jnp.einsum('gpc,gh->gchp', m, eye).reshape(g * cg, g * p) for m in (bb_re, bb_im)],
                           axis=1)
    cblk = jnp.concatenate([jnp.einsum('gcp,gh->gphc', m, eye).reshape(g * p, g * cg) for m in (c_re, -c_im)],
                           axis=0)
    a_bar = jnp.concatenate([bar_re.reshape(8, S5_COLS), bar_im.reshape(8, S5_COLS)], axis=0)
    return bblk.astype(BF16), cblk.astype(BF16), a_bar


def _s5_out_ep(y, u, d):
    return jax.nn.gelu(y + d * u)


def _s5_branch(proj2, u_blk, bsz, lp, a_re, a_im, b_re, b_im, c_re, c_im, d_skip, log_step, w_glu):
    w = BRANCH_WIDTH
    t = bsz * lp
    bblk, cblk, a_bar = _s5_weights(a_re, a_im, b_re, b_im, c_re, c_im, log_step)
    bu = _mm(proj2, bblk, a_blk=u_blk, name="s5_bu")
    hs = _s5_scan(bu.reshape(bsz, lp, S5_ROWS, S5_COLS), a_bar)
    y = _mm(hs.reshape(t, S5_ROWS * S5_COLS), cblk, out_dtype=BF16, ep=_s5_out_ep,
            extras=((proj2, "tile", u_blk), (d_skip.reshape(1, w), "row", 0)), name="s5_out")
    return _glu(y, w_glu.astype(BF16))


def _pad_cols(w, n):
    return jnp.pad(w, ((0, 0), (0, n - w.shape[1])))


def _scale_cols(x, s):
    return x * s


MOE_TM = 512


def _moe_up_kernel(te_ref, x_ref, wg_ref, wu_ref, s_ref, o_ref, *, ni):
    i = pl.program_id(1)

    @pl.when(i < te_ref[ni])
    def _():
        x = x_ref[...]
        g = jnp.dot(x, wg_ref[...], preferred_element_type=F32)
        u = jnp.dot(x, wu_ref[...], preferred_element_type=F32)
        o_ref[...] = (g * jax.nn.sigmoid(g) * u * s_ref[...]).astype(o_ref.dtype)

    @pl.when(i >= te_ref[ni])
    def _():
        o_ref[...] = jnp.zeros_like(o_ref)


def _moe_down_kernel(te_ref, a_ref, w_ref, o_ref, *, ni):
    i = pl.program_id(1)

    @pl.when(i < te_ref[ni])
    def _():
        o_ref[...] = jnp.dot(a_ref[...], w_ref[...], preferred_element_type=F32).astype(o_ref.dtype)

    @pl.when(i >= te_ref[ni])
    def _():
        o_ref[...] = jnp.zeros_like(o_ref)


def _moe_experts(xs, row_scale, tile_info, w_gate, w_up, w_down):
    r, d = xs.shape
    e, _, f = w_gate.shape
    tm = MOE_TM
    ni = r // tm
    tn = _pick(f, (512, 256, 128))
    wmap = lambda j, i, te: (te[i], 0, j)
    act = pl.pallas_call(
        functools.partial(_moe_up_kernel, ni=ni),
        name="moe_up",
        grid_spec=pltpu.PrefetchScalarGridSpec(
            num_scalar_prefetch=1, grid=(f // tn, ni),
            in_specs=[pl.BlockSpec((tm, d), lambda j, i, te: (i, 0)),
                      pl.BlockSpec((None, d, tn), wmap), pl.BlockSpec((None, d, tn), wmap),
                      pl.BlockSpec((tm, 1), lambda j, i, te: (i, 0))],
            out_specs=pl.BlockSpec((tm, tn), lambda j, i, te: (i, j))),
        out_shape=jax.ShapeDtypeStruct((r, f), BF16),
        compiler_params=_params(("arbitrary", "arbitrary"),
                                2 * (tm * d * 2 + 2 * d * tn * 2 + tm * tn * 2) + 4 * tm * tn * 4),
    )(tile_info, xs, w_gate, w_up, row_scale)
    tn2 = _pick(d, (1024, 512, 256, 128))
    return pl.pallas_call(
        functools.partial(_moe_down_kernel, ni=ni),
        name="moe_down",
        grid_spec=pltpu.PrefetchScalarGridSpec(
            num_scalar_prefetch=1, grid=(d // tn2, ni),
            in_specs=[pl.BlockSpec((tm, f), lambda j, i, te: (i, 0)),
                      pl.BlockSpec((None, f, tn2), wmap)],
            out_specs=pl.BlockSpec((tm, tn2), lambda j, i, te: (i, j))),
        out_shape=jax.ShapeDtypeStruct((r, d), F32),
        compiler_params=_params(("arbitrary", "arbitrary"),
                                2 * (tm * f * 2 + f * tn2 * 2 + tm * tn2 * 4) + 2 * tm * tn2 * 4),
    )(tile_info, act, w_down)


def _moe_ffn(h32, hb, w_router, b_router, w_gate, w_up, w_down):
    t, d = hb.shape
    e = w_gate.shape[0]
    tm = MOE_TM
    logits = _mm(h32, _pad_cols(w_router, LANE), precision=lax.Precision.HIGHEST,
                 name="router")[:, :N_EXPERTS] + b_router
    top_vals, top_idx = lax.top_k(logits, TOP_K)
    weights = jax.nn.softmax(top_vals, axis=-1)

    n_assign = t * TOP_K
    n_rows = n_assign + e * tm
    flat_e = top_idx.reshape(n_assign).astype(jnp.int32)
    onehot = jax.nn.one_hot(flat_e, e, dtype=jnp.int32)
    seen = jnp.cumsum(onehot, axis=0)
    counts = seen[-1]
    padded = (counts + tm - 1) // tm * tm
    group_end = jnp.cumsum(padded)
    group_start = group_end - padded
    sorted_start = jnp.cumsum(counts) - counts
    dest = group_start[flat_e] + jnp.sum(seen * onehot, axis=1) - 1
    order = jnp.argsort(flat_e, stable=True).astype(jnp.int32)
    rows = jnp.arange(n_rows, dtype=jnp.int32)
    row_e = jnp.minimum(jnp.searchsorted(group_end, rows, side="right"), e - 1).astype(jnp.int32)
    rank = rows - group_start[row_e]
    used = rank < counts[row_e]
    assign = order[jnp.where(used, sorted_start[row_e] + rank, 0)]
    src_tok = jnp.where(used, assign // TOP_K, 0)
    row_scale = jnp.where(used, weights.reshape(n_assign)[assign], 0.0)
    tile_info = jnp.concatenate([row_e[::tm], (group_end[-1:] // tm).astype(jnp.int32)])

    xs = jnp.take(hb, src_tok, axis=0)
    y = _moe_experts(xs, row_scale[:, None], tile_info, w_gate.astype(BF16), w_up.astype(BF16),
                     w_down.astype(BF16))
    dest = dest.reshape(t, TOP_K)
    return [jnp.take(y, dest[:, s], axis=0) for s in range(TOP_K)]


def kernel(x, meta_tokens, w_in, b_forget, w_gate_down, w_gate_up, b_gate, w_branch, w_out, ln_mix_g, ln_mix_b, ln_ffn_g, ln_ffn_b, rwkv_mu, rwkv_w_rkv, rwkv_w0, rwkv_w1, rwkv_w2, rwkv_a0, rwkv_a1, rwkv_a2, rwkv_v0, rwkv_v1, rwkv_v2, rwkv_g1, rwkv_g2, rwkv_k_k, rwkv_k_a, rwkv_r_k, rwkv_ln_w, rwkv_ln_b, s5_a_re, s5_a_im, s5_b_re, s5_b_im, s5_c_re, s5_c_im, s5_d, s5_log_step, s5_w_glu, ffn_w_gate, ffn_w_up, ffn_w_down, moe_router, moe_router_b, moe_w_gate, moe_w_up, moe_w_down):
    bsz, seq, d = x.shape
    depth = w_in.shape[0]
    w = BRANCH_WIDTH
    lp_real = N_PAD + N_META + seq
    lp = -(-lp_real // FOX_TQ) * FOX_TQ
    t = bsz * lp
    meta = jnp.broadcast_to(meta_tokens.astype(x.dtype)[None], (bsz, N_META, d))
    h = jnp.concatenate([jnp.zeros((bsz, N_PAD, d), x.dtype), meta, x,
                         jnp.zeros((bsz, lp - lp_real, d), x.dtype)], axis=1)
    hb = h.astype(BF16)
    pos = jnp.arange(lp)
    real = ((pos >= N_PAD) & (pos < lp_real))[None, :, None]
    f_lo = 7 * w
    q_scale = jnp.concatenate([jnp.full((1, w), FOX_DH ** -0.5 * LOG2E, F32), jnp.ones((1, 2 * w), F32)], axis=1)
    v_first = None
    for layer in range(depth):
        wl = w_in[layer]
        w_f32 = jnp.concatenate([wl[:, :4 * w], wl[:, f_lo + FOX_HEADS:]], axis=1).astype(BF16)
        w_fox = wl[:, 4 * w:f_lo].astype(BF16)
        w_small = _pad_cols(jnp.concatenate([w_gate_down[layer], wl[:, f_lo:f_lo + FOX_HEADS]], axis=1),
                            GATE_RANK + LANE).astype(BF16)
        hb2 = hb.reshape(t, d)
        proj2 = _mm(hb2, w_f32, name="proj")
        proj = proj2.reshape(bsz, lp, 5 * w)
        qkv = _mm(hb2, w_fox, out_dtype=BF16, ep=_scale_cols, extras=((q_scale, "row", 0),),
                  name="proj_fox").reshape(bsz, lp, 3 * w)
        small = _mm(hb2, w_small, name="proj_small")
        f_logit = small[:, GATE_RANK:GATE_RANK + FOX_HEADS].reshape(bsz, lp, FOX_HEADS)

        o_ret = _retention(proj)

        vres = None if layer == 0 else (rwkv_v0[layer - 1], rwkv_v1[layer - 1], rwkv_v2[layer - 1])
        r, lw, k, v, a, b, gate, bonus = _rwkv_pre(
            proj, 3, v_first, vres, rwkv_mu[layer], rwkv_w_rkv[layer], rwkv_w0[layer], rwkv_w1[layer],
            rwkv_w2[layer], rwkv_a0[layer], rwkv_a1[layer], rwkv_a2[layer], rwkv_g1[layer], rwkv_g2[layer],
            rwkv_k_k[layer], rwkv_k_a[layer], rwkv_r_k[layer])
        if layer == 0:
            v_first = v
        o_rwkv = _wkv7(r, lw, k, v, a, b, gate, bonus, rwkv_ln_w[layer], rwkv_ln_b[layer])

        log_f = jnp.where(real, jax.nn.log_sigmoid(f_logit + b_forget[layer]), 0.0)
        c = jnp.cumsum(log_f, axis=1).transpose(0, 2, 1)
        o_fox = _fox(qkv, c, lp_real)

        o_s5 = _s5_branch(proj2, 4, bsz, lp, s5_a_re[layer], s5_a_im[layer], s5_b_re[layer], s5_b_im[layer],
                          s5_c_re[layer], s5_c_im[layer], s5_d[layer], s5_log_step[layer], s5_w_glu[layer])

        merged = _merge([o_ret.reshape(t, w), o_rwkv.reshape(t, w), o_fox.reshape(t, w), o_s5],
                        w_branch[layer].astype(BF16), small, w_gate_up[layer].astype(BF16), b_gate[layer])
        mix = _mm(merged, w_out[layer].astype(BF16), name="w_out")
        h, hb = _res_ln([mix], h, ln_mix_g[layer], ln_mix_b[layer], mask_pad=None)

        j = layer // 2
        hb2 = hb.reshape(t, d)
        if layer % 2 == 0:
            act = _swiglu_up(hb2, ffn_w_gate[j].astype(BF16), ffn_w_up[j].astype(BF16))
            ffn = [_mm(act, ffn_w_down[j].astype(BF16), name="ffn_down")]
        else:
            ffn = _moe_ffn(h.reshape(t, d), hb2, moe_router[j], moe_router_b[j],
                           moe_w_gate[j], moe_w_up[j], moe_w_down[j])
        h, hb = _res_ln(ffn, h, ln_ffn_g[layer], ln_ffn_b[layer], mask_pad=(N_PAD, lp_real))
    return h[:, N_PAD + N_META:lp_real]
```

```python
import functools
import math

import jax
import jax.numpy as jnp
from jax import lax
from jax.experimental import pallas as pl
from jax.experimental.pallas import tpu as pltpu

F32 = jnp.float32
BF16 = jnp.bfloat16

D_MODEL = 4096
DEPTH = 4
N_META = 16
BLOCK = 128
N_PAD = (-N_META) % BLOCK
BRANCH_WIDTH = D_MODEL // 4
GATE_RANK = 256
RET_DK = 64
RET_DV = 128
RET_HEADS = BRANCH_WIDTH // RET_DV
RWKV_HEAD = 64
RWKV_HEADS = BRANCH_WIDTH // RWKV_HEAD
RWKV_GN_EPS = 64e-5
FOX_DH = 128
FOX_HEADS = BRANCH_WIDTH // FOX_DH
S5_GROUP = 16
S5_GROUPS = BRANCH_WIDTH // S5_GROUP
S5_STATE = 64
N_EXPERTS = 8
TOP_K = 2
ALPHA = (2.0 * DEPTH) ** 0.25
LN_EPS = 1e-5
NEG_INF = -1e30
LOG2E = math.log2(math.e)

V7X_VMEM_BUDGET = 58 * 1024 * 1024
LANE = 128
RWKV_CHUNK = 64
RWKV_QUAD = 4
RWKV_QW = RWKV_QUAD * RWKV_HEAD
FOX_TK = 128
FOX_SUB = 128
FOX_TQ = 768
ROW_TILES = (1408, 1536, 1280, 768, 640, 512, 256, 128)


def _pick(dim, candidates):
    for c in candidates:
        if dim % c == 0:
            return c
    return dim


def _params(sem, est_bytes):
    limit = int(min(V7X_VMEM_BUDGET, max(est_bytes * 5 // 4 + (2 << 20), 16 << 20)))
    return pltpu.CompilerParams(dimension_semantics=sem, vmem_limit_bytes=limit)


def _nbytes(shape, dtype):
    n = jnp.dtype(dtype).itemsize
    for s in shape:
        if s is not None:
            n *= s
    return n


def _mmf_kernel(*refs, nb, nx, nk, ep, precision):
    a_ref = refs[0]
    b_refs = refs[1:1 + nb]
    x_refs = refs[1 + nb:1 + nb + nx]
    o_ref = refs[1 + nb + nx]
    acc_refs = refs[2 + nb + nx:]
    a = a_ref[...]
    if precision is None and a.dtype != BF16:
        a = a.astype(BF16)
    dots = [jnp.dot(a, b[...], preferred_element_type=F32, precision=precision) for b in b_refs]

    def finish(vals):
        o_ref[...] = ep(*vals, *[x[...] for x in x_refs]).astype(o_ref.dtype)

    if nk == 1:
        finish(dots)
        return
    k = pl.program_id(2)

    @pl.when(k == 0)
    def _():
        for acc, d in zip(acc_refs, dots):
            acc[...] = d

    @pl.when(k > 0)
    def _():
        for acc, d in zip(acc_refs, dots):
            acc[...] += d

    @pl.when(k == nk - 1)
    def _():
        finish([acc[...] for acc in acc_refs])


def _mmf(a_spec, b_specs, x_specs, ep, *, m, n, kd, tm, tn, tk, out_dtype, precision=None, name="mm"):
    nk = kd // tk
    specs = [a_spec] + list(b_specs) + list(x_specs)
    est = 2 * sum(_nbytes(s[1], s[0].dtype) for s in specs) + 2 * tm * tn * jnp.dtype(out_dtype).itemsize
    est += (len(b_specs) + 2) * tm * tn * 4
    scratch = [pltpu.VMEM((tm, tn), F32) for _ in b_specs] if nk > 1 else []
    return pl.pallas_call(
        functools.partial(_mmf_kernel, nb=len(b_specs), nx=len(x_specs), nk=nk, ep=ep, precision=precision),
        name=name,
        grid=(m // tm, n // tn, nk),
        in_specs=[pl.BlockSpec(s[1], s[2]) for s in specs],
        out_specs=pl.BlockSpec((tm, tn), lambda i, j, k: (i, j)),
        out_shape=jax.ShapeDtypeStruct((m, n), out_dtype),
        scratch_shapes=scratch,
        compiler_params=_params(("parallel", "parallel", "arbitrary"), est),
    )(*[s[0] for s in specs])


def _ident(x):
    return x


def _mm(a, b, out_dtype=F32, precision=None, ep=_ident, extras=(), a_blk=0, name="mm"):
    m = a.shape[0]
    kd, n = b.shape
    tm = _pick(m, ROW_TILES)
    tn = _pick(n, (1024, 512, 384, 256, 128))
    tk = kd if kd <= 4096 else _pick(kd, (2048, 1536, 1024, 512))
    if precision is not None:
        tm = _pick(m, (640, 512, 256, 128))
        tn = _pick(n, (512, 384, 256, 128))
        tk = kd if kd <= 2048 else _pick(kd, (2048, 1024, 512))
    nk = kd // tk
    ob = jnp.dtype(out_dtype).itemsize

    def need(tn_):
        return (2 * (tm * tk * a.dtype.itemsize + tk * tn_ * b.dtype.itemsize + tm * tn_ * ob)
                + (3 if nk > 1 else 2) * tm * tn_ * 4)

    while need(tn) > V7X_VMEM_BUDGET - (6 << 20) and tn % 256 == 0 and n % (tn // 2) == 0:
        tn //= 2
    a_spec = (a, (tm, tk), lambda i, j, k: (i, a_blk * nk + k))
    b_spec = (b, (tk, tn), lambda i, j, k: (k, j))
    x_specs = []
    for arr, kind, off in extras:
        if kind == "tile":
            x_specs.append((arr, (tm, tn), lambda i, j, k, off=off: (i, j + off)))
        else:
            x_specs.append((arr, (1, tn), lambda i, j, k, off=off: (0, j + off)))
    return _mmf(a_spec, [b_spec], x_specs, ep, m=m, n=n, kd=kd, tm=tm, tn=tn, tk=tk,
                out_dtype=out_dtype, precision=precision, name=name)


def _silu_mul(g, u):
    return g * jax.nn.sigmoid(g) * u


def _swiglu_up(h, wg, wu):
    m, d = h.shape
    f = wg.shape[1]
    tm = _pick(m, ROW_TILES)
    tn = _pick(f, (512, 256, 128))
    wmap = lambda i, j, k: (0, j)
    return _mmf((h, (tm, d), lambda i, j, k: (i, 0)),
                [(wg, (d, tn), wmap), (wu, (d, tn), wmap)], [], _silu_mul,
                m=m, n=f, kd=d, tm=tm, tn=tn, tk=d, out_dtype=BF16, name="swiglu_up")


def _glu_ep(val, gt):
    return val * jax.nn.sigmoid(gt)


def _glu(y, w_glu):
    m, kd = y.shape
    n = w_glu.shape[1] // 2
    tm = _pick(m, ROW_TILES)
    tn = _pick(n, (512, 256, 128))
    nj = n // tn
    return _mmf((y, (tm, kd), lambda i, j, k: (i, 0)),
                [(w_glu, (kd, tn), lambda i, j, k: (0, j)), (w_glu, (kd, tn), lambda i, j, k: (0, nj + j))],
                [], _glu_ep, m=m, n=n, kd=kd, tm=tm, tn=tn, tk=kd, out_dtype=BF16, name="s5_glu")


def _res_ln_kernel(*refs, ny, tm, mask_pad):
    y_refs = refs[:ny]
    r_ref, g_ref, b_ref, o_ref, ob_ref = refs[ny:]
    z = ALPHA * r_ref[...] + y_refs[0][...]
    for y_ref in y_refs[1:]:
        z = z + y_ref[...]
    mu = jnp.mean(z, axis=-1, keepdims=True)
    zc = z - mu
    var = jnp.mean(zc * zc, axis=-1, keepdims=True)
    out = zc * lax.rsqrt(var + LN_EPS) * g_ref[...] + b_ref[...]
    if mask_pad is not None:
        pos = pl.program_id(1) * tm + lax.broadcasted_iota(jnp.int32, (tm, 1), 0)
        out = jnp.where((pos >= mask_pad[0]) & (pos < mask_pad[1]), out, 0.0)
    o_ref[...] = out
    ob_ref[...] = out.astype(BF16)


def _res_ln(ys, res, g, b, mask_pad):
    bsz, lp, d = res.shape
    tm = _pick(lp, (256, 128, 64, 32, 16, 8))
    est = 2 * ((4 + len(ys)) * tm * d * 4) + 4 * tm * d * 4
    spec = pl.BlockSpec((None, tm, d), lambda bi, i: (bi, i, 0))
    vspec = pl.BlockSpec((1, d), lambda bi, i: (0, 0))
    return pl.pallas_call(
        functools.partial(_res_ln_kernel, ny=len(ys), tm=tm, mask_pad=mask_pad),
        name="res_ln",
        grid=(bsz, lp // tm),
        in_specs=[spec] * (len(ys) + 1) + [vspec, vspec],
        out_specs=[spec, spec],
        out_shape=[jax.ShapeDtypeStruct((bsz, lp, d), F32), jax.ShapeDtypeStruct((bsz, lp, d), BF16)],
        compiler_params=_params(("parallel", "parallel"), est),
    )(*[y.reshape(bsz, lp, d) for y in ys], res, g.reshape(1, d), b.reshape(1, d))


def _merge_kernel(o0, o1, o2, o3, wb_ref, glr_ref, wgu_ref, bg_ref, out_ref):
    glr = glr_ref[...].astype(BF16)
    acc = None
    for i, o_ref in enumerate((o0, o1, o2, o3)):
        gate = jax.nn.sigmoid(jnp.dot(glr, wgu_ref[i], preferred_element_type=F32) + bg_ref[i])
        term = gate * jnp.dot(o_ref[...], wb_ref[i], preferred_element_type=F32)
        acc = term if acc is None else acc + term
    out_ref[...] = acc.astype(out_ref.dtype)


def _merge(outs, w_branch, small, w_gate_up, b_gate):
    m, w = outs[0].shape
    nb, r, d = w_gate_up.shape
    tm = _pick(m, ROW_TILES)
    tn = _pick(d, (512, 256, 128))
    est = 2 * (nb * tm * w * 2 + nb * w * tn * 2 + tm * r * 4 + nb * r * tn * 2 + tm * tn * 2) + 4 * tm * tn * 4
    ospec = pl.BlockSpec((tm, w), lambda i, j: (i, 0))
    return pl.pallas_call(
        _merge_kernel,
        name="merge",
        grid=(m // tm, d // tn),
        in_specs=[ospec, ospec, ospec, ospec,
                  pl.BlockSpec((nb, w, tn), lambda i, j: (0, 0, j)),
                  pl.BlockSpec((tm, r), lambda i, j: (i, 0)),
                  pl.BlockSpec((nb, r, tn), lambda i, j: (0, 0, j)),
                  pl.BlockSpec((nb, 1, tn), lambda i, j: (0, 0, j))],
        out_specs=pl.BlockSpec((tm, tn), lambda i, j: (i, j)),
        out_shape=jax.ShapeDtypeStruct((m, d), BF16),
        compiler_params=_params(("parallel", "parallel"), est),
    )(*outs, w_branch, small, w_gate_up, b_gate.reshape(nb, 1, d))


def _ret_consts(lp):
    f32 = F32
    half = RET_DK // 2
    inv = 1.0 / (10000.0 ** jnp.linspace(0.0, 1.0, half, dtype=f32))
    ang = jnp.arange(lp, dtype=f32)[:, None] * inv[None, :]
    cos, sin = jnp.cos(ang), jnp.sin(ang)
    cos_t = jnp.tile(jnp.concatenate([cos, cos], axis=-1), (1, RET_HEADS))
    sin_t = jnp.tile(jnp.concatenate([-sin, sin], axis=-1), (1, RET_HEADS))
    log_gamma = jnp.log(1.0 - 2.0 ** (-5.0 - jnp.arange(RET_HEADS, dtype=f32)))
    pos = jnp.arange(BLOCK, dtype=f32)
    rel = pos[:, None] - pos[None, :]
    dmat = jnp.where(rel >= 0, jnp.exp(log_gamma[:, None, None] * jnp.maximum(rel, 0.0)), 0.0)
    k_decay = jnp.exp(log_gamma[:, None] * (BLOCK - 1.0 - pos)[None, :])
    q_decay = jnp.exp(log_gamma[:, None] * (pos + 1.0)[None, :])
    kd_t = jnp.repeat(k_decay.T, RET_DK, axis=1)
    qd_t = jnp.repeat(q_decay.T, RET_DK, axis=1)
    cd = jnp.exp(log_gamma * BLOCK)
    cd_t = jnp.broadcast_to(cd[:, None, None], (RET_HEADS, 1, RET_DV))
    return cos_t, sin_t, dmat, qd_t, kd_t, cd_t


def _ret_kernel(q_ref, k_ref, v_ref, g_ref, cos_ref, sin_ref, dmat_ref, qd_ref, kd_ref, cd_ref,
                o_ref, st_ref):
    @pl.when(pl.program_id(1) == 0)
    def _():
        st_ref[...] = jnp.zeros_like(st_ref)

    cos = cos_ref[...]
    sin = sin_ref[...]
    width = RET_HEADS * RET_DK
    lane = lax.broadcasted_iota(jnp.int32, (BLOCK, width), 1)
    first = (lane % RET_DK) < (RET_DK // 2)

    def rot(x):
        sw = jnp.where(first, pltpu.roll(x, width - RET_DK // 2, 1), pltpu.roll(x, RET_DK // 2, 1))
        return x * cos + sw * sin

    q = rot(q_ref[...])
    k = rot(k_ref[...]) * RET_DK ** -0.5
    qb = q.astype(BF16)
    kb = k.astype(BF16)
    qs = (q * qd_ref[...]).astype(BF16)
    ks = (k * kd_ref[...]).astype(BF16)
    for h in range(RET_HEADS):
        ksl = slice(h * RET_DK, (h + 1) * RET_DK)
        vsl = slice(h * RET_DV, (h + 1) * RET_DV)
        vh = v_ref[:, vsl].astype(BF16)
        s = lax.dot_general(qb[:, ksl], kb[:, ksl], (((1,), (1,)), ((), ())),
                            preferred_element_type=F32) * dmat_ref[h]
        st = st_ref[h]
        y = (jnp.dot(s.astype(BF16), vh, preferred_element_type=F32)
             + jnp.dot(qs[:, ksl], st.astype(BF16), preferred_element_type=F32))
        st_ref[h] = st * cd_ref[h] + lax.dot_general(ks[:, ksl], vh, (((0,), (0,)), ((), ())),
                                                     preferred_element_type=F32)
        mu = jnp.mean(y, axis=-1, keepdims=True)
        yc = y - mu
        var = jnp.mean(yc * yc, axis=-1, keepdims=True)
        gh = g_ref[:, vsl]
        o_ref[:, vsl] = (gh * jax.nn.sigmoid(gh) * yc * lax.rsqrt(var + 1e-6)).astype(o_ref.dtype)


def _retention(proj):
    bsz, lp = proj.shape[:2]
    kw = RET_HEADS * RET_DK
    cos_t, sin_t, dmat, qd_t, kd_t, cd_t = _ret_consts(lp)
    est = 2 * BLOCK * 4 * (2 * kw + 2 * BRANCH_WIDTH + 4 * kw) + 2 * RET_HEADS * BLOCK * BLOCK * 4 + (8 << 20)
    tab = pl.BlockSpec((BLOCK, kw), lambda b, n: (n, 0))
    ctab = pl.BlockSpec((BLOCK, kw), lambda b, n: (0, 0))
    return pl.pallas_call(
        _ret_kernel,
        name="retention",
        grid=(bsz, lp // BLOCK),
        in_specs=[pl.BlockSpec((None, BLOCK, kw), lambda b, n: (b, n, 0)),
                  pl.BlockSpec((None, BLOCK, kw), lambda b, n: (b, n, 1)),
                  pl.BlockSpec((None, BLOCK, BRANCH_WIDTH), lambda b, n: (b, n, 1)),
                  pl.BlockSpec((None, BLOCK, BRANCH_WIDTH), lambda b, n: (b, n, 2)),
                  tab, tab,
                  pl.BlockSpec((RET_HEADS, BLOCK, BLOCK), lambda b, n: (0, 0, 0)),
                  ctab, ctab,
                  pl.BlockSpec((RET_HEADS, 1, RET_DV), lambda b, n: (0, 0, 0))],
        out_specs=pl.BlockSpec((None, BLOCK, BRANCH_WIDTH), lambda b, n: (b, n, 0)),
        out_shape=jax.ShapeDtypeStruct((bsz, lp, BRANCH_WIDTH), BF16),
        scratch_shapes=[pltpu.VMEM((RET_HEADS, RET_DK, RET_DV), F32)],
        compiler_params=_params(("parallel", "arbitrary"), est),
    )(proj, proj, proj, proj, cos_t, sin_t, dmat, qd_t, kd_t, cd_t)


def _split_dot(x, ones):
    hi = x.astype(BF16)
    lo = (x - hi.astype(F32)).astype(BF16)
    return (jnp.dot(hi, ones, preferred_element_type=F32) + jnp.dot(lo, ones, preferred_element_type=F32))


_RV_MU, _RV_W0, _RV_A0, _RV_V0, _RV_KK, _RV_KA, _RV_RK = 0, 6, 7, 8, 9, 10, 11
_RV_ROWS = 16


def _rwkv_pre_kernel(*refs, has_vres):
    (u_ref, up_ref, wr_ref, wk_ref, wv_ref, w1_ref, w2_ref, a1_ref, a2_ref, g1_ref, g2_ref,
     ones_ref, vec_ref) = refs[:13]
    pos = 13
    if has_vres:
        vf_ref, v1_ref, v2_ref = refs[13:16]
        pos = 16
    r_out, lw_out, k_out, v_out, a_out, b_out, gate_out, bonus_out = refs[pos:pos + 8]

    def vec(i):
        return vec_ref[i:i + 1, :]

    u = u_ref[...]
    tm = u.shape[0]
    prev_last = jnp.where(pl.program_id(1) == 0, 0.0, up_ref[7:8, :])
    row = lax.broadcasted_iota(jnp.int32, (tm, 1), 0)
    prev = jnp.where(row == 0, prev_last, pltpu.roll(u, 1, 0))
    xx = prev - u

    def mix(i):
        return (u + xx * vec(_RV_MU + i)).astype(BF16)

    def mm(x, w_ref):
        return jnp.dot(x, w_ref[...], preferred_element_type=F32)

    xr, xw, xk, xv, xa, xg = [mix(i) for i in range(6)]
    r = mm(xr, wr_ref)
    k = mm(xk, wk_ref)
    v = mm(xv, wv_ref)
    z = -(vec(_RV_W0) + mm(jnp.tanh(mm(xw, w1_ref)).astype(BF16), w2_ref))
    softplus = jnp.maximum(z, 0.0) + jnp.log(1.0 + jnp.exp(-jnp.abs(z)))
    lw_out[...] = -jnp.exp(-softplus - 0.5)
    if has_vres:
        mv = jax.nn.sigmoid(vec(_RV_V0) + mm(mm(xv, v1_ref).astype(BF16), v2_ref))
        v = v + (vf_ref[...] - v) * mv
    a = jax.nn.sigmoid(vec(_RV_A0) + mm(mm(xa, a1_ref).astype(BF16), a2_ref))
    gate_out[...] = mm(jax.nn.sigmoid(mm(xg, g1_ref)).astype(BF16), g2_ref)

    ones = ones_ref[...]
    kk = k * vec(_RV_KK)
    kk = kk / jnp.maximum(jnp.sqrt(_split_dot(kk * kk, ones)), 1e-12)
    k = k * (1.0 + (a - 1.0) * vec(_RV_KA))
    r_out[...] = r
    k_out[...] = k
    v_out[...] = v
    a_out[...] = -kk
    b_out[...] = kk * a
    bonus_out[...] = _split_dot(r * k * vec(_RV_RK), ones) * v


def _rwkv_pre(proj, u_blk, v_first, vres, mu, w_rkv, w0, w1, w2, a0, a1, a2, g1, g2, k_k, k_a, r_k):
    bsz, lp = proj.shape[:2]
    w = BRANCH_WIDTH
    tm = _pick(lp, (256, 128, 64, 32, 16, 8))
    has_vres = vres is not None

    def lora_pair(wa, wb):
        rp = -(-wa.shape[1] // LANE) * LANE
        return (jnp.pad(wa, ((0, 0), (0, rp - wa.shape[1]))).astype(BF16),
                jnp.pad(wb, ((0, rp - wb.shape[0]), (0, 0))).astype(BF16))

    w1p, w2p = lora_pair(w1, w2)
    a1p, a2p = lora_pair(a1, a2)
    g1p, g2p = lora_pair(g1, g2)
    hid = jnp.arange(w) // RWKV_HEAD
    ones = (hid[:, None] == hid[None, :]).astype(BF16)
    zero = jnp.zeros((w,), F32)
    rows = [mu[i] for i in range(6)] + [w0, a0, vres[0] if has_vres else zero, k_k, k_a, r_k.reshape(w)]
    vec = jnp.stack(rows + [zero] * (_RV_ROWS - len(rows)))

    tile = pl.BlockSpec((None, tm, w), lambda b, i: (b, i, 0))
    const = lambda arr: pl.BlockSpec(arr.shape, lambda b, i: (0,) * arr.ndim)
    weights = [w_rkv[0].astype(BF16), w_rkv[1].astype(BF16), w_rkv[2].astype(BF16),
               w1p, w2p, a1p, a2p, g1p, g2p, ones, vec]
    args = [proj, proj] + weights
    in_specs = [pl.BlockSpec((None, tm, w), lambda b, i: (b, i, u_blk)),
                pl.BlockSpec((None, 8, w), lambda b, i: (b, jnp.maximum(i * (tm // 8) - 1, 0), u_blk))]
    in_specs += [const(x) for x in weights]
    if has_vres:
        v1p, v2p = lora_pair(vres[1], vres[2])
        args += [v_first, v1p, v2p]
        in_specs += [tile, const(v1p), const(v2p)]
    est = 2 * sum(x.size * x.dtype.itemsize for x in weights) + 2 * 11 * tm * w * 4 + 40 * tm * w * 4
    outs = pl.pallas_call(
        functools.partial(_rwkv_pre_kernel, has_vres=has_vres),
        name="rwkv_pre",
        grid=(bsz, lp // tm),
        in_specs=in_specs,
        out_specs=[tile] * 8,
        out_shape=[jax.ShapeDtypeStruct((bsz, lp, w), F32)] * 8,
        compiler_params=_params(("parallel", "parallel"), est),
    )(*args)
    return outs


def _wkv_kernel(r_ref, lw_ref, k_ref, v_ref, a_ref, b_ref, gate_ref, bonus_ref, lnw_ref, lnb_ref,
                o_ref, ht_ref):
    c = RWKV_CHUNK
    wd = RWKV_QW

    @pl.when(pl.program_id(1) == 0)
    def _():
        ht_ref[...] = jnp.zeros_like(ht_ref)

    ti = lax.broadcasted_iota(jnp.int32, (c, c), 0)
    tj = lax.broadcasted_iota(jnp.int32, (c, c), 1)
    cum_all = jnp.dot((ti >= tj).astype(F32), lw_ref[...], preferred_element_type=F32,
                      precision=lax.Precision.HIGHEST)

    ri = lax.broadcasted_iota(jnp.int32, (wd, wd), 0)
    ci = lax.broadcasted_iota(jnp.int32, (wd, wd), 1)
    same = (ri // c) == (ci // RWKV_HEAD)
    strict = same & ((ci % c) < (ri % c))
    incl = same & ((ci % c) <= (ri % c))
    ones_blk = jnp.where(same, 1.0, 0.0).astype(BF16)

    def expand(x):
        xb = x.astype(BF16)
        return jnp.where(same, jnp.concatenate([xb] * RWKV_QUAD, axis=0), jnp.zeros((), BF16))

    def nt(x, y):
        return lax.dot_general(x, y, (((1,), (1,)), ((), ())), preferred_element_type=F32)

    def tn(x, y):
        return lax.dot_general(x, y, (((0,), (0,)), ((), ())), preferred_element_type=F32)

    def mm(x, y):
        return jnp.dot(x, y, preferred_element_type=F32)

    groups = range(r_ref.shape[1] // wd)
    sls = [slice(q * wd, (q + 1) * wd) for q in groups]
    ar, bt, kt, vb, bw, kw, w_last = [], [], [], [], [], [], []
    for sl in sls:
        lw = lw_ref[:, sl]
        cum = cum_all[:, sl]
        cum_last = cum[c - 1:c, :]
        e_neg = jnp.exp(-cum)
        e_rem = jnp.exp(cum_last - cum)
        k = k_ref[:, sl]
        b = b_ref[:, sl]
        ar.append(jnp.concatenate([expand(a_ref[:, sl] * jnp.exp(cum - lw)),
                                   expand(r_ref[:, sl] * jnp.exp(cum))], axis=0))
        bt.append(expand(b * e_neg))
        kt.append(expand(k * e_neg))
        vb.append(expand(v_ref[:, sl]))
        bw.append(expand(b * e_rem))
        kw.append(expand(k * e_rem))
        w_last.append(jnp.exp(cum_last))

    g_b = [nt(ar[q], bt[q]) for q in groups]
    g_k = [nt(ar[q], kt[q]) for q in groups]
    hts = [ht_ref[q] for q in groups]
    g_h = [nt(ar[q], hts[q].astype(BF16)) for q in groups]
    p = [jnp.where(strict, g_b[q][:wd], 0.0) for q in groups]
    a_ak = [jnp.where(strict, g_k[q][:wd], 0.0).astype(BF16) for q in groups]
    a_rb = [jnp.where(incl, g_b[q][wd:], 0.0).astype(BF16) for q in groups]
    a_rk = [jnp.where(incl, g_k[q][wd:], 0.0).astype(BF16) for q in groups]
    u = [g_h[q][:wd] + mm(a_ak[q], vb[q]) for q in groups]
    steps = int(math.log2(c))
    for i in range(steps):
        pb = [x.astype(BF16) for x in p]
        u = [u[q] + mm(pb[q], u[q].astype(BF16)) for q in groups]
        if i < steps - 1:
            p = [mm(pb[q], pb[q]) for q in groups]
    ub = [x.astype(BF16) for x in u]
    y_bd = [g_h[q][wd:] + mm(a_rb[q], ub[q]) + mm(a_rk[q], vb[q]) for q in groups]
    for q in groups:
        ht_ref[q] = hts[q] * w_last[q] + tn(ub[q], bw[q]) + tn(vb[q], kw[q])
    for q in groups:
        sl = sls[q]
        y = y_bd[q][0:c]
        for h in range(1, RWKV_QUAD):
            y = y + y_bd[q][h * c:(h + 1) * c]
        mean = _split_dot(y, ones_blk) * (1.0 / RWKV_HEAD)
        yc = y - mean
        var = _split_dot(yc * yc, ones_blk) * (1.0 / RWKV_HEAD)
        out = yc * lax.rsqrt(var + RWKV_GN_EPS) * lnw_ref[:, sl] + lnb_ref[:, sl] + bonus_ref[:, sl]
        o_ref[:, sl] = (out * gate_ref[:, sl]).astype(o_ref.dtype)


def _wkv7(r, lw, k, v, a, b, gate, bonus, ln_w, ln_b):
    bsz, lp, w = r.shape
    c = RWKV_CHUNK
    spec = pl.BlockSpec((None, c, w), lambda bi, n: (bi, n, 0))
    vspec = pl.BlockSpec((1, w), lambda bi, n: (0, 0))
    est = 2 * 9 * c * w * 4 + 4 * 40 * RWKV_QW * RWKV_QW * 4
    return pl.pallas_call(
        _wkv_kernel,
        name="wkv7",
        grid=(bsz, lp // c),
        in_specs=[spec] * 8 + [vspec, vspec],
        out_specs=spec,
        out_shape=jax.ShapeDtypeStruct((bsz, lp, w), BF16),
        scratch_shapes=[pltpu.VMEM((w // RWKV_QW, RWKV_QW, RWKV_QW), F32)],
        compiler_params=_params(("parallel", "arbitrary"), est),
    )(r, lw, k, v, a, b, gate, bonus, ln_w.reshape(1, w), ln_b.reshape(1, w))


def _fox_kernel(q_ref, k_ref, v_ref, ck_ref, o_ref, acc_ref, st_a, st_b, pt_a, pt_b, qt_ref):
    tk, sw = FOX_TK, FOX_SUB
    tq = q_ref.shape[0]
    nsub = tq // sw
    qi = pl.program_id(2)
    qt_ref[...] = q_ref[...].astype(F32).T.astype(BF16)
    head_lane = lax.broadcasted_iota(jnp.int32, (tk, FOX_HEADS), 1) == pl.program_id(1)
    tri = (lax.broadcasted_iota(jnp.int32, (tk, sw), 0) <= lax.broadcasted_iota(jnp.int32, (tk, sw), 1))
    ones = tuple(jnp.ones((1, sw), F32) for _ in range(nsub))

    def issue_scores(kb, st_out, first_sub=0):
        ks = pl.multiple_of(kb * tk, tk)
        st_out[:, first_sub * sw:] = jnp.dot(k_ref[pl.ds(ks, tk), :], qt_ref[:, first_sub * sw:],
                                             preferred_element_type=F32)

    def issue_pv(kb, pt_in, alphas, first_sub=0):
        ks = pl.multiple_of(kb * tk, tk)
        pv = lax.dot_general(v_ref[pl.ds(ks, tk), :], pt_in[:, first_sub * sw:], (((0,), (0,)), ((), ())),
                             preferred_element_type=F32)
        for r in range(first_sub, nsub):
            acc_ref[r] = alphas[r] * acc_ref[r] + pv[:, (r - first_sub) * sw:(r - first_sub + 1) * sw]

    def softmax(kb, st_in, pt_out, ml, first_sub=0, tri_sub=None):
        ck_heads = ck_ref[pl.ds(pl.multiple_of(kb * tk, tk), tk), :]
        ck = jnp.sum(jnp.where(head_lane, ck_heads, 0.0), axis=1, keepdims=True)
        ck = jnp.broadcast_to(ck, (tk, sw))
        ml, alphas = list(ml), list(ones)
        for r in range(first_sub, nsub):
            s = st_in[:, r * sw:(r + 1) * sw] - ck
            if r == tri_sub:
                s = jnp.where(tri, s, NEG_INF)
            m, l = ml[r]
            m_new = jnp.maximum(m, jnp.max(s, axis=0, keepdims=True))
            alphas[r] = jnp.exp2(m - m_new)
            p = jnp.exp2(s - m_new)
            ml[r] = (m_new, alphas[r] * l + jnp.sum(p, axis=0, keepdims=True))
            pt_out[:, r * sw:(r + 1) * sw] = p.astype(BF16)
        return tuple(ml), tuple(alphas)

    bufs = ((st_a, pt_a), (st_b, pt_b))

    def step(kb, parity, alphas_prev, ml, first_sub=0, tri_sub=None, prev_first=0, next_first=0):
        (st_cur, pt_cur), (st_nxt, pt_prv) = bufs[parity], bufs[1 - parity]
        if next_first is not None:
            issue_scores(kb + 1, st_nxt, next_first)
        issue_pv(jnp.maximum(kb - 1, 0), pt_prv, alphas_prev, prev_first)
        return softmax(kb, st_cur, pt_cur, ml, first_sub, tri_sub)

    def body(i, carry):
        alphas, ml = carry
        ml, alphas = step(2 * i, 0, alphas, ml)
        ml, alphas = step(2 * i + 1, 1, alphas, ml)
        return alphas, ml

    acc_ref[...] = jnp.zeros_like(acc_ref)
    pt_b[...] = jnp.zeros_like(pt_b)
    issue_scores(0, st_a)
    n_main = qi * (tq // tk)
    init_ml = tuple((jnp.full((1, sw), NEG_INF, F32), jnp.zeros((1, sw), F32)) for _ in range(nsub))
    alphas, ml = lax.fori_loop(0, n_main // 2, body, (ones, init_ml))
    ndiag = tq // tk
    for d in range(ndiag):
        ml, alphas = step(n_main + d, d % 2, alphas, ml, first_sub=d, tri_sub=d, prev_first=max(d - 1, 0),
                          next_first=d + 1 if d + 1 < ndiag else None)
    issue_pv(n_main + ndiag - 1, bufs[(ndiag - 1) % 2][1], alphas, ndiag - 1)
    for r in range(nsub):
        o_ref[r * sw:(r + 1) * sw, :] = (acc_ref[r] / ml[r][1]).T.astype(o_ref.dtype)


def _fox(qkv, c, lp_real):
    bsz, lp = qkv.shape[:2]
    h, dh, w = FOX_HEADS, FOX_DH, BRANCH_WIDTH
    tq = FOX_TQ
    pos = jnp.arange(lp)
    ck = jnp.where(((pos >= N_PAD) & (pos < lp_real))[None, :, None], c, -NEG_INF) * LOG2E
    est = 2 * (2 * tq * dh * 2 + 2 * lp * dh * 2 + lp * LANE * 4) + (16 << 20)
    return pl.pallas_call(
        _fox_kernel,
        name="fox",
        grid=(bsz, h, lp // tq),
        in_specs=[pl.BlockSpec((None, tq, dh), lambda b, hh, i: (b, i, hh)),
                  pl.BlockSpec((None, lp, dh), lambda b, hh, i: (b, 0, h + hh)),
                  pl.BlockSpec((None, lp, dh), lambda b, hh, i: (b, 0, 2 * h + hh)),
                  pl.BlockSpec((None, lp, h), lambda b, hh, i: (b, 0, 0))],
        out_specs=pl.BlockSpec((None, tq, dh), lambda b, hh, i: (b, i, hh)),
        out_shape=jax.ShapeDtypeStruct((bsz, lp, w), BF16),
        scratch_shapes=[pltpu.VMEM((tq // FOX_SUB, dh, FOX_SUB), F32),
                        pltpu.VMEM((FOX_TK, tq), F32), pltpu.VMEM((FOX_TK, tq), F32),
                        pltpu.VMEM((FOX_TK, tq), BF16), pltpu.VMEM((FOX_TK, tq), BF16),
                        pltpu.VMEM((dh, tq), BF16)],
        compiler_params=_params(("parallel", "parallel", "parallel"), est),
    )(qkv, qkv, qkv, ck)


S5_CHUNK = 8
S5_SLAB = LANE // S5_GROUP
S5_NSLAB = S5_GROUPS // S5_SLAB


def _s5_tables(a_re, a_im, b_re, b_im, c_re, c_im, log_step):
    hp = lax.Precision.HIGHEST
    g, p, cg, n, sg, ns = S5_GROUPS, S5_STATE, S5_GROUP, S5_CHUNK, S5_SLAB, S5_NSLAB
    dt = jnp.exp(log_step)[:, None]
    mag = jnp.exp(a_re * dt)
    bar_re = mag * jnp.cos(a_im * dt)
    bar_im = mag * jnp.sin(a_im * dt)
    den = a_re * a_re + a_im * a_im
    num_re = bar_re - 1.0
    coef_re = (num_re * a_re + bar_im * a_im) / den
    coef_im = (bar_im * a_re - num_re * a_im) / den
    bb_re = coef_re[..., None] * b_re - coef_im[..., None] * b_im
    bb_im = coef_re[..., None] * b_im + coef_im[..., None] * b_re
    pw_re, pw_im = [jnp.ones_like(bar_re)], [jnp.zeros_like(bar_im)]
    for _ in range(n):
        pw_re, pw_im = (pw_re + [pw_re[-1] * bar_re - pw_im[-1] * bar_im],
                        pw_im + [pw_re[-1] * bar_im + pw_im[-1] * bar_re])
    eye = jnp.eye(sg, dtype=F32)

    w_re = jnp.stack([pw_re[n - 1 - s][..., None] * bb_re - pw_im[n - 1 - s][..., None] * bb_im for s in range(n)])
    w_im = jnp.stack([pw_re[n - 1 - s][..., None] * bb_im + pw_im[n - 1 - s][..., None] * bb_re for s in range(n)])
    gm = jnp.concatenate(
        [jnp.einsum('sjgpc,gh->jsgchp', x.reshape(n, ns, sg, p, cg), eye).reshape(ns, n, sg * cg, sg * p)
         for x in (w_re, w_im)], axis=-1)

    ca_re = jnp.stack([c_re * pw_re[tau][:, None, :] - c_im * pw_im[tau][:, None, :] for tau in range(n + 1)])
    ca_im = jnp.stack([c_re * pw_im[tau][:, None, :] + c_im * pw_re[tau][:, None, :] for tau in range(n + 1)])
    k = (jnp.einsum('tgcp,gpd->tgcd', ca_re[:n], bb_re, precision=hp)
         - jnp.einsum('tgcp,gpd->tgcd', ca_im[:n], bb_im, precision=hp))
    lag = jnp.arange(n)[None, :] - jnp.arange(n)[:, None]
    kst = jnp.where((lag >= 0)[:, :, None, None, None], k[jnp.clip(lag, 0, n - 1)], 0.0)
    mm = jnp.einsum('stjgcd,gh->jsgdthc', kst.reshape(n, n, ns, sg, cg, cg), eye).reshape(ns, n, sg * cg, n * LANE)

    pm = jnp.concatenate(
        [jnp.einsum('tjgcp,gh->jgpthc', x.reshape(n, ns, sg, cg, p), eye).reshape(ns, sg * p, n * LANE)
         for x in (ca_re[1:], -ca_im[1:])], axis=1)
    a8 = jnp.concatenate([pw_re[n].reshape(ns, sg * p), pw_im[n].reshape(ns, sg * p)], axis=1)
    return gm.astype(BF16), mm.astype(BF16), pm.astype(BF16), a8


def _chunk_rows(ref, s, nb):
    return ref[pl.ds(s, nb, stride=S5_CHUNK), :]


def _s5_in_kernel(u_ref, gm_ref, z_ref):
    nb = z_ref.shape[0]
    acc = None
    for s in range(S5_CHUNK):
        d = jnp.dot(_chunk_rows(u_ref, s, nb).astype(BF16), gm_ref[s], preferred_element_type=F32)
        acc = d if acc is None else acc + d
    z_ref[...] = acc


def _s5_state_kernel(z_ref, a_ref, h_ref, st_ref):
    @pl.when(pl.program_id(1) == 0)
    def _():
        st_ref[...] = jnp.zeros_like(st_ref)

    ns, nb, sw2 = z_ref.shape
    sw = sw2 // 2
    coef = [(a_ref[j:j + 1, :sw], a_ref[j:j + 1, sw:]) for j in range(ns)]

    def body(n, carry):
        out = []
        for j in range(ns):
            hr, hi = carry[j]
            h_ref[j, pl.ds(n, 1), :] = jnp.concatenate([hr, hi], axis=1)
            z = z_ref[j, pl.ds(n, 1), :]
            ar, ai = coef[j]
            out.append((ar * hr - ai * hi + z[:, :sw], ar * hi + ai * hr + z[:, sw:]))
        return tuple(out)

    init = tuple((st_ref[j:j + 1, :sw], st_ref[j:j + 1, sw:]) for j in range(ns))
    fin = lax.fori_loop(0, nb, body, init)
    for j in range(ns):
        st_ref[j:j + 1, :sw] = fin[j][0]
        st_ref[j:j + 1, sw:] = fin[j][1]


def _s5_out_kernel(u_ref, h_ref, mm_ref, pm_ref, d_ref, o_ref, y_ref):
    nb = h_ref.shape[0]
    acc = jnp.dot(h_ref[...].astype(BF16), pm_ref[...], preferred_element_type=F32)
    for s in range(S5_CHUNK):
        acc = acc + jnp.dot(_chunk_rows(u_ref, s, nb).astype(BF16), mm_ref[s], preferred_element_type=F32)
    for t in range(S5_CHUNK):
        y_ref[pl.ds(t, nb, stride=S5_CHUNK), :] = acc[:, t * LANE:(t + 1) * LANE]
    o_ref[...] = jax.nn.gelu(y_ref[...] + d_ref[...] * u_ref[...]).astype(o_ref.dtype)


def _s5_branch(proj, u_blk, a_re, a_im, b_re, b_im, c_re, c_im, d_skip, log_step, w_glu):
    bsz, lp = proj.shape[:2]
    w, n, ns = BRANCH_WIDTH, S5_CHUNK, S5_NSLAB
    sw2 = 2 * S5_SLAB * S5_STATE
    tb = _pick(lp, ROW_TILES)
    nb = tb // n
    nchunk = lp // n
    gm, mm, pm, a8 = _s5_tables(a_re, a_im, b_re, b_im, c_re, c_im, log_step)
    u_spec = pl.BlockSpec((None, tb, LANE), lambda b, j, i: (b, i, u_blk * ns + j))
    st_spec = pl.BlockSpec((None, None, nb, sw2), lambda b, j, i: (b, j, i, 0))
    z = pl.pallas_call(
        _s5_in_kernel,
        name="s5_in",
        grid=(bsz, ns, lp // tb),
        in_specs=[u_spec, pl.BlockSpec((None, n, LANE, sw2), lambda b, j, i: (j, 0, 0, 0))],
        out_specs=st_spec,
        out_shape=jax.ShapeDtypeStruct((bsz, ns, nchunk, sw2), F32),
        compiler_params=_params(("parallel", "parallel", "parallel"),
                                2 * (tb * LANE * 4 + n * LANE * sw2 * 2 + nb * sw2 * 4) + 2 * nb * sw2 * 4),
    )(proj, gm)
    nbb = _pick(nchunk, (96, 64, 32, 16, 8))
    blk = pl.BlockSpec((None, ns, nbb, sw2), lambda b, i: (b, 0, i, 0))
    h_start = pl.pallas_call(
        _s5_state_kernel,
        name="s5_state",
        grid=(bsz, nchunk // nbb),
        in_specs=[blk, pl.BlockSpec((ns, sw2), lambda b, i: (0, 0))],
        out_specs=blk,
        out_shape=jax.ShapeDtypeStruct((bsz, ns, nchunk, sw2), F32),
        scratch_shapes=[pltpu.VMEM((ns, sw2), F32)],
        compiler_params=_params(("parallel", "arbitrary"), 4 * ns * nbb * sw2 * 4),
    )(z, a8)
    y = pl.pallas_call(
        _s5_out_kernel,
        name="s5_out",
        grid=(bsz, ns, lp // tb),
        in_specs=[u_spec, st_spec,
                  pl.BlockSpec((None, n, LANE, n * LANE), lambda b, j, i: (j, 0, 0, 0)),
                  pl.BlockSpec((None, sw2, n * LANE), lambda b, j, i: (j, 0, 0)),
                  pl.BlockSpec((1, LANE), lambda b, j, i: (0, j))],
        out_specs=pl.BlockSpec((None, tb, LANE), lambda b, j, i: (b, i, j)),
        out_shape=jax.ShapeDtypeStruct((bsz, lp, w), BF16),
        scratch_shapes=[pltpu.VMEM((tb, LANE), F32)],
        compiler_params=_params(("parallel", "parallel", "parallel"),
                                2 * (tb * LANE * 4 + nb * sw2 * 4 + n * LANE * n * LANE * 2 + sw2 * n * LANE * 2
                                     + tb * LANE * 2) + 3 * nb * n * LANE * 4),
    )(proj, h_start, mm, pm, d_skip.reshape(1, w))
    return _glu(y.reshape(bsz * lp, w), w_glu.astype(BF16))


def _pad_cols(w, n):
    return jnp.pad(w, ((0, 0), (0, n - w.shape[1])))


def _scale_cols(x, s):
    return x * s


MOE_TM = 512


def _moe_up_kernel(te_ref, x_ref, wg_ref, wu_ref, s_ref, o_ref, *, ni):
    i = pl.program_id(1)

    @pl.when(i < te_ref[ni])
    def _():
        x = x_ref[...]
        g = jnp.dot(x, wg_ref[...], preferred_element_type=F32)
        u = jnp.dot(x, wu_ref[...], preferred_element_type=F32)
        o_ref[...] = (g * jax.nn.sigmoid(g) * u * s_ref[...]).astype(o_ref.dtype)

    @pl.when(i >= te_ref[ni])
    def _():
        o_ref[...] = jnp.zeros_like(o_ref)


def _moe_down_kernel(te_ref, a_ref, w_ref, o_ref, *, ni):
    i = pl.program_id(1)

    @pl.when(i < te_ref[ni])
    def _():
        o_ref[...] = jnp.dot(a_ref[...], w_ref[...], preferred_element_type=F32).astype(o_ref.dtype)

    @pl.when(i >= te_ref[ni])
    def _():
        o_ref[...] = jnp.zeros_like(o_ref)


def _moe_experts(xs, row_scale, tile_info, w_gate, w_up, w_down):
    r, d = xs.shape
    e, _, f = w_gate.shape
    tm = MOE_TM
    ni = r // tm
    tn = _pick(f, (512, 256, 128))
    wmap = lambda j, i, te: (te[i], 0, j)
    act = pl.pallas_call(
        functools.partial(_moe_up_kernel, ni=ni),
        name="moe_up",
        grid_spec=pltpu.PrefetchScalarGridSpec(
            num_scalar_prefetch=1, grid=(f // tn, ni),
            in_specs=[pl.BlockSpec((tm, d), lambda j, i, te: (i, 0)),
                      pl.BlockSpec((None, d, tn), wmap), pl.BlockSpec((None, d, tn), wmap),
                      pl.BlockSpec((tm, 1), lambda j, i, te: (i, 0))],
            out_specs=pl.BlockSpec((tm, tn), lambda j, i, te: (i, j))),
        out_shape=jax.ShapeDtypeStruct((r, f), BF16),
        compiler_params=_params(("arbitrary", "arbitrary"),
                                2 * (tm * d * 2 + 2 * d * tn * 2 + tm * tn * 2) + 4 * tm * tn * 4),
    )(tile_info, xs, w_gate, w_up, row_scale)
    tn2 = _pick(d, (1024, 512, 256, 128))
    return pl.pallas_call(
        functools.partial(_moe_down_kernel, ni=ni),
        name="moe_down",
        grid_spec=pltpu.PrefetchScalarGridSpec(
            num_scalar_prefetch=1, grid=(d // tn2, ni),
            in_specs=[pl.BlockSpec((tm, f), lambda j, i, te: (i, 0)),
                      pl.BlockSpec((None, f, tn2), wmap)],
            out_specs=pl.BlockSpec((tm, tn2), lambda j, i, te: (i, j))),
        out_shape=jax.ShapeDtypeStruct((r, d), F32),
        compiler_params=_params(("arbitrary", "arbitrary"),
                                2 * (tm * f * 2 + f * tn2 * 2 + tm * tn2 * 4) + 2 * tm * tn2 * 4),
    )(tile_info, act, w_down)


def _moe_ffn(h32, hb, w_router, b_router, w_gate, w_up, w_down):
    t, d = hb.shape
    e = w_gate.shape[0]
    tm = MOE_TM
    logits = _mm(h32, _pad_cols(w_router, LANE), precision=lax.Precision.HIGHEST,
                 name="router")[:, :N_EXPERTS] + b_router
    top_vals, top_idx = lax.top_k(logits, TOP_K)
    weights = jax.nn.softmax(top_vals, axis=-1)

    n_assign = t * TOP_K
    n_rows = n_assign + e * tm
    flat_e = top_idx.reshape(n_assign).astype(jnp.int32)
    onehot = jax.nn.one_hot(flat_e, e, dtype=jnp.int32)
    seen = jnp.cumsum(onehot, axis=0)
    counts = seen[-1]
    padded = (counts + tm - 1) // tm * tm
    group_end = jnp.cumsum(padded)
    group_start = group_end - padded
    sorted_start = jnp.cumsum(counts) - counts
    dest = group_start[flat_e] + jnp.sum(seen * onehot, axis=1) - 1
    order = jnp.argsort(flat_e, stable=True).astype(jnp.int32)
    rows = jnp.arange(n_rows, dtype=jnp.int32)
    row_e = jnp.minimum(jnp.searchsorted(group_end, rows, side="right"), e - 1).astype(jnp.int32)
    rank = rows - group_start[row_e]
    used = rank < counts[row_e]
    assign = order[jnp.where(used, sorted_start[row_e] + rank, 0)]
    src_tok = jnp.where(used, assign // TOP_K, 0)
    row_scale = jnp.where(used, weights.reshape(n_assign)[assign], 0.0)
    tile_info = jnp.concatenate([row_e[::tm], (group_end[-1:] // tm).astype(jnp.int32)])

    xs = jnp.take(hb, src_tok, axis=0)
    y = _moe_experts(xs, row_scale[:, None], tile_info, w_gate.astype(BF16), w_up.astype(BF16),
                     w_down.astype(BF16))
    dest = dest.reshape(t, TOP_K)
    return [jnp.take(y, dest[:, s], axis=0) for s in range(TOP_K)]


def kernel(x, meta_tokens, w_in, b_forget, w_gate_down, w_gate_up, b_gate, w_branch, w_out, ln_mix_g, ln_mix_b, ln_ffn_g, ln_ffn_b, rwkv_mu, rwkv_w_rkv, rwkv_w0, rwkv_w1, rwkv_w2, rwkv_a0, rwkv_a1, rwkv_a2, rwkv_v0, rwkv_v1, rwkv_v2, rwkv_g1, rwkv_g2, rwkv_k_k, rwkv_k_a, rwkv_r_k, rwkv_ln_w, rwkv_ln_b, s5_a_re, s5_a_im, s5_b_re, s5_b_im, s5_c_re, s5_c_im, s5_d, s5_log_step, s5_w_glu, ffn_w_gate, ffn_w_up, ffn_w_down, moe_router, moe_router_b, moe_w_gate, moe_w_up, moe_w_down):
    bsz, seq, d = x.shape
    depth = w_in.shape[0]
    w = BRANCH_WIDTH
    lp_real = N_PAD + N_META + seq
    lp = -(-lp_real // FOX_TQ) * FOX_TQ
    t = bsz * lp
    meta = jnp.broadcast_to(meta_tokens.astype(x.dtype)[None], (bsz, N_META, d))
    h = jnp.concatenate([jnp.zeros((bsz, N_PAD, d), x.dtype), meta, x,
                         jnp.zeros((bsz, lp - lp_real, d), x.dtype)], axis=1)
    hb = h.astype(BF16)
    pos = jnp.arange(lp)
    real = ((pos >= N_PAD) & (pos < lp_real))[None, :, None]
    f_lo = 7 * w
    q_scale = jnp.concatenate([jnp.full((1, w), FOX_DH ** -0.5 * LOG2E, F32), jnp.ones((1, 2 * w), F32)], axis=1)
    v_first = None
    for layer in range(depth):
        wl = w_in[layer]
        w_f32 = jnp.concatenate([wl[:, :4 * w], wl[:, f_lo + FOX_HEADS:]], axis=1).astype(BF16)
        w_fox = wl[:, 4 * w:f_lo].astype(BF16)
        w_small = _pad_cols(jnp.concatenate([w_gate_down[layer], wl[:, f_lo:f_lo + FOX_HEADS]], axis=1),
                            GATE_RANK + LANE).astype(BF16)
        hb2 = hb.reshape(t, d)
        proj = _mm(hb2, w_f32, name="proj").reshape(bsz, lp, 5 * w)
        qkv = _mm(hb2, w_fox, out_dtype=BF16, ep=_scale_cols, extras=((q_scale, "row", 0),),
                  name="proj_fox").reshape(bsz, lp, 3 * w)
        small = _mm(hb2, w_small, name="proj_small")
        f_logit = small[:, GATE_RANK:GATE_RANK + FOX_HEADS].reshape(bsz, lp, FOX_HEADS)

        o_ret = _retention(proj)

        vres = None if layer == 0 else (rwkv_v0[layer - 1], rwkv_v1[layer - 1], rwkv_v2[layer - 1])
        r, lw, k, v, a, b, gate, bonus = _rwkv_pre(
            proj, 3, v_first, vres, rwkv_mu[layer], rwkv_w_rkv[layer], rwkv_w0[layer], rwkv_w1[layer],
            rwkv_w2[layer], rwkv_a0[layer], rwkv_a1[layer], rwkv_a2[layer], rwkv_g1[layer], rwkv_g2[layer],
            rwkv_k_k[layer], rwkv_k_a[layer], rwkv_r_k[layer])
        if layer == 0:
            v_first = v
        o_rwkv = _wkv7(r, lw, k, v, a, b, gate, bonus, rwkv_ln_w[layer], rwkv_ln_b[layer])

        log_f = jnp.where(real, jax.nn.log_sigmoid(f_logit + b_forget[layer]), 0.0)
        c = jnp.cumsum(log_f, axis=1)
        o_fox = _fox(qkv, c, lp_real)

        o_s5 = _s5_branch(proj, 4, s5_a_re[layer], s5_a_im[layer], s5_b_re[layer], s5_b_im[layer],
                          s5_c_re[layer], s5_c_im[layer], s5_d[layer], s5_log_step[layer], s5_w_glu[layer])

        merged = _merge([o_ret.reshape(t, w), o_rwkv.reshape(t, w), o_fox.reshape(t, w), o_s5],
                        w_branch[layer].astype(BF16), small, w_gate_up[layer].astype(BF16), b_gate[layer])
        mix = _mm(merged, w_out[layer].astype(BF16), name="w_out")
        h, hb = _res_ln([mix], h, ln_mix_g[layer], ln_mix_b[layer], mask_pad=None)

        j = layer // 2
        hb2 = hb.reshape(t, d)
        if layer % 2 == 0:
            act = _swiglu_up(hb2, ffn_w_gate[j].astype(BF16), ffn_w_up[j].astype(BF16))
            ffn = [_mm(act, ffn_w_down[j].astype(BF16), name="ffn_down")]
        else:
            ffn = _moe_ffn(h.reshape(t, d), hb2, moe_router[j], moe_router_b[j],
                           moe_w_gate[j], moe_w_up[j], moe_w_down[j])
        h, hb = _res_ln(ffn, h, ln_ffn_g[layer], ln_ffn_b[layer], mask_pad=(N_PAD, lp_real))
    return h[:, N_PAD + N_META:lp_real]
```

```python
import functools
import math

import jax
import jax.numpy as jnp
from jax import lax
from jax.experimental import pallas as pl
from jax.experimental.pallas import tpu as pltpu

F32 = jnp.float32
BF16 = jnp.bfloat16

D_MODEL = 4096
DEPTH = 4
N_META = 16
BLOCK = 128
N_PAD = (-N_META) % BLOCK
BRANCH_WIDTH = D_MODEL // 4
GATE_RANK = 256
RET_DK = 64
RET_DV = 128
RET_HEADS = BRANCH_WIDTH // RET_DV
RWKV_HEAD = 64
RWKV_HEADS = BRANCH_WIDTH // RWKV_HEAD
RWKV_GN_EPS = 64e-5
FOX_DH = 128
FOX_HEADS = BRANCH_WIDTH // FOX_DH
S5_GROUP = 16
S5_GROUPS = BRANCH_WIDTH // S5_GROUP
S5_STATE = 64
N_EXPERTS = 8
TOP_K = 2
ALPHA = (2.0 * DEPTH) ** 0.25
LN_EPS = 1e-5
NEG_INF = -1e30
LOG2E = math.log2(math.e)

V7X_VMEM_BUDGET = 58 * 1024 * 1024
LANE = 128
RWKV_CHUNK = 64
RWKV_QUAD = 4
RWKV_QW = RWKV_QUAD * RWKV_HEAD
FOX_TK = 128
FOX_SUB = 128
FOX_TQ = 768
ROW_TILES = (1408, 1536, 1280, 768, 640, 512, 256, 128)


def _pick(dim, candidates):
    for c in candidates:
        if dim % c == 0:
            return c
    return dim


def _params(sem, est_bytes):
    limit = int(min(V7X_VMEM_BUDGET, max(est_bytes * 5 // 4 + (2 << 20), 16 << 20)))
    return pltpu.CompilerParams(dimension_semantics=sem, vmem_limit_bytes=limit)


def _nbytes(shape, dtype):
    n = jnp.dtype(dtype).itemsize
    for s in shape:
        if s is not None:
            n *= s
    return n


def _mmf_kernel(*refs, nb, nx, nk, ep, precision):
    a_ref = refs[0]
    b_refs = refs[1:1 + nb]
    x_refs = refs[1 + nb:1 + nb + nx]
    o_ref = refs[1 + nb + nx]
    acc_refs = refs[2 + nb + nx:]
    a = a_ref[...]
    if precision is None and a.dtype != BF16:
        a = a.astype(BF16)
    def weight(b_ref):
        b = b_ref[...]
        return b.astype(BF16) if precision is None and b.dtype != BF16 else b

    dots = [jnp.dot(a, weight(b), preferred_element_type=F32, precision=precision) for b in b_refs]

    def finish(vals):
        o_ref[...] = ep(*vals, *[x[...] for x in x_refs]).astype(o_ref.dtype)

    if nk == 1:
        finish(dots)
        return
    k = pl.program_id(2)

    @pl.when(k == 0)
    def _():
        for acc, d in zip(acc_refs, dots):
            acc[...] = d

    @pl.when(k > 0)
    def _():
        for acc, d in zip(acc_refs, dots):
            acc[...] += d

    @pl.when(k == nk - 1)
    def _():
        finish([acc[...] for acc in acc_refs])


def _mmf(a_spec, b_specs, x_specs, ep, *, m, n, kd, tm, tn, tk, out_dtype, precision=None, name="mm"):
    nk = kd // tk
    specs = [a_spec] + list(b_specs) + list(x_specs)
    est = 2 * sum(_nbytes(s[1], s[0].dtype) for s in specs) + 2 * tm * tn * jnp.dtype(out_dtype).itemsize
    est += (len(b_specs) + 2) * tm * tn * 4
    scratch = [pltpu.VMEM((tm, tn), F32) for _ in b_specs] if nk > 1 else []
    return pl.pallas_call(
        functools.partial(_mmf_kernel, nb=len(b_specs), nx=len(x_specs), nk=nk, ep=ep, precision=precision),
        name=name,
        grid=(m // tm, n // tn, nk),
        in_specs=[pl.BlockSpec(s[1], s[2]) for s in specs],
        out_specs=pl.BlockSpec((tm, tn), lambda i, j, k: (i, j)),
        out_shape=jax.ShapeDtypeStruct((m, n), out_dtype),
        scratch_shapes=scratch,
        compiler_params=_params(("parallel", "parallel", "arbitrary"), est),
    )(*[s[0] for s in specs])


def _ident(x):
    return x


def _mm(a, b, out_dtype=F32, precision=None, ep=_ident, extras=(), a_blk=0, name="mm"):
    m = a.shape[0]
    kd, n = b.shape
    tm = _pick(m, ROW_TILES)
    tn = _pick(n, (1024, 512, 384, 256, 128))
    tk = kd if kd <= 4096 else _pick(kd, (2048, 1536, 1024, 512))
    if precision is not None:
        tm = _pick(m, (640, 512, 256, 128))
        tn = _pick(n, (512, 384, 256, 128))
        tk = kd if kd <= 2048 else _pick(kd, (2048, 1024, 512))
    nk = kd // tk
    ob = jnp.dtype(out_dtype).itemsize

    def need(tn_):
        return (2 * (tm * tk * a.dtype.itemsize + tk * tn_ * b.dtype.itemsize + tm * tn_ * ob)
                + (3 if nk > 1 else 2) * tm * tn_ * 4)

    while need(tn) > V7X_VMEM_BUDGET - (6 << 20) and tn % 256 == 0 and n % (tn // 2) == 0:
        tn //= 2
    a_spec = (a, (tm, tk), lambda i, j, k: (i, a_blk * nk + k))
    b_spec = (b, (tk, tn), lambda i, j, k: (k, j))
    x_specs = []
    for arr, kind, off in extras:
        if kind == "tile":
            x_specs.append((arr, (tm, tn), lambda i, j, k, off=off: (i, j + off)))
        else:
            x_specs.append((arr, (1, tn), lambda i, j, k, off=off: (0, j + off)))
    return _mmf(a_spec, [b_spec], x_specs, ep, m=m, n=n, kd=kd, tm=tm, tn=tn, tk=tk,
                out_dtype=out_dtype, precision=precision, name=name)


def _silu_mul(g, u):
    return g * jax.nn.sigmoid(g) * u


def _swiglu_up(h, wg, wu):
    m, d = h.shape
    f = wg.shape[1]
    tm = _pick(m, ROW_TILES)
    tn = _pick(f, (512, 256, 128))
    wmap = lambda i, j, k: (0, j)
    return _mmf((h, (tm, d), lambda i, j, k: (i, 0)),
                [(wg, (d, tn), wmap), (wu, (d, tn), wmap)], [], _silu_mul,
                m=m, n=f, kd=d, tm=tm, tn=tn, tk=d, out_dtype=BF16, name="swiglu_up")


def _glu_ep(val, gt):
    return val * jax.nn.sigmoid(gt)


def _glu(y, w_glu):
    m, kd = y.shape
    n = w_glu.shape[1] // 2
    tm = _pick(m, ROW_TILES)
    tn = _pick(n, (512, 256, 128))
    nj = n // tn
    return _mmf((y, (tm, kd), lambda i, j, k: (i, 0)),
                [(w_glu, (kd, tn), lambda i, j, k: (0, j)), (w_glu, (kd, tn), lambda i, j, k: (0, nj + j))],
                [], _glu_ep, m=m, n=n, kd=kd, tm=tm, tn=tn, tk=kd, out_dtype=BF16, name="s5_glu")


def _res_ln_kernel(*refs, ny, tm, mask_pad):
    y_refs = refs[:ny]
    r_ref, g_ref, b_ref, o_ref, ob_ref = refs[ny:]
    z = ALPHA * r_ref[...] + y_refs[0][...]
    for y_ref in y_refs[1:]:
        z = z + y_ref[...]
    mu = jnp.mean(z, axis=-1, keepdims=True)
    zc = z - mu
    var = jnp.mean(zc * zc, axis=-1, keepdims=True)
    out = zc * lax.rsqrt(var + LN_EPS) * g_ref[...] + b_ref[...]
    if mask_pad is not None:
        pos = pl.program_id(1) * tm + lax.broadcasted_iota(jnp.int32, (tm, 1), 0)
        out = jnp.where((pos >= mask_pad[0]) & (pos < mask_pad[1]), out, 0.0)
    o_ref[...] = out
    ob_ref[...] = out.astype(BF16)


def _res_ln(ys, res, g, b, mask_pad):
    bsz, lp, d = res.shape
    tm = _pick(lp, (256, 128, 64, 32, 16, 8))
    est = 2 * ((4 + len(ys)) * tm * d * 4) + 4 * tm * d * 4
    spec = pl.BlockSpec((None, tm, d), lambda bi, i: (bi, i, 0))
    vspec = pl.BlockSpec((1, d), lambda bi, i: (0, 0))
    return pl.pallas_call(
        functools.partial(_res_ln_kernel, ny=len(ys), tm=tm, mask_pad=mask_pad),
        name="res_ln",
        grid=(bsz, lp // tm),
        in_specs=[spec] * (len(ys) + 1) + [vspec, vspec],
        out_specs=[spec, spec],
        out_shape=[jax.ShapeDtypeStruct((bsz, lp, d), F32), jax.ShapeDtypeStruct((bsz, lp, d), BF16)],
        compiler_params=_params(("parallel", "parallel"), est),
    )(*[y.reshape(bsz, lp, d) for y in ys], res, g.reshape(1, d), b.reshape(1, d))


def _merge_kernel(o0, o1, o2, o3, wb_ref, glr_ref, wgu_ref, bg_ref, out_ref):
    glr = glr_ref[...].astype(BF16)
    acc = None
    for i, o_ref in enumerate((o0, o1, o2, o3)):
        gate = jax.nn.sigmoid(jnp.dot(glr, wgu_ref[i], preferred_element_type=F32) + bg_ref[i])
        term = gate * jnp.dot(o_ref[...], wb_ref[i], preferred_element_type=F32)
        acc = term if acc is None else acc + term
    out_ref[...] = acc.astype(out_ref.dtype)


def _merge(outs, w_branch, small, w_gate_up, b_gate):
    m, w = outs[0].shape
    nb, r, d = w_gate_up.shape
    tm = _pick(m, ROW_TILES)
    tn = _pick(d, (512, 256, 128))
    est = 2 * (nb * tm * w * 2 + nb * w * tn * 2 + tm * r * 4 + nb * r * tn * 2 + tm * tn * 2) + 4 * tm * tn * 4
    ospec = pl.BlockSpec((tm, w), lambda i, j: (i, 0))
    return pl.pallas_call(
        _merge_kernel,
        name="merge",
        grid=(m // tm, d // tn),
        in_specs=[ospec, ospec, ospec, ospec,
                  pl.BlockSpec((nb, w, tn), lambda i, j: (0, 0, j)),
                  pl.BlockSpec((tm, r), lambda i, j: (i, 0)),
                  pl.BlockSpec((nb, r, tn), lambda i, j: (0, 0, j)),
                  pl.BlockSpec((nb, 1, tn), lambda i, j: (0, 0, j))],
        out_specs=pl.BlockSpec((tm, tn), lambda i, j: (i, j)),
        out_shape=jax.ShapeDtypeStruct((m, d), BF16),
        compiler_params=_params(("parallel", "parallel"), est),
    )(*outs, w_branch, small, w_gate_up, b_gate.reshape(nb, 1, d))


def _ret_consts(lp):
    f32 = F32
    half = RET_DK // 2
    inv = 1.0 / (10000.0 ** jnp.linspace(0.0, 1.0, half, dtype=f32))
    ang = jnp.arange(lp, dtype=f32)[:, None] * inv[None, :]
    cos, sin = jnp.cos(ang), jnp.sin(ang)
    cos_t = jnp.tile(jnp.concatenate([cos, cos], axis=-1), (1, RET_HEADS))
    sin_t = jnp.tile(jnp.concatenate([-sin, sin], axis=-1), (1, RET_HEADS))
    log_gamma = jnp.log(1.0 - 2.0 ** (-5.0 - jnp.arange(RET_HEADS, dtype=f32)))
    pos = jnp.arange(BLOCK, dtype=f32)
    rel = pos[:, None] - pos[None, :]
    dmat = jnp.where(rel >= 0, jnp.exp(log_gamma[:, None, None] * jnp.maximum(rel, 0.0)), 0.0)
    k_decay = jnp.exp(log_gamma[:, None] * (BLOCK - 1.0 - pos)[None, :])
    q_decay = jnp.exp(log_gamma[:, None] * (pos + 1.0)[None, :])
    kd_t = jnp.repeat(k_decay.T, RET_DK, axis=1)
    qd_t = jnp.repeat(q_decay.T, RET_DK, axis=1)
    cd = jnp.exp(log_gamma * BLOCK)
    cd_t = jnp.broadcast_to(cd[:, None, None], (RET_HEADS, 1, RET_DV))
    return cos_t, sin_t, dmat, qd_t, kd_t, cd_t


def _ret_kernel(q_ref, k_ref, v_ref, g_ref, cos_ref, sin_ref, dmat_ref, qd_ref, kd_ref, cd_ref,
                o_ref, st_ref):
    @pl.when(pl.program_id(1) == 0)
    def _():
        st_ref[...] = jnp.zeros_like(st_ref)

    cos = cos_ref[...]
    sin = sin_ref[...]
    width = RET_HEADS * RET_DK
    lane = lax.broadcasted_iota(jnp.int32, (BLOCK, width), 1)
    first = (lane % RET_DK) < (RET_DK // 2)

    def rot(x):
        sw = jnp.where(first, pltpu.roll(x, width - RET_DK // 2, 1), pltpu.roll(x, RET_DK // 2, 1))
        return x * cos + sw * sin

    q = rot(q_ref[...])
    k = rot(k_ref[...]) * RET_DK ** -0.5
    qb = q.astype(BF16)
    kb = k.astype(BF16)
    qs = (q * qd_ref[...]).astype(BF16)
    ks = (k * kd_ref[...]).astype(BF16)
    for h in range(RET_HEADS):
        ksl = slice(h * RET_DK, (h + 1) * RET_DK)
        vsl = slice(h * RET_DV, (h + 1) * RET_DV)
        vh = v_ref[:, vsl].astype(BF16)
        s = lax.dot_general(qb[:, ksl], kb[:, ksl], (((1,), (1,)), ((), ())),
                            preferred_element_type=F32) * dmat_ref[h]
        st = st_ref[h]
        y = (jnp.dot(s.astype(BF16), vh, preferred_element_type=F32)
             + jnp.dot(qs[:, ksl], st.astype(BF16), preferred_element_type=F32))
        st_ref[h] = st * cd_ref[h] + lax.dot_general(ks[:, ksl], vh, (((0,), (0,)), ((), ())),
                                                     preferred_element_type=F32)
        mu = jnp.mean(y, axis=-1, keepdims=True)
        yc = y - mu
        var = jnp.mean(yc * yc, axis=-1, keepdims=True)
        gh = g_ref[:, vsl]
        o_ref[:, vsl] = (gh * jax.nn.sigmoid(gh) * yc * lax.rsqrt(var + 1e-6)).astype(o_ref.dtype)


def _retention(proj, consts):
    bsz, lp = proj.shape[:2]
    kw = RET_HEADS * RET_DK
    cos_t, sin_t, dmat, qd_t, kd_t, cd_t = consts
    est = 2 * BLOCK * 4 * (2 * kw + 2 * BRANCH_WIDTH + 4 * kw) + 2 * RET_HEADS * BLOCK * BLOCK * 4 + (8 << 20)
    tab = pl.BlockSpec((BLOCK, kw), lambda b, n: (n, 0))
    ctab = pl.BlockSpec((BLOCK, kw), lambda b, n: (0, 0))
    return pl.pallas_call(
        _ret_kernel,
        name="retention",
        grid=(bsz, lp // BLOCK),
        in_specs=[pl.BlockSpec((None, BLOCK, kw), lambda b, n: (b, n, 0)),
                  pl.BlockSpec((None, BLOCK, kw), lambda b, n: (b, n, 1)),
                  pl.BlockSpec((None, BLOCK, BRANCH_WIDTH), lambda b, n: (b, n, 1)),
                  pl.BlockSpec((None, BLOCK, BRANCH_WIDTH), lambda b, n: (b, n, 2)),
                  tab, tab,
                  pl.BlockSpec((RET_HEADS, BLOCK, BLOCK), lambda b, n: (0, 0, 0)),
                  ctab, ctab,
                  pl.BlockSpec((RET_HEADS, 1, RET_DV), lambda b, n: (0, 0, 0))],
        out_specs=pl.BlockSpec((None, BLOCK, BRANCH_WIDTH), lambda b, n: (b, n, 0)),
        out_shape=jax.ShapeDtypeStruct((bsz, lp, BRANCH_WIDTH), BF16),
        scratch_shapes=[pltpu.VMEM((RET_HEADS, RET_DK, RET_DV), F32)],
        compiler_params=_params(("parallel", "arbitrary"), est),
    )(proj, proj, proj, proj, cos_t, sin_t, dmat, qd_t, kd_t, cd_t)


def _split_dot(x, ones):
    hi = x.astype(BF16)
    lo = (x - hi.astype(F32)).astype(BF16)
    return (jnp.dot(hi, ones, preferred_element_type=F32) + jnp.dot(lo, ones, preferred_element_type=F32))


_RV_MU, _RV_W0, _RV_A0, _RV_V0, _RV_KK, _RV_KA, _RV_RK = 0, 6, 7, 8, 9, 10, 11
_RV_ROWS = 16


def _rwkv_pre_kernel(*refs, has_vres):
    (u_ref, up_ref, wr_ref, wk_ref, wv_ref, w1_ref, w2_ref, a1_ref, a2_ref, g1_ref, g2_ref,
     ones_ref, vec_ref) = refs[:13]
    pos = 13
    if has_vres:
        vf_ref, v1_ref, v2_ref = refs[13:16]
        pos = 16
    r_out, lw_out, k_out, v_out, a_out, b_out, gate_out, bonus_out = refs[pos:pos + 8]

    def vec(i):
        return vec_ref[i:i + 1, :]

    u = u_ref[...]
    tm = u.shape[0]
    prev_last = jnp.where(pl.program_id(1) == 0, 0.0, up_ref[7:8, :])
    row = lax.broadcasted_iota(jnp.int32, (tm, 1), 0)
    prev = jnp.where(row == 0, prev_last, pltpu.roll(u, 1, 0))
    xx = prev - u

    def mix(i):
        return (u + xx * vec(_RV_MU + i)).astype(BF16)

    def mm(x, w_ref):
        return jnp.dot(x, w_ref[...], preferred_element_type=F32)

    xr, xw, xk, xv, xa, xg = [mix(i) for i in range(6)]
    r = mm(xr, wr_ref)
    k = mm(xk, wk_ref)
    v = mm(xv, wv_ref)
    z = -(vec(_RV_W0) + mm(jnp.tanh(mm(xw, w1_ref)).astype(BF16), w2_ref))
    softplus = jnp.maximum(z, 0.0) + jnp.log(1.0 + jnp.exp(-jnp.abs(z)))
    lw_out[...] = -jnp.exp(-softplus - 0.5)
    if has_vres:
        mv = jax.nn.sigmoid(vec(_RV_V0) + mm(mm(xv, v1_ref).astype(BF16), v2_ref))
        v = v + (vf_ref[...] - v) * mv
    a = jax.nn.sigmoid(vec(_RV_A0) + mm(mm(xa, a1_ref).astype(BF16), a2_ref))
    gate_out[...] = mm(jax.nn.sigmoid(mm(xg, g1_ref)).astype(BF16), g2_ref)

    ones = ones_ref[...]
    kk = k * vec(_RV_KK)
    kk = kk / jnp.maximum(jnp.sqrt(_split_dot(kk * kk, ones)), 1e-12)
    k = k * (1.0 + (a - 1.0) * vec(_RV_KA))
    r_out[...] = r
    k_out[...] = k
    v_out[...] = v
    a_out[...] = -kk
    b_out[...] = kk * a
    bonus_out[...] = _split_dot(r * k * vec(_RV_RK), ones) * v


def _rwkv_pre(proj, u_blk, v_first, vres, mu, w_rkv, w0, w1, w2, a0, a1, a2, g1, g2, k_k, k_a, r_k):
    bsz, lp = proj.shape[:2]
    w = BRANCH_WIDTH
    tm = _pick(lp, (256, 128, 64, 32, 16, 8))
    has_vres = vres is not None

    def lora_pair(wa, wb):
        rp = -(-wa.shape[1] // LANE) * LANE
        return (jnp.pad(wa, ((0, 0), (0, rp - wa.shape[1]))).astype(BF16),
                jnp.pad(wb, ((0, rp - wb.shape[0]), (0, 0))).astype(BF16))

    w1p, w2p = lora_pair(w1, w2)
    a1p, a2p = lora_pair(a1, a2)
    g1p, g2p = lora_pair(g1, g2)
    hid = jnp.arange(w) // RWKV_HEAD
    ones = (hid[:, None] == hid[None, :]).astype(BF16)
    zero = jnp.zeros((w,), F32)
    rows = [mu[i] for i in range(6)] + [w0, a0, vres[0] if has_vres else zero, k_k, k_a, r_k.reshape(w)]
    vec = jnp.stack(rows + [zero] * (_RV_ROWS - len(rows)))

    tile = pl.BlockSpec((None, tm, w), lambda b, i: (b, i, 0))
    const = lambda arr: pl.BlockSpec(arr.shape, lambda b, i: (0,) * arr.ndim)
    weights = [w_rkv[0].astype(BF16), w_rkv[1].astype(BF16), w_rkv[2].astype(BF16),
               w1p, w2p, a1p, a2p, g1p, g2p, ones, vec]
    args = [proj, proj] + weights
    in_specs = [pl.BlockSpec((None, tm, w), lambda b, i: (b, i, u_blk)),
                pl.BlockSpec((None, 8, w), lambda b, i: (b, jnp.maximum(i * (tm // 8) - 1, 0), u_blk))]
    in_specs += [const(x) for x in weights]
    if has_vres:
        v1p, v2p = lora_pair(vres[1], vres[2])
        args += [v_first, v1p, v2p]
        in_specs += [tile, const(v1p), const(v2p)]
    est = 2 * sum(x.size * x.dtype.itemsize for x in weights) + 2 * 11 * tm * w * 4 + 40 * tm * w * 4
    outs = pl.pallas_call(
        functools.partial(_rwkv_pre_kernel, has_vres=has_vres),
        name="rwkv_pre",
        grid=(bsz, lp // tm),
        in_specs=in_specs,
        out_specs=[tile] * 8,
        out_shape=[jax.ShapeDtypeStruct((bsz, lp, w), F32)] * 8,
        compiler_params=_params(("parallel", "parallel"), est),
    )(*args)
    return outs


def _wkv_kernel(r_ref, lw_ref, k_ref, v_ref, a_ref, b_ref, gate_ref, bonus_ref, lnw_ref, lnb_ref,
                o_ref, ht_ref):
    c = RWKV_CHUNK
    wd = RWKV_QW

    @pl.when(pl.program_id(1) == 0)
    def _():
        ht_ref[...] = jnp.zeros_like(ht_ref)

    ti = lax.broadcasted_iota(jnp.int32, (c, c), 0)
    tj = lax.broadcasted_iota(jnp.int32, (c, c), 1)
    cum_all = jnp.dot((ti >= tj).astype(F32), lw_ref[...], preferred_element_type=F32,
                      precision=lax.Precision.HIGHEST)

    ri = lax.broadcasted_iota(jnp.int32, (wd, wd), 0)
    ci = lax.broadcasted_iota(jnp.int32, (wd, wd), 1)
    same = (ri // c) == (ci // RWKV_HEAD)
    strict = same & ((ci % c) < (ri % c))
    incl = same & ((ci % c) <= (ri % c))
    ones_blk = jnp.where(same, 1.0, 0.0).astype(BF16)

    def expand(x):
        xb = x.astype(BF16)
        return jnp.where(same, jnp.concatenate([xb] * RWKV_QUAD, axis=0), jnp.zeros((), BF16))

    def nt(x, y):
        return lax.dot_general(x, y, (((1,), (1,)), ((), ())), preferred_element_type=F32)

    def tn(x, y):
        return lax.dot_general(x, y, (((0,), (0,)), ((), ())), preferred_element_type=F32)

    def mm(x, y):
        return jnp.dot(x, y, preferred_element_type=F32)

    groups = range(r_ref.shape[1] // wd)
    sls = [slice(q * wd, (q + 1) * wd) for q in groups]
    ar, bt, kt, vb, bw, kw, w_last = [], [], [], [], [], [], []
    for sl in sls:
        lw = lw_ref[:, sl]
        cum = cum_all[:, sl]
        cum_last = cum[c - 1:c, :]
        e_neg = jnp.exp(-cum)
        e_rem = jnp.exp(cum_last - cum)
        k = k_ref[:, sl]
        b = b_ref[:, sl]
        ar.append(jnp.concatenate([expand(a_ref[:, sl] * jnp.exp(cum - lw)),
                                   expand(r_ref[:, sl] * jnp.exp(cum))], axis=0))
        bt.append(expand(b * e_neg))
        kt.append(expand(k * e_neg))
        vb.append(expand(v_ref[:, sl]))
        bw.append(expand(b * e_rem))
        kw.append(expand(k * e_rem))
        w_last.append(jnp.exp(cum_last))

    g_b = [nt(ar[q], bt[q]) for q in groups]
    g_k = [nt(ar[q], kt[q]) for q in groups]
    hts = [ht_ref[q] for q in groups]
    g_h = [nt(ar[q], hts[q].astype(BF16)) for q in groups]
    p = [jnp.where(strict, g_b[q][:wd], 0.0) for q in groups]
    a_ak = [jnp.where(strict, g_k[q][:wd], 0.0).astype(BF16) for q in groups]
    a_rb = [jnp.where(incl, g_b[q][wd:], 0.0).astype(BF16) for q in groups]
    a_rk = [jnp.where(incl, g_k[q][wd:], 0.0).astype(BF16) for q in groups]
    u = [g_h[q][:wd] + mm(a_ak[q], vb[q]) for q in groups]
    steps = int(math.log2(c))
    for i in range(steps):
        pb = [x.astype(BF16) for x in p]
        u = [u[q] + mm(pb[q], u[q].astype(BF16)) for q in groups]
        if i < steps - 1:
            p = [mm(pb[q], pb[q]) for q in groups]
    ub = [x.astype(BF16) for x in u]
    y_bd = [g_h[q][wd:] + mm(a_rb[q], ub[q]) + mm(a_rk[q], vb[q]) for q in groups]
    for q in groups:
        ht_ref[q] = hts[q] * w_last[q] + tn(ub[q], bw[q]) + tn(vb[q], kw[q])
    for q in groups:
        sl = sls[q]
        y = y_bd[q][0:c]
        for h in range(1, RWKV_QUAD):
            y = y + y_bd[q][h * c:(h + 1) * c]
        mean = _split_dot(y, ones_blk) * (1.0 / RWKV_HEAD)
        yc = y - mean
        var = _split_dot(yc * yc, ones_blk) * (1.0 / RWKV_HEAD)
        out = yc * lax.rsqrt(var + RWKV_GN_EPS) * lnw_ref[:, sl] + lnb_ref[:, sl] + bonus_ref[:, sl]
        o_ref[:, sl] = (out * gate_ref[:, sl]).astype(o_ref.dtype)


def _wkv7(r, lw, k, v, a, b, gate, bonus, ln_w, ln_b):
    bsz, lp, w = r.shape
    c = RWKV_CHUNK
    spec = pl.BlockSpec((None, c, w), lambda bi, n: (bi, n, 0))
    vspec = pl.BlockSpec((1, w), lambda bi, n: (0, 0))
    est = 2 * 9 * c * w * 4 + 4 * 40 * RWKV_QW * RWKV_QW * 4
    return pl.pallas_call(
        _wkv_kernel,
        name="wkv7",
        grid=(bsz, lp // c),
        in_specs=[spec] * 8 + [vspec, vspec],
        out_specs=spec,
        out_shape=jax.ShapeDtypeStruct((bsz, lp, w), BF16),
        scratch_shapes=[pltpu.VMEM((w // RWKV_QW, RWKV_QW, RWKV_QW), F32)],
        compiler_params=_params(("parallel", "arbitrary"), est),
    )(r, lw, k, v, a, b, gate, bonus, ln_w.reshape(1, w), ln_b.reshape(1, w))


def _fox_kernel(q_ref, k_ref, v_ref, ck_ref, o_ref, acc_ref, st_a, st_b, pt_a, pt_b, qt_ref):
    tk, sw = FOX_TK, FOX_SUB
    tq = q_ref.shape[0]
    nsub = tq // sw
    qi = pl.program_id(2)
    qt_ref[...] = q_ref[...].astype(F32).T.astype(BF16)
    head_lane = lax.broadcasted_iota(jnp.int32, (tk, FOX_HEADS), 1) == pl.program_id(1)
    tri = (lax.broadcasted_iota(jnp.int32, (tk, sw), 0) <= lax.broadcasted_iota(jnp.int32, (tk, sw), 1))
    ones = tuple(jnp.ones((1, sw), F32) for _ in range(nsub))

    def issue_scores(kb, st_out, first_sub=0):
        ks = pl.multiple_of(kb * tk, tk)
        st_out[:, first_sub * sw:] = jnp.dot(k_ref[pl.ds(ks, tk), :], qt_ref[:, first_sub * sw:],
                                             preferred_element_type=F32)

    def issue_pv(kb, pt_in, alphas, first_sub=0):
        ks = pl.multiple_of(kb * tk, tk)
        pv = lax.dot_general(v_ref[pl.ds(ks, tk), :], pt_in[:, first_sub * sw:], (((0,), (0,)), ((), ())),
                             preferred_element_type=F32)
        for r in range(first_sub, nsub):
            acc_ref[r] = alphas[r] * acc_ref[r] + pv[:, (r - first_sub) * sw:(r - first_sub + 1) * sw]

    def softmax(kb, st_in, pt_out, ml, first_sub=0, tri_sub=None):
        ck_heads = ck_ref[pl.ds(pl.multiple_of(kb * tk, tk), tk), :]
        ck = jnp.sum(jnp.where(head_lane, ck_heads, 0.0), axis=1, keepdims=True)
        ck = jnp.broadcast_to(ck, (tk, sw))
        ml, alphas = list(ml), list(ones)
        for r in range(first_sub, nsub):
            s = st_in[:, r * sw:(r + 1) * sw] - ck
            if r == tri_sub:
                s = jnp.where(tri, s, NEG_INF)
            m, l = ml[r]
            m_new = jnp.maximum(m, jnp.max(s, axis=0, keepdims=True))
            alphas[r] = jnp.exp2(m - m_new)
            p = jnp.exp2(s - m_new)
            ml[r] = (m_new, alphas[r] * l + jnp.sum(p, axis=0, keepdims=True))
            pt_out[:, r * sw:(r + 1) * sw] = p.astype(BF16)
        return tuple(ml), tuple(alphas)

    bufs = ((st_a, pt_a), (st_b, pt_b))

    def step(kb, parity, alphas_prev, ml, first_sub=0, tri_sub=None, prev_first=0, next_first=0):
        (st_cur, pt_cur), (st_nxt, pt_prv) = bufs[parity], bufs[1 - parity]
        if next_first is not None:
            issue_scores(kb + 1, st_nxt, next_first)
        issue_pv(jnp.maximum(kb - 1, 0), pt_prv, alphas_prev, prev_first)
        return softmax(kb, st_cur, pt_cur, ml, first_sub, tri_sub)

    def body(i, carry):
        alphas, ml = carry
        ml, alphas = step(2 * i, 0, alphas, ml)
        ml, alphas = step(2 * i + 1, 1, alphas, ml)
        return alphas, ml

    acc_ref[...] = jnp.zeros_like(acc_ref)
    pt_b[...] = jnp.zeros_like(pt_b)
    issue_scores(0, st_a)
    n_main = qi * (tq // tk)
    init_ml = tuple((jnp.full((1, sw), NEG_INF, F32), jnp.zeros((1, sw), F32)) for _ in range(nsub))
    alphas, ml = lax.fori_loop(0, n_main // 2, body, (ones, init_ml))
    ndiag = tq // tk
    for d in range(ndiag):
        ml, alphas = step(n_main + d, d % 2, alphas, ml, first_sub=d, tri_sub=d, prev_first=max(d - 1, 0),
                          next_first=d + 1 if d + 1 < ndiag else None)
    issue_pv(n_main + ndiag - 1, bufs[(ndiag - 1) % 2][1], alphas, ndiag - 1)
    for r in range(nsub):
        o_ref[r * sw:(r + 1) * sw, :] = (acc_ref[r] / ml[r][1]).T.astype(o_ref.dtype)


def _fox(qkv, c, lp_real):
    bsz, lp = qkv.shape[:2]
    h, dh, w = FOX_HEADS, FOX_DH, BRANCH_WIDTH
    tq = FOX_TQ
    pos = jnp.arange(lp)
    ck = jnp.where(((pos >= N_PAD) & (pos < lp_real))[None, :, None], c, -NEG_INF) * LOG2E
    est = 2 * (2 * tq * dh * 2 + 2 * lp * dh * 2 + lp * LANE * 4) + (16 << 20)
    return pl.pallas_call(
        _fox_kernel,
        name="fox",
        grid=(bsz, h, lp // tq),
        in_specs=[pl.BlockSpec((None, tq, dh), lambda b, hh, i: (b, i, hh)),
                  pl.BlockSpec((None, lp, dh), lambda b, hh, i: (b, 0, h + hh)),
                  pl.BlockSpec((None, lp, dh), lambda b, hh, i: (b, 0, 2 * h + hh)),
                  pl.BlockSpec((None, lp, h), lambda b, hh, i: (b, 0, 0))],
        out_specs=pl.BlockSpec((None, tq, dh), lambda b, hh, i: (b, i, hh)),
        out_shape=jax.ShapeDtypeStruct((bsz, lp, w), BF16),
        scratch_shapes=[pltpu.VMEM((tq // FOX_SUB, dh, FOX_SUB), F32),
                        pltpu.VMEM((FOX_TK, tq), F32), pltpu.VMEM((FOX_TK, tq), F32),
                        pltpu.VMEM((FOX_TK, tq), BF16), pltpu.VMEM((FOX_TK, tq), BF16),
                        pltpu.VMEM((dh, tq), BF16)],
        compiler_params=_params(("parallel", "parallel", "parallel"), est),
    )(qkv, qkv, qkv, ck)


S5_CHUNK = 8
S5_SLAB = LANE // S5_GROUP
S5_NSLAB = S5_GROUPS // S5_SLAB


def _s5_tables(a_re, a_im, b_re, b_im, c_re, c_im, log_step):
    hp = lax.Precision.HIGHEST
    g, p, cg, n, sg, ns = S5_GROUPS, S5_STATE, S5_GROUP, S5_CHUNK, S5_SLAB, S5_NSLAB
    dt = jnp.exp(log_step)[:, None]
    mag = jnp.exp(a_re * dt)
    bar_re = mag * jnp.cos(a_im * dt)
    bar_im = mag * jnp.sin(a_im * dt)
    den = a_re * a_re + a_im * a_im
    num_re = bar_re - 1.0
    coef_re = (num_re * a_re + bar_im * a_im) / den
    coef_im = (bar_im * a_re - num_re * a_im) / den
    bb_re = coef_re[..., None] * b_re - coef_im[..., None] * b_im
    bb_im = coef_re[..., None] * b_im + coef_im[..., None] * b_re
    pw_re, pw_im = [jnp.ones_like(bar_re)], [jnp.zeros_like(bar_im)]
    for _ in range(n):
        pw_re, pw_im = (pw_re + [pw_re[-1] * bar_re - pw_im[-1] * bar_im],
                        pw_im + [pw_re[-1] * bar_im + pw_im[-1] * bar_re])
    eye = jnp.eye(sg, dtype=F32)

    w_re = jnp.stack([pw_re[n - 1 - s][..., None] * bb_re - pw_im[n - 1 - s][..., None] * bb_im for s in range(n)])
    w_im = jnp.stack([pw_re[n - 1 - s][..., None] * bb_im + pw_im[n - 1 - s][..., None] * bb_re for s in range(n)])
    gm = jnp.concatenate(
        [jnp.einsum('sjgpc,gh->jsgchp', x.reshape(n, ns, sg, p, cg), eye).reshape(ns, n, sg * cg, sg * p)
         for x in (w_re, w_im)], axis=-1)

    ca_re = jnp.stack([c_re * pw_re[tau][:, None, :] - c_im * pw_im[tau][:, None, :] for tau in range(n + 1)])
    ca_im = jnp.stack([c_re * pw_im[tau][:, None, :] + c_im * pw_re[tau][:, None, :] for tau in range(n + 1)])
    k = (jnp.einsum('tgcp,gpd->tgcd', ca_re[:n], bb_re, precision=hp)
         - jnp.einsum('tgcp,gpd->tgcd', ca_im[:n], bb_im, precision=hp))
    lag = jnp.arange(n)[None, :] - jnp.arange(n)[:, None]
    kst = jnp.where((lag >= 0)[:, :, None, None, None], k[jnp.clip(lag, 0, n - 1)], 0.0)
    mm = jnp.einsum('stjgcd,gh->jsgdthc', kst.reshape(n, n, ns, sg, cg, cg), eye).reshape(ns, n, sg * cg, n * LANE)

    pm = jnp.concatenate(
        [jnp.einsum('tjgcp,gh->jgpthc', x.reshape(n, ns, sg, cg, p), eye).reshape(ns, sg * p, n * LANE)
         for x in (ca_re[1:], -ca_im[1:])], axis=1)
    a8 = jnp.concatenate([pw_re[n].reshape(ns, sg * p), pw_im[n].reshape(ns, sg * p)], axis=1)
    return gm.astype(BF16), mm.astype(BF16), pm.astype(BF16), a8


def _chunk_rows(ref, s, nb):
    return ref[pl.ds(s, nb, stride=S5_CHUNK), :]


def _s5_in_kernel(u_ref, gm_ref, z_ref):
    nb = z_ref.shape[0]
    acc = None
    for s in range(S5_CHUNK):
        d = jnp.dot(_chunk_rows(u_ref, s, nb).astype(BF16), gm_ref[s], preferred_element_type=F32)
        acc = d if acc is None else acc + d
    z_ref[...] = acc


def _s5_state_kernel(z_ref, a_ref, h_ref, st_ref):
    @pl.when(pl.program_id(1) == 0)
    def _():
        st_ref[...] = jnp.zeros_like(st_ref)

    ns, nb, sw2 = z_ref.shape
    sw = sw2 // 2
    coef = [(a_ref[j:j + 1, :sw], a_ref[j:j + 1, sw:]) for j in range(ns)]

    def body(n, carry):
        out = []
        for j in range(ns):
            hr, hi = carry[j]
            h_ref[j, pl.ds(n, 1), :] = jnp.concatenate([hr, hi], axis=1)
            z = z_ref[j, pl.ds(n, 1), :]
            ar, ai = coef[j]
            out.append((ar * hr - ai * hi + z[:, :sw], ar * hi + ai * hr + z[:, sw:]))
        return tuple(out)

    init = tuple((st_ref[j:j + 1, :sw], st_ref[j:j + 1, sw:]) for j in range(ns))
    fin = lax.fori_loop(0, nb, body, init)
    for j in range(ns):
        st_ref[j:j + 1, :sw] = fin[j][0]
        st_ref[j:j + 1, sw:] = fin[j][1]


def _s5_out_kernel(u_ref, h_ref, mm_ref, pm_ref, d_ref, o_ref, y_ref):
    nb = h_ref.shape[0]
    acc = jnp.dot(h_ref[...].astype(BF16), pm_ref[...], preferred_element_type=F32)
    for s in range(S5_CHUNK):
        acc = acc + jnp.dot(_chunk_rows(u_ref, s, nb).astype(BF16), mm_ref[s], preferred_element_type=F32)
    for t in range(S5_CHUNK):
        y_ref[pl.ds(t, nb, stride=S5_CHUNK), :] = acc[:, t * LANE:(t + 1) * LANE]
    o_ref[...] = jax.nn.gelu(y_ref[...] + d_ref[...] * u_ref[...]).astype(o_ref.dtype)


def _s5_branch(proj, u_blk, a_re, a_im, b_re, b_im, c_re, c_im, d_skip, log_step, w_glu):
    bsz, lp = proj.shape[:2]
    w, n, ns = BRANCH_WIDTH, S5_CHUNK, S5_NSLAB
    sw2 = 2 * S5_SLAB * S5_STATE
    tb = _pick(lp, ROW_TILES)
    nb = tb // n
    nchunk = lp // n
    gm, mm, pm, a8 = _s5_tables(a_re, a_im, b_re, b_im, c_re, c_im, log_step)
    u_spec = pl.BlockSpec((None, tb, LANE), lambda b, j, i: (b, i, u_blk * ns + j))
    st_spec = pl.BlockSpec((None, None, nb, sw2), lambda b, j, i: (b, j, i, 0))
    z = pl.pallas_call(
        _s5_in_kernel,
        name="s5_in",
        grid=(bsz, ns, lp // tb),
        in_specs=[u_spec, pl.BlockSpec((None, n, LANE, sw2), lambda b, j, i: (j, 0, 0, 0))],
        out_specs=st_spec,
        out_shape=jax.ShapeDtypeStruct((bsz, ns, nchunk, sw2), F32),
        compiler_params=_params(("parallel", "parallel", "parallel"),
                                2 * (tb * LANE * 4 + n * LANE * sw2 * 2 + nb * sw2 * 4) + 2 * nb * sw2 * 4),
    )(proj, gm)
    nbb = _pick(nchunk, (96, 64, 32, 16, 8))
    blk = pl.BlockSpec((None, ns, nbb, sw2), lambda b, i: (b, 0, i, 0))
    h_start = pl.pallas_call(
        _s5_state_kernel,
        name="s5_state",
        grid=(bsz, nchunk // nbb),
        in_specs=[blk, pl.BlockSpec((ns, sw2), lambda b, i: (0, 0))],
        out_specs=blk,
        out_shape=jax.ShapeDtypeStruct((bsz, ns, nchunk, sw2), F32),
        scratch_shapes=[pltpu.VMEM((ns, sw2), F32)],
        compiler_params=_params(("parallel", "arbitrary"), 4 * ns * nbb * sw2 * 4),
    )(z, a8)
    y = pl.pallas_call(
        _s5_out_kernel,
        name="s5_out",
        grid=(bsz, ns, lp // tb),
        in_specs=[u_spec, st_spec,
                  pl.BlockSpec((None, n, LANE, n * LANE), lambda b, j, i: (j, 0, 0, 0)),
                  pl.BlockSpec((None, sw2, n * LANE), lambda b, j, i: (j, 0, 0)),
                  pl.BlockSpec((1, LANE), lambda b, j, i: (0, j))],
        out_specs=pl.BlockSpec((None, tb, LANE), lambda b, j, i: (b, i, j)),
        out_shape=jax.ShapeDtypeStruct((bsz, lp, w), BF16),
        scratch_shapes=[pltpu.VMEM((tb, LANE), F32)],
        compiler_params=_params(("parallel", "parallel", "parallel"),
                                2 * (tb * LANE * 4 + nb * sw2 * 4 + n * LANE * n * LANE * 2 + sw2 * n * LANE * 2
                                     + tb * LANE * 2) + 3 * nb * n * LANE * 4),
    )(proj, h_start, mm, pm, d_skip.reshape(1, w))
    return _glu(y.reshape(bsz * lp, w), w_glu.astype(BF16))


def _pad_cols(w, n):
    return jnp.pad(w, ((0, 0), (0, n - w.shape[1])))


def _scale_cols(x, s):
    return x * s


MOE_TM = 512


def _new_expert(te_ref, i):
    return (i == 0) | (te_ref[i] != te_ref[jnp.maximum(i - 1, 0)])


def _moe_up_kernel(te_ref, x_ref, wg_ref, wu_ref, s_ref, o_ref, wgb_ref, wub_ref, *, ni):
    i = pl.program_id(1)

    @pl.when(_new_expert(te_ref, i))
    def _():
        wgb_ref[...] = wg_ref[...].astype(BF16)
        wub_ref[...] = wu_ref[...].astype(BF16)

    @pl.when(i < te_ref[ni])
    def _():
        x = x_ref[...]
        g = jnp.dot(x, wgb_ref[...], preferred_element_type=F32)
        u = jnp.dot(x, wub_ref[...], preferred_element_type=F32)
        o_ref[...] = (g * jax.nn.sigmoid(g) * u * s_ref[...]).astype(o_ref.dtype)

    @pl.when(i >= te_ref[ni])
    def _():
        o_ref[...] = jnp.zeros_like(o_ref)


def _moe_down_kernel(te_ref, a_ref, w_ref, o_ref, wb_ref, *, ni):
    i = pl.program_id(1)

    @pl.when(_new_expert(te_ref, i))
    def _():
        wb_ref[...] = w_ref[...].astype(BF16)

    @pl.when(i < te_ref[ni])
    def _():
        o_ref[...] = jnp.dot(a_ref[...], wb_ref[...], preferred_element_type=F32).astype(o_ref.dtype)

    @pl.when(i >= te_ref[ni])
    def _():
        o_ref[...] = jnp.zeros_like(o_ref)


def _moe_experts(xs, row_scale, tile_info, w_gate, w_up, w_down):
    r, d = xs.shape
    e, _, f = w_gate.shape
    tm = MOE_TM
    ni = r // tm
    tn = _pick(f, (512, 256, 128))
    wmap = lambda j, i, te: (te[i], 0, j)
    act = pl.pallas_call(
        functools.partial(_moe_up_kernel, ni=ni),
        name="moe_up",
        grid_spec=pltpu.PrefetchScalarGridSpec(
            num_scalar_prefetch=1, grid=(f // tn, ni),
            in_specs=[pl.BlockSpec((tm, d), lambda j, i, te: (i, 0)),
                      pl.BlockSpec((None, d, tn), wmap), pl.BlockSpec((None, d, tn), wmap),
                      pl.BlockSpec((tm, 1), lambda j, i, te: (i, 0))],
            out_specs=pl.BlockSpec((tm, tn), lambda j, i, te: (i, j)),
            scratch_shapes=[pltpu.VMEM((d, tn), BF16), pltpu.VMEM((d, tn), BF16)]),
        out_shape=jax.ShapeDtypeStruct((r, f), BF16),
        compiler_params=_params(("arbitrary", "arbitrary"),
                                2 * (tm * d * 2 + 2 * d * tn * 4 + tm * tn * 2) + 2 * d * tn * 2 + 4 * tm * tn * 4),
    )(tile_info, xs, w_gate, w_up, row_scale)
    tn2 = _pick(d, (1024, 512, 256, 128))
    return pl.pallas_call(
        functools.partial(_moe_down_kernel, ni=ni),
        name="moe_down",
        grid_spec=pltpu.PrefetchScalarGridSpec(
            num_scalar_prefetch=1, grid=(d // tn2, ni),
            in_specs=[pl.BlockSpec((tm, f), lambda j, i, te: (i, 0)),
                      pl.BlockSpec((None, f, tn2), wmap)],
            out_specs=pl.BlockSpec((tm, tn2), lambda j, i, te: (i, j)),
            scratch_shapes=[pltpu.VMEM((f, tn2), BF16)]),
        out_shape=jax.ShapeDtypeStruct((r, d), F32),
        compiler_params=_params(("arbitrary", "arbitrary"),
                                2 * (tm * f * 2 + f * tn2 * 4 + tm * tn2 * 4) + f * tn2 * 2 + 2 * tm * tn2 * 4),
    )(tile_info, act, w_down)


def _moe_ffn(h32, hb, w_router, b_router, w_gate, w_up, w_down):
    t, d = hb.shape
    e = w_gate.shape[0]
    tm = MOE_TM
    logits = _mm(h32, _pad_cols(w_router, LANE), precision=lax.Precision.HIGHEST,
                 name="router")[:, :N_EXPERTS] + b_router
    top_vals, top_idx = lax.top_k(logits, TOP_K)
    weights = jax.nn.softmax(top_vals, axis=-1)

    n_assign = t * TOP_K
    n_rows = n_assign + e * tm
    flat_e = top_idx.reshape(n_assign).astype(jnp.int32)
    onehot = jax.nn.one_hot(flat_e, e, dtype=jnp.int32)
    seen = jnp.cumsum(onehot, axis=0)
    counts = seen[-1]
    padded = (counts + tm - 1) // tm * tm
    group_end = jnp.cumsum(padded)
    group_start = group_end - padded
    sorted_start = jnp.cumsum(counts) - counts
    dest = group_start[flat_e] + jnp.sum(seen * onehot, axis=1) - 1
    order = jnp.argsort(flat_e, stable=True).astype(jnp.int32)
    rows = jnp.arange(n_rows, dtype=jnp.int32)
    row_e = jnp.minimum(jnp.searchsorted(group_end, rows, side="right"), e - 1).astype(jnp.int32)
    rank = rows - group_start[row_e]
    used = rank < counts[row_e]
    assign = order[jnp.where(used, sorted_start[row_e] + rank, 0)]
    src_tok = jnp.where(used, assign // TOP_K, 0)
    row_scale = jnp.where(used, weights.reshape(n_assign)[assign], 0.0)
    tile_info = jnp.concatenate([row_e[::tm], (group_end[-1:] // tm).astype(jnp.int32)])

    xs = jnp.take(hb, src_tok, axis=0)
    y = _moe_experts(xs, row_scale[:, None], tile_info, w_gate, w_up, w_down)
    dest = dest.reshape(t, TOP_K)
    return [jnp.take(y, dest[:, s], axis=0) for s in range(TOP_K)]


def kernel(x, meta_tokens, w_in, b_forget, w_gate_down, w_gate_up, b_gate, w_branch, w_out, ln_mix_g, ln_mix_b, ln_ffn_g, ln_ffn_b, rwkv_mu, rwkv_w_rkv, rwkv_w0, rwkv_w1, rwkv_w2, rwkv_a0, rwkv_a1, rwkv_a2, rwkv_v0, rwkv_v1, rwkv_v2, rwkv_g1, rwkv_g2, rwkv_k_k, rwkv_k_a, rwkv_r_k, rwkv_ln_w, rwkv_ln_b, s5_a_re, s5_a_im, s5_b_re, s5_b_im, s5_c_re, s5_c_im, s5_d, s5_log_step, s5_w_glu, ffn_w_gate, ffn_w_up, ffn_w_down, moe_router, moe_router_b, moe_w_gate, moe_w_up, moe_w_down):
    bsz, seq, d = x.shape
    depth = w_in.shape[0]
    w = BRANCH_WIDTH
    lp_real = N_PAD + N_META + seq
    lp = -(-lp_real // FOX_TQ) * FOX_TQ
    t = bsz * lp
    meta = jnp.broadcast_to(meta_tokens.astype(x.dtype)[None], (bsz, N_META, d))
    h = jnp.concatenate([jnp.zeros((bsz, N_PAD, d), x.dtype), meta, x,
                         jnp.zeros((bsz, lp - lp_real, d), x.dtype)], axis=1)
    hb = h.astype(BF16)
    pos = jnp.arange(lp)
    real = ((pos >= N_PAD) & (pos < lp_real))[None, :, None]
    f_lo = 7 * w
    q_scale = jnp.concatenate([jnp.full((1, w), FOX_DH ** -0.5 * LOG2E, F32), jnp.ones((1, 2 * w), F32)], axis=1)
    ret_consts = _ret_consts(lp)
    v_first = None
    for layer in range(depth):
        wl = w_in[layer]
        w_f32 = jnp.concatenate([wl[:, :4 * w], wl[:, f_lo + FOX_HEADS:]], axis=1).astype(BF16)
        w_fox = wl[:, 4 * w:f_lo].astype(BF16)
        w_small = _pad_cols(jnp.concatenate([w_gate_down[layer], wl[:, f_lo:f_lo + FOX_HEADS]], axis=1),
                            GATE_RANK + LANE).astype(BF16)
        hb2 = hb.reshape(t, d)
        proj = _mm(hb2, w_f32, name="proj").reshape(bsz, lp, 5 * w)
        qkv = _mm(hb2, w_fox, out_dtype=BF16, ep=_scale_cols, extras=((q_scale, "row", 0),),
                  name="proj_fox").reshape(bsz, lp, 3 * w)
        small = _mm(hb2, w_small, name="proj_small")
        f_logit = small[:, GATE_RANK:GATE_RANK + FOX_HEADS].reshape(bsz, lp, FOX_HEADS)

        o_ret = _retention(proj, ret_consts)

        vres = None if layer == 0 else (rwkv_v0[layer - 1], rwkv_v1[layer - 1], rwkv_v2[layer - 1])
        r, lw, k, v, a, b, gate, bonus = _rwkv_pre(
            proj, 3, v_first, vres, rwkv_mu[layer], rwkv_w_rkv[layer], rwkv_w0[layer], rwkv_w1[layer],
            rwkv_w2[layer], rwkv_a0[layer], rwkv_a1[layer], rwkv_a2[layer], rwkv_g1[layer], rwkv_g2[layer],
            rwkv_k_k[layer], rwkv_k_a[layer], rwkv_r_k[layer])
        if layer == 0:
            v_first = v
        o_rwkv = _wkv7(r, lw, k, v, a, b, gate, bonus, rwkv_ln_w[layer], rwkv_ln_b[layer])

        log_f = jnp.where(real, jax.nn.log_sigmoid(f_logit + b_forget[layer]), 0.0)
        c = jnp.cumsum(log_f, axis=1)
        o_fox = _fox(qkv, c, lp_real)

        o_s5 = _s5_branch(proj, 4, s5_a_re[layer], s5_a_im[layer], s5_b_re[layer], s5_b_im[layer],
                          s5_c_re[layer], s5_c_im[layer], s5_d[layer], s5_log_step[layer], s5_w_glu[layer])

        merged = _merge([o_ret.reshape(t, w), o_rwkv.reshape(t, w), o_fox.reshape(t, w), o_s5],
                        w_branch[layer].astype(BF16), small, w_gate_up[layer].astype(BF16), b_gate[layer])
        mix = _mm(merged, w_out[layer], name="w_out")
        h, hb = _res_ln([mix], h, ln_mix_g[layer], ln_mix_b[layer], mask_pad=None)

        j = layer // 2
        hb2 = hb.reshape(t, d)
        if layer % 2 == 0:
            act = _swiglu_up(hb2, ffn_w_gate[j].astype(BF16), ffn_w_up[j].astype(BF16))
            ffn = [_mm(act, ffn_w_down[j], name="ffn_down")]
        else:
            ffn = _moe_ffn(h.reshape(t, d), hb2, moe_router[j], moe_router_b[j],
                           moe_w_gate[j], moe_w_up[j], moe_w_down[j])
        h, hb = _res_ln(ffn, h, ln_ffn_g[layer], ln_ffn_b[layer], mask_pad=(N_PAD, lp_real))
    return h[:, N_PAD + N_META:lp_real]
```

```python
import functools
import math

import jax
import jax.numpy as jnp
from jax import lax
from jax.experimental import pallas as pl
from jax.experimental.pallas import tpu as pltpu

F32 = jnp.float32
BF16 = jnp.bfloat16

D_MODEL = 4096
DEPTH = 4
N_META = 16
BLOCK = 128
N_PAD = (-N_META) % BLOCK
BRANCH_WIDTH = D_MODEL // 4
GATE_RANK = 256
RET_DK = 64
RET_DV = 128
RET_HEADS = BRANCH_WIDTH // RET_DV
RWKV_HEAD = 64
RWKV_HEADS = BRANCH_WIDTH // RWKV_HEAD
RWKV_GN_EPS = 64e-5
FOX_DH = 128
FOX_HEADS = BRANCH_WIDTH // FOX_DH
S5_GROUP = 16
S5_GROUPS = BRANCH_WIDTH // S5_GROUP
S5_STATE = 64
N_EXPERTS = 8
TOP_K = 2
ALPHA = (2.0 * DEPTH) ** 0.25
LN_EPS = 1e-5
NEG_INF = -1e30
LOG2E = math.log2(math.e)

V7X_VMEM_BUDGET = 58 * 1024 * 1024
LANE = 128
RWKV_CHUNK = 64
RWKV_QUAD = 4
RWKV_QW = RWKV_QUAD * RWKV_HEAD
FOX_TK = 128
FOX_SUB = 128
FOX_TQ = 768
ROW_TILES = (1408, 1536, 1280, 768, 640, 512, 256, 128)


def _pick(dim, candidates):
    for c in candidates:
        if dim % c == 0:
            return c
    return dim


def _params(sem, est_bytes):
    limit = int(min(V7X_VMEM_BUDGET, max(est_bytes * 5 // 4 + (2 << 20), 16 << 20)))
    return pltpu.CompilerParams(dimension_semantics=sem, vmem_limit_bytes=limit)


def _nbytes(shape, dtype):
    n = jnp.dtype(dtype).itemsize
    for s in shape:
        if s is not None:
            n *= s
    return n


def _mmf_kernel(*refs, nb, nx, nk, ep, precision):
    a_ref = refs[0]
    b_refs = refs[1:1 + nb]
    x_refs = refs[1 + nb:1 + nb + nx]
    o_ref = refs[1 + nb + nx]
    acc_refs = refs[2 + nb + nx:]
    a = a_ref[...]
    if precision is None and a.dtype != BF16:
        a = a.astype(BF16)
    def weight(b_ref):
        b = b_ref[...]
        return b.astype(BF16) if precision is None and b.dtype != BF16 else b

    dots = [jnp.dot(a, weight(b), preferred_element_type=F32, precision=precision) for b in b_refs]

    def finish(vals):
        o_ref[...] = ep(*vals, *[x[...] for x in x_refs]).astype(o_ref.dtype)

    if nk == 1:
        finish(dots)
        return
    k = pl.program_id(2)

    @pl.when(k == 0)
    def _():
        for acc, d in zip(acc_refs, dots):
            acc[...] = d

    @pl.when(k > 0)
    def _():
        for acc, d in zip(acc_refs, dots):
            acc[...] += d

    @pl.when(k == nk - 1)
    def _():
        finish([acc[...] for acc in acc_refs])


def _mmf(a_spec, b_specs, x_specs, ep, *, m, n, kd, tm, tn, tk, out_dtype, precision=None, name="mm"):
    nk = kd // tk
    specs = [a_spec] + list(b_specs) + list(x_specs)
    est = 2 * sum(_nbytes(s[1], s[0].dtype) for s in specs) + 2 * tm * tn * jnp.dtype(out_dtype).itemsize
    est += (len(b_specs) + 2) * tm * tn * 4
    scratch = [pltpu.VMEM((tm, tn), F32) for _ in b_specs] if nk > 1 else []
    return pl.pallas_call(
        functools.partial(_mmf_kernel, nb=len(b_specs), nx=len(x_specs), nk=nk, ep=ep, precision=precision),
        name=name,
        grid=(m // tm, n // tn, nk),
        in_specs=[pl.BlockSpec(s[1], s[2]) for s in specs],
        out_specs=pl.BlockSpec((tm, tn), lambda i, j, k: (i, j)),
        out_shape=jax.ShapeDtypeStruct((m, n), out_dtype),
        scratch_shapes=scratch,
        compiler_params=_params(("parallel", "parallel", "arbitrary"), est),
    )(*[s[0] for s in specs])


def _ident(x):
    return x


def _mm(a, b, out_dtype=F32, precision=None, ep=_ident, extras=(), a_blk=0, b_layer=None, name="mm"):
    m = a.shape[0]
    kd, n = b.shape[-2:]
    tm = _pick(m, ROW_TILES)
    tn = _pick(n, (1024, 512, 384, 256, 128))
    tk = kd if kd <= 4096 else _pick(kd, (2048, 1536, 1024, 512))
    if precision is not None:
        tm = _pick(m, (640, 512, 256, 128))
        tn = _pick(n, (512, 384, 256, 128))
        tk = kd if kd <= 2048 else _pick(kd, (2048, 1024, 512))
    nk = kd // tk
    ob = jnp.dtype(out_dtype).itemsize

    def need(tn_):
        return (2 * (tm * tk * a.dtype.itemsize + tk * tn_ * b.dtype.itemsize + tm * tn_ * ob)
                + (3 if nk > 1 else 2) * tm * tn_ * 4)

    while need(tn) > V7X_VMEM_BUDGET - (6 << 20) and tn % 256 == 0 and n % (tn // 2) == 0:
        tn //= 2
    a_spec = (a, (tm, tk), lambda i, j, k: (i, a_blk * nk + k))
    if b_layer is None:
        b_spec = (b, (tk, tn), lambda i, j, k: (k, j))
    else:
        b_spec = (b, (None, tk, tn), lambda i, j, k: (b_layer, k, j))
    x_specs = []
    for arr, kind, off in extras:
        if kind == "tile":
            x_specs.append((arr, (tm, tn), lambda i, j, k, off=off: (i, j + off)))
        else:
            x_specs.append((arr, (1, tn), lambda i, j, k, off=off: (0, j + off)))
    return _mmf(a_spec, [b_spec], x_specs, ep, m=m, n=n, kd=kd, tm=tm, tn=tn, tk=tk,
                out_dtype=out_dtype, precision=precision, name=name)


def _silu_mul(g, u):
    return g * jax.nn.sigmoid(g) * u


def _swiglu_up(h, wg, wu):
    m, d = h.shape
    f = wg.shape[1]
    tm = _pick(m, ROW_TILES)
    tn = _pick(f, (512, 256, 128))
    wmap = lambda i, j, k: (0, j)
    return _mmf((h, (tm, d), lambda i, j, k: (i, 0)),
                [(wg, (d, tn), wmap), (wu, (d, tn), wmap)], [], _silu_mul,
                m=m, n=f, kd=d, tm=tm, tn=tn, tk=d, out_dtype=BF16, name="swiglu_up")


def _glu_ep(val, gt):
    return val * jax.nn.sigmoid(gt)


def _glu(y, w_glu):
    m, kd = y.shape
    n = w_glu.shape[1] // 2
    tm = _pick(m, ROW_TILES)
    tn = _pick(n, (512, 256, 128))
    nj = n // tn
    return _mmf((y, (tm, kd), lambda i, j, k: (i, 0)),
                [(w_glu, (kd, tn), lambda i, j, k: (0, j)), (w_glu, (kd, tn), lambda i, j, k: (0, nj + j))],
                [], _glu_ep, m=m, n=n, kd=kd, tm=tm, tn=tn, tk=kd, out_dtype=BF16, name="s5_glu")


def _res_ln_kernel(*refs, ny, tm, mask_pad):
    y_refs = refs[:ny]
    r_ref, g_ref, b_ref, o_ref, ob_ref = refs[ny:]
    z = ALPHA * r_ref[...] + y_refs[0][...]
    for y_ref in y_refs[1:]:
        z = z + y_ref[...]
    mu = jnp.mean(z, axis=-1, keepdims=True)
    zc = z - mu
    var = jnp.mean(zc * zc, axis=-1, keepdims=True)
    out = zc * lax.rsqrt(var + LN_EPS) * g_ref[...] + b_ref[...]
    if mask_pad is not None:
        pos = pl.program_id(1) * tm + lax.broadcasted_iota(jnp.int32, (tm, 1), 0)
        out = jnp.where((pos >= mask_pad[0]) & (pos < mask_pad[1]), out, 0.0)
    o_ref[...] = out
    ob_ref[...] = out.astype(BF16)


def _res_ln(ys, res, g, b, mask_pad):
    bsz, lp, d = res.shape
    tm = _pick(lp, (256, 128, 64, 32, 16, 8))
    est = 2 * ((4 + len(ys)) * tm * d * 4) + 4 * tm * d * 4
    spec = pl.BlockSpec((None, tm, d), lambda bi, i: (bi, i, 0))
    vspec = pl.BlockSpec((1, d), lambda bi, i: (0, 0))
    return pl.pallas_call(
        functools.partial(_res_ln_kernel, ny=len(ys), tm=tm, mask_pad=mask_pad),
        name="res_ln",
        grid=(bsz, lp // tm),
        in_specs=[spec] * (len(ys) + 1) + [vspec, vspec],
        out_specs=[spec, spec],
        out_shape=[jax.ShapeDtypeStruct((bsz, lp, d), F32), jax.ShapeDtypeStruct((bsz, lp, d), BF16)],
        compiler_params=_params(("parallel", "parallel"), est),
    )(*[y.reshape(bsz, lp, d) for y in ys], res, g.reshape(1, d), b.reshape(1, d))


def _merge_kernel(o0, o1, o2, o3, wb_ref, glr_ref, wgu_ref, bg_ref, out_ref):
    glr = glr_ref[...].astype(BF16)
    acc = None
    for i, o_ref in enumerate((o0, o1, o2, o3)):
        gate = jax.nn.sigmoid(jnp.dot(glr, wgu_ref[i], preferred_element_type=F32) + bg_ref[i])
        term = gate * jnp.dot(o_ref[...], wb_ref[i], preferred_element_type=F32)
        acc = term if acc is None else acc + term
    out_ref[...] = acc.astype(out_ref.dtype)


def _merge(outs, w_branch, small, w_gate_up, b_gate):
    m, w = outs[0].shape
    nb, r, d = w_gate_up.shape
    tm = _pick(m, ROW_TILES)
    tn = _pick(d, (512, 256, 128))
    est = 2 * (nb * tm * w * 2 + nb * w * tn * 2 + tm * r * 4 + nb * r * tn * 2 + tm * tn * 2) + 4 * tm * tn * 4
    ospec = pl.BlockSpec((tm, w), lambda i, j: (i, 0))
    return pl.pallas_call(
        _merge_kernel,
        name="merge",
        grid=(m // tm, d // tn),
        in_specs=[ospec, ospec, ospec, ospec,
                  pl.BlockSpec((nb, w, tn), lambda i, j: (0, 0, j)),
                  pl.BlockSpec((tm, r), lambda i, j: (i, 0)),
                  pl.BlockSpec((nb, r, tn), lambda i, j: (0, 0, j)),
                  pl.BlockSpec((nb, 1, tn), lambda i, j: (0, 0, j))],
        out_specs=pl.BlockSpec((tm, tn), lambda i, j: (i, j)),
        out_shape=jax.ShapeDtypeStruct((m, d), BF16),
        compiler_params=_params(("parallel", "parallel"), est),
    )(*outs, w_branch, small, w_gate_up, b_gate.reshape(nb, 1, d))


def _ret_consts(lp):
    f32 = F32
    half = RET_DK // 2
    inv = 1.0 / (10000.0 ** jnp.linspace(0.0, 1.0, half, dtype=f32))
    ang = jnp.arange(lp, dtype=f32)[:, None] * inv[None, :]
    cos, sin = jnp.cos(ang), jnp.sin(ang)
    cos_t = jnp.tile(jnp.concatenate([cos, cos], axis=-1), (1, RET_HEADS))
    sin_t = jnp.tile(jnp.concatenate([-sin, sin], axis=-1), (1, RET_HEADS))
    log_gamma = jnp.log(1.0 - 2.0 ** (-5.0 - jnp.arange(RET_HEADS, dtype=f32)))
    pos = jnp.arange(BLOCK, dtype=f32)
    rel = pos[:, None] - pos[None, :]
    dmat = jnp.where(rel >= 0, jnp.exp(log_gamma[:, None, None] * jnp.maximum(rel, 0.0)), 0.0)
    k_decay = jnp.exp(log_gamma[:, None] * (BLOCK - 1.0 - pos)[None, :])
    q_decay = jnp.exp(log_gamma[:, None] * (pos + 1.0)[None, :])
    kd_t = jnp.repeat(k_decay.T, RET_DK, axis=1)
    qd_t = jnp.repeat(q_decay.T, RET_DK, axis=1)
    cd = jnp.exp(log_gamma * BLOCK)
    cd_t = jnp.broadcast_to(cd[:, None, None], (RET_HEADS, 1, RET_DV))
    return cos_t, sin_t, dmat, qd_t, kd_t, cd_t


def _ret_kernel(q_ref, k_ref, v_ref, g_ref, cos_ref, sin_ref, dmat_ref, qd_ref, kd_ref, cd_ref,
                o_ref, st_ref):
    @pl.when(pl.program_id(1) == 0)
    def _():
        st_ref[...] = jnp.zeros_like(st_ref)

    cos = cos_ref[...]
    sin = sin_ref[...]
    width = RET_HEADS * RET_DK
    lane = lax.broadcasted_iota(jnp.int32, (BLOCK, width), 1)
    first = (lane % RET_DK) < (RET_DK // 2)

    def rot(x):
        sw = jnp.where(first, pltpu.roll(x, width - RET_DK // 2, 1), pltpu.roll(x, RET_DK // 2, 1))
        return x * cos + sw * sin

    q = rot(q_ref[...])
    k = rot(k_ref[...]) * RET_DK ** -0.5
    qb = q.astype(BF16)
    kb = k.astype(BF16)
    qs = (q * qd_ref[...]).astype(BF16)
    ks = (k * kd_ref[...]).astype(BF16)
    for h in range(RET_HEADS):
        ksl = slice(h * RET_DK, (h + 1) * RET_DK)
        vsl = slice(h * RET_DV, (h + 1) * RET_DV)
        vh = v_ref[:, vsl].astype(BF16)
        s = lax.dot_general(qb[:, ksl], kb[:, ksl], (((1,), (1,)), ((), ())),
                            preferred_element_type=F32) * dmat_ref[h]
        st = st_ref[h]
        y = (jnp.dot(s.astype(BF16), vh, preferred_element_type=F32)
             + jnp.dot(qs[:, ksl], st.astype(BF16), preferred_element_type=F32))
        st_ref[h] = st * cd_ref[h] + lax.dot_general(ks[:, ksl], vh, (((0,), (0,)), ((), ())),
                                                     preferred_element_type=F32)
        mu = jnp.mean(y, axis=-1, keepdims=True)
        yc = y - mu
        var = jnp.mean(yc * yc, axis=-1, keepdims=True)
        gh = g_ref[:, vsl]
        o_ref[:, vsl] = (gh * jax.nn.sigmoid(gh) * yc * lax.rsqrt(var + 1e-6)).astype(o_ref.dtype)


def _retention(proj, consts):
    bsz, lp = proj.shape[:2]
    kw = RET_HEADS * RET_DK
    cos_t, sin_t, dmat, qd_t, kd_t, cd_t = consts
    est = 2 * BLOCK * 4 * (2 * kw + 2 * BRANCH_WIDTH + 4 * kw) + 2 * RET_HEADS * BLOCK * BLOCK * 4 + (8 << 20)
    tab = pl.BlockSpec((BLOCK, kw), lambda b, n: (n, 0))
    ctab = pl.BlockSpec((BLOCK, kw), lambda b, n: (0, 0))
    return pl.pallas_call(
        _ret_kernel,
        name="retention",
        grid=(bsz, lp // BLOCK),
        in_specs=[pl.BlockSpec((None, BLOCK, kw), lambda b, n: (b, n, 0)),
                  pl.BlockSpec((None, BLOCK, kw), lambda b, n: (b, n, 1)),
                  pl.BlockSpec((None, BLOCK, BRANCH_WIDTH), lambda b, n: (b, n, 1)),
                  pl.BlockSpec((None, BLOCK, BRANCH_WIDTH), lambda b, n: (b, n, 2)),
                  tab, tab,
                  pl.BlockSpec((RET_HEADS, BLOCK, BLOCK), lambda b, n: (0, 0, 0)),
                  ctab, ctab,
                  pl.BlockSpec((RET_HEADS, 1, RET_DV), lambda b, n: (0, 0, 0))],
        out_specs=pl.BlockSpec((None, BLOCK, BRANCH_WIDTH), lambda b, n: (b, n, 0)),
        out_shape=jax.ShapeDtypeStruct((bsz, lp, BRANCH_WIDTH), BF16),
        scratch_shapes=[pltpu.VMEM((RET_HEADS, RET_DK, RET_DV), F32)],
        compiler_params=_params(("parallel", "arbitrary"), est),
    )(proj, proj, proj, proj, cos_t, sin_t, dmat, qd_t, kd_t, cd_t)


def _split_dot(x, ones):
    hi = x.astype(BF16)
    lo = (x - hi.astype(F32)).astype(BF16)
    return (jnp.dot(hi, ones, preferred_element_type=F32) + jnp.dot(lo, ones, preferred_element_type=F32))


_RV_MU, _RV_W0, _RV_A0, _RV_V0, _RV_KK, _RV_KA, _RV_RK = 0, 6, 7, 8, 9, 10, 11
_RV_ROWS = 16


def _rwkv_pre_kernel(*refs, has_vres):
    (u_ref, up_ref, wr_ref, wk_ref, wv_ref, w1_ref, w2_ref, a1_ref, a2_ref, g1_ref, g2_ref,
     ones_ref, vec_ref) = refs[:13]
    pos = 13
    if has_vres:
        vf_ref, v1_ref, v2_ref = refs[13:16]
        pos = 16
    r_out, lw_out, k_out, v_out, a_out, b_out, gate_out, bonus_out = refs[pos:pos + 8]

    def vec(i):
        return vec_ref[i:i + 1, :]

    u = u_ref[...]
    tm = u.shape[0]
    prev_last = jnp.where(pl.program_id(1) == 0, 0.0, up_ref[7:8, :])
    row = lax.broadcasted_iota(jnp.int32, (tm, 1), 0)
    prev = jnp.where(row == 0, prev_last, pltpu.roll(u, 1, 0))
    xx = prev - u

    def mix(i):
        return (u + xx * vec(_RV_MU + i)).astype(BF16)

    def mm(x, w_ref):
        return jnp.dot(x, w_ref[...], preferred_element_type=F32)

    xr, xw, xk, xv, xa, xg = [mix(i) for i in range(6)]
    r = mm(xr, wr_ref)
    k = mm(xk, wk_ref)
    v = mm(xv, wv_ref)
    z = -(vec(_RV_W0) + mm(jnp.tanh(mm(xw, w1_ref)).astype(BF16), w2_ref))
    softplus = jnp.maximum(z, 0.0) + jnp.log(1.0 + jnp.exp(-jnp.abs(z)))
    lw_out[...] = -jnp.exp(-softplus - 0.5)
    if has_vres:
        mv = jax.nn.sigmoid(vec(_RV_V0) + mm(mm(xv, v1_ref).astype(BF16), v2_ref))
        v = v + (vf_ref[...] - v) * mv
    a = jax.nn.sigmoid(vec(_RV_A0) + mm(mm(xa, a1_ref).astype(BF16), a2_ref))
    gate_out[...] = mm(jax.nn.sigmoid(mm(xg, g1_ref)).astype(BF16), g2_ref)

    ones = ones_ref[...]
    kk = k * vec(_RV_KK)
    kk = kk / jnp.maximum(jnp.sqrt(_split_dot(kk * kk, ones)), 1e-12)
    k = k * (1.0 + (a - 1.0) * vec(_RV_KA))
    r_out[...] = r
    k_out[...] = k
    v_out[...] = v
    a_out[...] = -kk
    b_out[...] = kk * a
    bonus_out[...] = _split_dot(r * k * vec(_RV_RK), ones) * v


def _rwkv_pre(proj, u_blk, v_first, vres, mu, w_rkv, w0, w1, w2, a0, a1, a2, g1, g2, k_k, k_a, r_k):
    bsz, lp = proj.shape[:2]
    w = BRANCH_WIDTH
    tm = _pick(lp, (256, 128, 64, 32, 16, 8))
    has_vres = vres is not None

    def lora_pair(wa, wb):
        rp = -(-wa.shape[1] // LANE) * LANE
        return (jnp.pad(wa, ((0, 0), (0, rp - wa.shape[1]))).astype(BF16),
                jnp.pad(wb, ((0, rp - wb.shape[0]), (0, 0))).astype(BF16))

    w1p, w2p = lora_pair(w1, w2)
    a1p, a2p = lora_pair(a1, a2)
    g1p, g2p = lora_pair(g1, g2)
    hid = jnp.arange(w) // RWKV_HEAD
    ones = (hid[:, None] == hid[None, :]).astype(BF16)
    zero = jnp.zeros((w,), F32)
    rows = [mu[i] for i in range(6)] + [w0, a0, vres[0] if has_vres else zero, k_k, k_a, r_k.reshape(w)]
    vec = jnp.stack(rows + [zero] * (_RV_ROWS - len(rows)))

    tile = pl.BlockSpec((None, tm, w), lambda b, i: (b, i, 0))
    const = lambda arr: pl.BlockSpec(arr.shape, lambda b, i: (0,) * arr.ndim)
    weights = [w_rkv[0].astype(BF16), w_rkv[1].astype(BF16), w_rkv[2].astype(BF16),
               w1p, w2p, a1p, a2p, g1p, g2p, ones, vec]
    args = [proj, proj] + weights
    in_specs = [pl.BlockSpec((None, tm, w), lambda b, i: (b, i, u_blk)),
                pl.BlockSpec((None, 8, w), lambda b, i: (b, jnp.maximum(i * (tm // 8) - 1, 0), u_blk))]
    in_specs += [const(x) for x in weights]
    if has_vres:
        v1p, v2p = lora_pair(vres[1], vres[2])
        args += [v_first, v1p, v2p]
        in_specs += [tile, const(v1p), const(v2p)]
    est = 2 * sum(x.size * x.dtype.itemsize for x in weights) + 2 * 11 * tm * w * 4 + 40 * tm * w * 4
    outs = pl.pallas_call(
        functools.partial(_rwkv_pre_kernel, has_vres=has_vres),
        name="rwkv_pre",
        grid=(bsz, lp // tm),
        in_specs=in_specs,
        out_specs=[tile] * 8,
        out_shape=[jax.ShapeDtypeStruct((bsz, lp, w), F32)] * 8,
        compiler_params=_params(("parallel", "parallel"), est),
    )(*args)
    return outs


def _wkv_kernel(r_ref, lw_ref, k_ref, v_ref, a_ref, b_ref, gate_ref, bonus_ref, lnw_ref, lnb_ref,
                o_ref, ht_ref):
    c = RWKV_CHUNK
    wd = RWKV_QW

    @pl.when(pl.program_id(1) == 0)
    def _():
        ht_ref[...] = jnp.zeros_like(ht_ref)

    ti = lax.broadcasted_iota(jnp.int32, (c, c), 0)
    tj = lax.broadcasted_iota(jnp.int32, (c, c), 1)
    cum_all = jnp.dot((ti >= tj).astype(F32), lw_ref[...], preferred_element_type=F32,
                      precision=lax.Precision.HIGHEST)

    ri = lax.broadcasted_iota(jnp.int32, (wd, wd), 0)
    ci = lax.broadcasted_iota(jnp.int32, (wd, wd), 1)
    same = (ri // c) == (ci // RWKV_HEAD)
    strict = same & ((ci % c) < (ri % c))
    incl = same & ((ci % c) <= (ri % c))
    ones_blk = jnp.where(same, 1.0, 0.0).astype(BF16)

    def expand(x):
        xb = x.astype(BF16)
        return jnp.where(same, jnp.concatenate([xb] * RWKV_QUAD, axis=0), jnp.zeros((), BF16))

    def nt(x, y):
        return lax.dot_general(x, y, (((1,), (1,)), ((), ())), preferred_element_type=F32)

    def tn(x, y):
        return lax.dot_general(x, y, (((0,), (0,)), ((), ())), preferred_element_type=F32)

    def mm(x, y):
        return jnp.dot(x, y, preferred_element_type=F32)

    groups = range(r_ref.shape[1] // wd)
    sls = [slice(q * wd, (q + 1) * wd) for q in groups]
    ar, bt, kt, vb, bw, kw, w_last = [], [], [], [], [], [], []
    for sl in sls:
        lw = lw_ref[:, sl]
        cum = cum_all[:, sl]
        cum_last = cum[c - 1:c, :]
        e_neg = jnp.exp(-cum)
        e_rem = jnp.exp(cum_last - cum)
        k = k_ref[:, sl]
        b = b_ref[:, sl]
        ar.append(jnp.concatenate([expand(a_ref[:, sl] * jnp.exp(cum - lw)),
                                   expand(r_ref[:, sl] * jnp.exp(cum))], axis=0))
        bt.append(expand(b * e_neg))
        kt.append(expand(k * e_neg))
        vb.append(expand(v_ref[:, sl]))
        bw.append(expand(b * e_rem))
        kw.append(expand(k * e_rem))
        w_last.append(jnp.exp(cum_last))

    g_b = [nt(ar[q], bt[q]) for q in groups]
    g_k = [nt(ar[q], kt[q]) for q in groups]
    hts = [ht_ref[q] for q in groups]
    g_h = [nt(ar[q], hts[q].astype(BF16)) for q in groups]
    p = [jnp.where(strict, g_b[q][:wd], 0.0) for q in groups]
    a_ak = [jnp.where(strict, g_k[q][:wd], 0.0).astype(BF16) for q in groups]
    a_rb = [jnp.where(incl, g_b[q][wd:], 0.0).astype(BF16) for q in groups]
    a_rk = [jnp.where(incl, g_k[q][wd:], 0.0).astype(BF16) for q in groups]
    u = [g_h[q][:wd] + mm(a_ak[q], vb[q]) for q in groups]
    steps = int(math.log2(c))
    for i in range(steps):
        pb = [x.astype(BF16) for x in p]
        u = [u[q] + mm(pb[q], u[q].astype(BF16)) for q in groups]
        if i < steps - 1:
            p = [mm(pb[q], pb[q]) for q in groups]
    ub = [x.astype(BF16) for x in u]
    y_bd = [g_h[q][wd:] + mm(a_rb[q], ub[q]) + mm(a_rk[q], vb[q]) for q in groups]
    for q in groups:
        ht_ref[q] = hts[q] * w_last[q] + tn(ub[q], bw[q]) + tn(vb[q], kw[q])
    for q in groups:
        sl = sls[q]
        y = y_bd[q][0:c]
        for h in range(1, RWKV_QUAD):
            y = y + y_bd[q][h * c:(h + 1) * c]
        mean = _split_dot(y, ones_blk) * (1.0 / RWKV_HEAD)
        yc = y - mean
        var = _split_dot(yc * yc, ones_blk) * (1.0 / RWKV_HEAD)
        out = yc * lax.rsqrt(var + RWKV_GN_EPS) * lnw_ref[:, sl] + lnb_ref[:, sl] + bonus_ref[:, sl]
        o_ref[:, sl] = (out * gate_ref[:, sl]).astype(o_ref.dtype)


def _wkv7(r, lw, k, v, a, b, gate, bonus, ln_w, ln_b):
    bsz, lp, w = r.shape
    c = RWKV_CHUNK
    spec = pl.BlockSpec((None, c, w), lambda bi, n: (bi, n, 0))
    vspec = pl.BlockSpec((1, w), lambda bi, n: (0, 0))
    est = 2 * 9 * c * w * 4 + 4 * 40 * RWKV_QW * RWKV_QW * 4
    return pl.pallas_call(
        _wkv_kernel,
        name="wkv7",
        grid=(bsz, lp // c),
        in_specs=[spec] * 8 + [vspec, vspec],
        out_specs=spec,
        out_shape=jax.ShapeDtypeStruct((bsz, lp, w), BF16),
        scratch_shapes=[pltpu.VMEM((w // RWKV_QW, RWKV_QW, RWKV_QW), F32)],
        compiler_params=_params(("parallel", "arbitrary"), est),
    )(r, lw, k, v, a, b, gate, bonus, ln_w.reshape(1, w), ln_b.reshape(1, w))


def _fox_kernel(q_ref, k_ref, v_ref, ck_ref, o_ref, acc_ref, st_a, st_b, pt_a, pt_b, qt_ref):
    tk, sw = FOX_TK, FOX_SUB
    tq = q_ref.shape[0]
    nsub = tq // sw
    qi = pl.program_id(2)
    qt_ref[...] = q_ref[...].astype(F32).T.astype(BF16)
    head_lane = lax.broadcasted_iota(jnp.int32, (tk, FOX_HEADS), 1) == pl.program_id(1)
    tri = (lax.broadcasted_iota(jnp.int32, (tk, sw), 0) <= lax.broadcasted_iota(jnp.int32, (tk, sw), 1))
    ones = tuple(jnp.ones((1, sw), F32) for _ in range(nsub))

    def issue_scores(kb, st_out, first_sub=0):
        ks = pl.multiple_of(kb * tk, tk)
        st_out[:, first_sub * sw:] = jnp.dot(k_ref[pl.ds(ks, tk), :], qt_ref[:, first_sub * sw:],
                                             preferred_element_type=F32)

    def issue_pv(kb, pt_in, alphas, first_sub=0):
        ks = pl.multiple_of(kb * tk, tk)
        pv = lax.dot_general(v_ref[pl.ds(ks, tk), :], pt_in[:, first_sub * sw:], (((0,), (0,)), ((), ())),
                             preferred_element_type=F32)
        for r in range(first_sub, nsub):
            acc_ref[r] = alphas[r] * acc_ref[r] + pv[:, (r - first_sub) * sw:(r - first_sub + 1) * sw]

    def softmax(kb, st_in, pt_out, ml, first_sub=0, tri_sub=None):
        ck_heads = ck_ref[pl.ds(pl.multiple_of(kb * tk, tk), tk), :]
        ck = jnp.sum(jnp.where(head_lane, ck_heads, 0.0), axis=1, keepdims=True)
        ck = jnp.broadcast_to(ck, (tk, sw))
        ml, alphas = list(ml), list(ones)
        for r in range(first_sub, nsub):
            s = st_in[:, r * sw:(r + 1) * sw] - ck
            if r == tri_sub:
                s = jnp.where(tri, s, NEG_INF)
            m, l = ml[r]
            m_new = jnp.maximum(m, jnp.max(s, axis=0, keepdims=True))
            alphas[r] = jnp.exp2(m - m_new)
            p = jnp.exp2(s - m_new)
            ml[r] = (m_new, alphas[r] * l + jnp.sum(p, axis=0, keepdims=True))
            pt_out[:, r * sw:(r + 1) * sw] = p.astype(BF16)
        return tuple(ml), tuple(alphas)

    bufs = ((st_a, pt_a), (st_b, pt_b))

    def step(kb, parity, alphas_prev, ml, first_sub=0, tri_sub=None, prev_first=0, next_first=0):
        (st_cur, pt_cur), (st_nxt, pt_prv) = bufs[parity], bufs[1 - parity]
        if next_first is not None:
            issue_scores(kb + 1, st_nxt, next_first)
        issue_pv(jnp.maximum(kb - 1, 0), pt_prv, alphas_prev, prev_first)
        return softmax(kb, st_cur, pt_cur, ml, first_sub, tri_sub)

    def body(i, carry):
        alphas, ml = carry
        ml, alphas = step(2 * i, 0, alphas, ml)
        ml, alphas = step(2 * i + 1, 1, alphas, ml)
        return alphas, ml

    acc_ref[...] = jnp.zeros_like(acc_ref)
    pt_b[...] = jnp.zeros_like(pt_b)
    issue_scores(0, st_a)
    n_main = qi * (tq // tk)
    init_ml = tuple((jnp.full((1, sw), NEG_INF, F32), jnp.zeros((1, sw), F32)) for _ in range(nsub))
    alphas, ml = lax.fori_loop(0, n_main // 2, body, (ones, init_ml))
    ndiag = tq // tk
    for d in range(ndiag):
        ml, alphas = step(n_main + d, d % 2, alphas, ml, first_sub=d, tri_sub=d, prev_first=max(d - 1, 0),
                          next_first=d + 1 if d + 1 < ndiag else None)
    issue_pv(n_main + ndiag - 1, bufs[(ndiag - 1) % 2][1], alphas, ndiag - 1)
    for r in range(nsub):
        o_ref[r * sw:(r + 1) * sw, :] = (acc_ref[r] / ml[r][1]).T.astype(o_ref.dtype)


def _fox(qkv, c, lp_real):
    bsz, lp = qkv.shape[:2]
    h, dh, w = FOX_HEADS, FOX_DH, BRANCH_WIDTH
    tq = FOX_TQ
    pos = jnp.arange(lp)
    ck = jnp.where(((pos >= N_PAD) & (pos < lp_real))[None, :, None], c, -NEG_INF) * LOG2E
    est = 2 * (2 * tq * dh * 2 + 2 * lp * dh * 2 + lp * LANE * 4) + (16 << 20)
    return pl.pallas_call(
        _fox_kernel,
        name="fox",
        grid=(bsz, h, lp // tq),
        in_specs=[pl.BlockSpec((None, tq, dh), lambda b, hh, i: (b, i, hh)),
                  pl.BlockSpec((None, lp, dh), lambda b, hh, i: (b, 0, h + hh)),
                  pl.BlockSpec((None, lp, dh), lambda b, hh, i: (b, 0, 2 * h + hh)),
                  pl.BlockSpec((None, lp, h), lambda b, hh, i: (b, 0, 0))],
        out_specs=pl.BlockSpec((None, tq, dh), lambda b, hh, i: (b, i, hh)),
        out_shape=jax.ShapeDtypeStruct((bsz, lp, w), BF16),
        scratch_shapes=[pltpu.VMEM((tq // FOX_SUB, dh, FOX_SUB), F32),
                        pltpu.VMEM((FOX_TK, tq), F32), pltpu.VMEM((FOX_TK, tq), F32),
                        pltpu.VMEM((FOX_TK, tq), BF16), pltpu.VMEM((FOX_TK, tq), BF16),
                        pltpu.VMEM((dh, tq), BF16)],
        compiler_params=_params(("parallel", "parallel", "parallel"), est),
    )(qkv, qkv, qkv, ck)


S5_CHUNK = 8
S5_SLAB = LANE // S5_GROUP
S5_NSLAB = S5_GROUPS // S5_SLAB


def _s5_tables(a_re, a_im, b_re, b_im, c_re, c_im, log_step):
    hp = lax.Precision.HIGHEST
    g, p, cg, n, sg, ns = S5_GROUPS, S5_STATE, S5_GROUP, S5_CHUNK, S5_SLAB, S5_NSLAB
    dt = jnp.exp(log_step)[:, None]
    mag = jnp.exp(a_re * dt)
    bar_re = mag * jnp.cos(a_im * dt)
    bar_im = mag * jnp.sin(a_im * dt)
    den = a_re * a_re + a_im * a_im
    num_re = bar_re - 1.0
    coef_re = (num_re * a_re + bar_im * a_im) / den
    coef_im = (bar_im * a_re - num_re * a_im) / den
    bb_re = coef_re[..., None] * b_re - coef_im[..., None] * b_im
    bb_im = coef_re[..., None] * b_im + coef_im[..., None] * b_re
    pw_re, pw_im = [jnp.ones_like(bar_re)], [jnp.zeros_like(bar_im)]
    for _ in range(n):
        pw_re, pw_im = (pw_re + [pw_re[-1] * bar_re - pw_im[-1] * bar_im],
                        pw_im + [pw_re[-1] * bar_im + pw_im[-1] * bar_re])
    eye = jnp.eye(sg, dtype=F32)

    w_re = jnp.stack([pw_re[n - 1 - s][..., None] * bb_re - pw_im[n - 1 - s][..., None] * bb_im for s in range(n)])
    w_im = jnp.stack([pw_re[n - 1 - s][..., None] * bb_im + pw_im[n - 1 - s][..., None] * bb_re for s in range(n)])
    gm = jnp.concatenate(
        [jnp.einsum('sjgpc,gh->jsgchp', x.reshape(n, ns, sg, p, cg), eye).reshape(ns, n, sg * cg, sg * p)
         for x in (w_re, w_im)], axis=-1)

    ca_re = jnp.stack([c_re * pw_re[tau][:, None, :] - c_im * pw_im[tau][:, None, :] for tau in range(n + 1)])
    ca_im = jnp.stack([c_re * pw_im[tau][:, None, :] + c_im * pw_re[tau][:, None, :] for tau in range(n + 1)])
    k = (jnp.einsum('tgcp,gpd->tgcd', ca_re[:n], bb_re, precision=hp)
         - jnp.einsum('tgcp,gpd->tgcd', ca_im[:n], bb_im, precision=hp))
    lag = jnp.arange(n)[None, :] - jnp.arange(n)[:, None]
    kst = jnp.where((lag >= 0)[:, :, None, None, None], k[jnp.clip(lag, 0, n - 1)], 0.0)
    mm = jnp.einsum('stjgcd,gh->jsgdthc', kst.reshape(n, n, ns, sg, cg, cg), eye).reshape(ns, n, sg * cg, n * LANE)

    pm = jnp.concatenate(
        [jnp.einsum('tjgcp,gh->jgpthc', x.reshape(n, ns, sg, cg, p), eye).reshape(ns, sg * p, n * LANE)
         for x in (ca_re[1:], -ca_im[1:])], axis=1)
    a8 = jnp.concatenate([pw_re[n].reshape(ns, sg * p), pw_im[n].reshape(ns, sg * p)], axis=1)
    return gm.astype(BF16), mm.astype(BF16), pm.astype(BF16), a8


def _chunk_rows(ref, s, nb):
    return ref[pl.ds(s, nb, stride=S5_CHUNK), :]


def _s5_in_kernel(u_ref, gm_ref, z_ref):
    nb = z_ref.shape[0]
    acc = None
    for s in range(S5_CHUNK):
        d = jnp.dot(_chunk_rows(u_ref, s, nb).astype(BF16), gm_ref[s], preferred_element_type=F32)
        acc = d if acc is None else acc + d
    z_ref[...] = acc


def _s5_state_kernel(z_ref, a_ref, h_ref, st_ref):
    @pl.when(pl.program_id(1) == 0)
    def _():
        st_ref[...] = jnp.zeros_like(st_ref)

    ns, nb, sw2 = z_ref.shape
    sw = sw2 // 2
    coef = [(a_ref[j:j + 1, :sw], a_ref[j:j + 1, sw:]) for j in range(ns)]

    def body(n, carry):
        out = []
        for j in range(ns):
            hr, hi = carry[j]
            h_ref[j, pl.ds(n, 1), :] = jnp.concatenate([hr, hi], axis=1)
            z = z_ref[j, pl.ds(n, 1), :]
            ar, ai = coef[j]
            out.append((ar * hr - ai * hi + z[:, :sw], ar * hi + ai * hr + z[:, sw:]))
        return tuple(out)

    init = tuple((st_ref[j:j + 1, :sw], st_ref[j:j + 1, sw:]) for j in range(ns))
    fin = lax.fori_loop(0, nb, body, init)
    for j in range(ns):
        st_ref[j:j + 1, :sw] = fin[j][0]
        st_ref[j:j + 1, sw:] = fin[j][1]


def _s5_out_kernel(u_ref, h_ref, mm_ref, pm_ref, d_ref, o_ref, y_ref):
    nb = h_ref.shape[0]
    acc = jnp.dot(h_ref[...].astype(BF16), pm_ref[...], preferred_element_type=F32)
    for s in range(S5_CHUNK):
        acc = acc + jnp.dot(_chunk_rows(u_ref, s, nb).astype(BF16), mm_ref[s], preferred_element_type=F32)
    for t in range(S5_CHUNK):
        y_ref[pl.ds(t, nb, stride=S5_CHUNK), :] = acc[:, t * LANE:(t + 1) * LANE]
    o_ref[...] = jax.nn.gelu(y_ref[...] + d_ref[...] * u_ref[...]).astype(o_ref.dtype)


def _s5_branch(proj, u_blk, a_re, a_im, b_re, b_im, c_re, c_im, d_skip, log_step, w_glu):
    bsz, lp = proj.shape[:2]
    w, n, ns = BRANCH_WIDTH, S5_CHUNK, S5_NSLAB
    sw2 = 2 * S5_SLAB * S5_STATE
    tb = _pick(lp, ROW_TILES)
    nb = tb // n
    nchunk = lp // n
    gm, mm, pm, a8 = _s5_tables(a_re, a_im, b_re, b_im, c_re, c_im, log_step)
    u_spec = pl.BlockSpec((None, tb, LANE), lambda b, j, i: (b, i, u_blk * ns + j))
    st_spec = pl.BlockSpec((None, None, nb, sw2), lambda b, j, i: (b, j, i, 0))
    z = pl.pallas_call(
        _s5_in_kernel,
        name="s5_in",
        grid=(bsz, ns, lp // tb),
        in_specs=[u_spec, pl.BlockSpec((None, n, LANE, sw2), lambda b, j, i: (j, 0, 0, 0))],
        out_specs=st_spec,
        out_shape=jax.ShapeDtypeStruct((bsz, ns, nchunk, sw2), F32),
        compiler_params=_params(("parallel", "parallel", "parallel"),
                                2 * (tb * LANE * 4 + n * LANE * sw2 * 2 + nb * sw2 * 4) + 2 * nb * sw2 * 4),
    )(proj, gm)
    nbb = _pick(nchunk, (96, 64, 32, 16, 8))
    blk = pl.BlockSpec((None, ns, nbb, sw2), lambda b, i: (b, 0, i, 0))
    h_start = pl.pallas_call(
        _s5_state_kernel,
        name="s5_state",
        grid=(bsz, nchunk // nbb),
        in_specs=[blk, pl.BlockSpec((ns, sw2), lambda b, i: (0, 0))],
        out_specs=blk,
        out_shape=jax.ShapeDtypeStruct((bsz, ns, nchunk, sw2), F32),
        scratch_shapes=[pltpu.VMEM((ns, sw2), F32)],
        compiler_params=_params(("parallel", "arbitrary"), 4 * ns * nbb * sw2 * 4),
    )(z, a8)
    y = pl.pallas_call(
        _s5_out_kernel,
        name="s5_out",
        grid=(bsz, ns, lp // tb),
        in_specs=[u_spec, st_spec,
                  pl.BlockSpec((None, n, LANE, n * LANE), lambda b, j, i: (j, 0, 0, 0)),
                  pl.BlockSpec((None, sw2, n * LANE), lambda b, j, i: (j, 0, 0)),
                  pl.BlockSpec((1, LANE), lambda b, j, i: (0, j))],
        out_specs=pl.BlockSpec((None, tb, LANE), lambda b, j, i: (b, i, j)),
        out_shape=jax.ShapeDtypeStruct((bsz, lp, w), BF16),
        scratch_shapes=[pltpu.VMEM((tb, LANE), F32)],
        compiler_params=_params(("parallel", "parallel", "parallel"),
                                2 * (tb * LANE * 4 + nb * sw2 * 4 + n * LANE * n * LANE * 2 + sw2 * n * LANE * 2
                                     + tb * LANE * 2) + 3 * nb * n * LANE * 4),
    )(proj, h_start, mm, pm, d_skip.reshape(1, w))
    return _glu(y.reshape(bsz * lp, w), w_glu.astype(BF16))


def _pad_cols(w, n):
    return jnp.pad(w, ((0, 0), (0, n - w.shape[1])))


def _scale_cols(x, s):
    return x * s


MOE_TM = 512


def _new_expert(te_ref, i):
    return (i == 0) | (te_ref[i] != te_ref[jnp.maximum(i - 1, 0)])


def _moe_up_kernel(te_ref, x_ref, wg_ref, wu_ref, s_ref, o_ref, wgb_ref, wub_ref, *, ni):
    i = pl.program_id(1)

    @pl.when(_new_expert(te_ref, i))
    def _():
        wgb_ref[...] = wg_ref[...].astype(BF16)
        wub_ref[...] = wu_ref[...].astype(BF16)

    @pl.when(i < te_ref[ni])
    def _():
        x = x_ref[...]
        g = jnp.dot(x, wgb_ref[...], preferred_element_type=F32)
        u = jnp.dot(x, wub_ref[...], preferred_element_type=F32)
        o_ref[...] = (g * jax.nn.sigmoid(g) * u * s_ref[...]).astype(o_ref.dtype)

    @pl.when(i >= te_ref[ni])
    def _():
        o_ref[...] = jnp.zeros_like(o_ref)


def _moe_down_kernel(te_ref, a_ref, w_ref, o_ref, wb_ref, *, ni):
    i = pl.program_id(1)

    @pl.when(_new_expert(te_ref, i))
    def _():
        wb_ref[...] = w_ref[...].astype(BF16)

    @pl.when(i < te_ref[ni])
    def _():
        o_ref[...] = jnp.dot(a_ref[...], wb_ref[...], preferred_element_type=F32).astype(o_ref.dtype)

    @pl.when(i >= te_ref[ni])
    def _():
        o_ref[...] = jnp.zeros_like(o_ref)


def _moe_experts(xs, row_scale, tile_info, layer, w_gate, w_up, w_down):
    r, d = xs.shape
    f = w_gate.shape[-1]
    tm = MOE_TM
    ni = r // tm
    tn = _pick(f, (512, 256, 128))
    wmap = lambda j, i, te: (layer, te[i], 0, j)
    act = pl.pallas_call(
        functools.partial(_moe_up_kernel, ni=ni),
        name="moe_up",
        grid_spec=pltpu.PrefetchScalarGridSpec(
            num_scalar_prefetch=1, grid=(f // tn, ni),
            in_specs=[pl.BlockSpec((tm, d), lambda j, i, te: (i, 0)),
                      pl.BlockSpec((None, None, d, tn), wmap), pl.BlockSpec((None, None, d, tn), wmap),
                      pl.BlockSpec((tm, 1), lambda j, i, te: (i, 0))],
            out_specs=pl.BlockSpec((tm, tn), lambda j, i, te: (i, j)),
            scratch_shapes=[pltpu.VMEM((d, tn), BF16), pltpu.VMEM((d, tn), BF16)]),
        out_shape=jax.ShapeDtypeStruct((r, f), BF16),
        compiler_params=_params(("arbitrary", "arbitrary"),
                                2 * (tm * d * 2 + 2 * d * tn * 4 + tm * tn * 2) + 2 * d * tn * 2 + 4 * tm * tn * 4),
    )(tile_info, xs, w_gate, w_up, row_scale)
    tn2 = _pick(d, (1024, 512, 256, 128))
    return pl.pallas_call(
        functools.partial(_moe_down_kernel, ni=ni),
        name="moe_down",
        grid_spec=pltpu.PrefetchScalarGridSpec(
            num_scalar_prefetch=1, grid=(d // tn2, ni),
            in_specs=[pl.BlockSpec((tm, f), lambda j, i, te: (i, 0)),
                      pl.BlockSpec((None, None, f, tn2), wmap)],
            out_specs=pl.BlockSpec((tm, tn2), lambda j, i, te: (i, j)),
            scratch_shapes=[pltpu.VMEM((f, tn2), BF16)]),
        out_shape=jax.ShapeDtypeStruct((r, d), F32),
        compiler_params=_params(("arbitrary", "arbitrary"),
                                2 * (tm * f * 2 + f * tn2 * 4 + tm * tn2 * 4) + f * tn2 * 2 + 2 * tm * tn2 * 4),
    )(tile_info, act, w_down)


def _moe_ffn(h32, hb, w_router, b_router, layer, w_gate, w_up, w_down):
    t, d = hb.shape
    e = w_gate.shape[1]
    tm = MOE_TM
    logits = _mm(h32, _pad_cols(w_router, LANE), precision=lax.Precision.HIGHEST,
                 name="router")[:, :N_EXPERTS] + b_router
    top_vals, top_idx = lax.top_k(logits, TOP_K)
    weights = jax.nn.softmax(top_vals, axis=-1)

    n_assign = t * TOP_K
    n_rows = n_assign + e * tm
    flat_e = top_idx.reshape(n_assign).astype(jnp.int32)
    onehot = jax.nn.one_hot(flat_e, e, dtype=jnp.int32)
    seen = jnp.cumsum(onehot, axis=0)
    counts = seen[-1]
    padded = (counts + tm - 1) // tm * tm
    group_end = jnp.cumsum(padded)
    group_start = group_end - padded
    sorted_start = jnp.cumsum(counts) - counts
    dest = group_start[flat_e] + jnp.sum(seen * onehot, axis=1) - 1
    order = jnp.argsort(flat_e, stable=True).astype(jnp.int32)
    rows = jnp.arange(n_rows, dtype=jnp.int32)
    row_e = jnp.minimum(jnp.searchsorted(group_end, rows, side="right"), e - 1).astype(jnp.int32)
    rank = rows - group_start[row_e]
    used = rank < counts[row_e]
    assign = order[jnp.where(used, sorted_start[row_e] + rank, 0)]
    src_tok = jnp.where(used, assign // TOP_K, 0)
    row_scale = jnp.where(used, weights.reshape(n_assign)[assign], 0.0)
    tile_info = jnp.concatenate([row_e[::tm], (group_end[-1:] // tm).astype(jnp.int32)])

    xs = jnp.take(hb, src_tok, axis=0)
    y = _moe_experts(xs, row_scale[:, None], tile_info, layer, w_gate, w_up, w_down)
    dest = dest.reshape(t, TOP_K)
    return [jnp.take(y, dest[:, s], axis=0) for s in range(TOP_K)]


def kernel(x, meta_tokens, w_in, b_forget, w_gate_down, w_gate_up, b_gate, w_branch, w_out, ln_mix_g, ln_mix_b, ln_ffn_g, ln_ffn_b, rwkv_mu, rwkv_w_rkv, rwkv_w0, rwkv_w1, rwkv_w2, rwkv_a0, rwkv_a1, rwkv_a2, rwkv_v0, rwkv_v1, rwkv_v2, rwkv_g1, rwkv_g2, rwkv_k_k, rwkv_k_a, rwkv_r_k, rwkv_ln_w, rwkv_ln_b, s5_a_re, s5_a_im, s5_b_re, s5_b_im, s5_c_re, s5_c_im, s5_d, s5_log_step, s5_w_glu, ffn_w_gate, ffn_w_up, ffn_w_down, moe_router, moe_router_b, moe_w_gate, moe_w_up, moe_w_down):
    bsz, seq, d = x.shape
    depth = w_in.shape[0]
    w = BRANCH_WIDTH
    lp_real = N_PAD + N_META + seq
    lp = -(-lp_real // FOX_TQ) * FOX_TQ
    t = bsz * lp
    meta = jnp.broadcast_to(meta_tokens.astype(x.dtype)[None], (bsz, N_META, d))
    h = jnp.concatenate([jnp.zeros((bsz, N_PAD, d), x.dtype), meta, x,
                         jnp.zeros((bsz, lp - lp_real, d), x.dtype)], axis=1)
    hb = h.astype(BF16)
    pos = jnp.arange(lp)
    real = ((pos >= N_PAD) & (pos < lp_real))[None, :, None]
    f_lo = 7 * w
    q_scale = jnp.concatenate([jnp.full((1, w), FOX_DH ** -0.5 * LOG2E, F32), jnp.ones((1, 2 * w), F32)], axis=1)
    ret_consts = _ret_consts(lp)
    v_first = None
    for layer in range(depth):
        wl = w_in[layer]
        w_f32 = jnp.concatenate([wl[:, :4 * w], wl[:, f_lo + FOX_HEADS:]], axis=1).astype(BF16)
        w_fox = wl[:, 4 * w:f_lo].astype(BF16)
        w_small = _pad_cols(jnp.concatenate([w_gate_down[layer], wl[:, f_lo:f_lo + FOX_HEADS]], axis=1),
                            GATE_RANK + LANE).astype(BF16)
        hb2 = hb.reshape(t, d)
        proj = _mm(hb2, w_f32, name="proj").reshape(bsz, lp, 5 * w)
        qkv = _mm(hb2, w_fox, out_dtype=BF16, ep=_scale_cols, extras=((q_scale, "row", 0),),
                  name="proj_fox").reshape(bsz, lp, 3 * w)
        small = _mm(hb2, w_small, name="proj_small")
        f_logit = small[:, GATE_RANK:GATE_RANK + FOX_HEADS].reshape(bsz, lp, FOX_HEADS)

        o_ret = _retention(proj, ret_consts)

        vres = None if layer == 0 else (rwkv_v0[layer - 1], rwkv_v1[layer - 1], rwkv_v2[layer - 1])
        r, lw, k, v, a, b, gate, bonus = _rwkv_pre(
            proj, 3, v_first, vres, rwkv_mu[layer], rwkv_w_rkv[layer], rwkv_w0[layer], rwkv_w1[layer],
            rwkv_w2[layer], rwkv_a0[layer], rwkv_a1[layer], rwkv_a2[layer], rwkv_g1[layer], rwkv_g2[layer],
            rwkv_k_k[layer], rwkv_k_a[layer], rwkv_r_k[layer])
        if layer == 0:
            v_first = v
        o_rwkv = _wkv7(r, lw, k, v, a, b, gate, bonus, rwkv_ln_w[layer], rwkv_ln_b[layer])

        log_f = jnp.where(real, jax.nn.log_sigmoid(f_logit + b_forget[layer]), 0.0)
        c = jnp.cumsum(log_f, axis=1)
        o_fox = _fox(qkv, c, lp_real)

        o_s5 = _s5_branch(proj, 4, s5_a_re[layer], s5_a_im[layer], s5_b_re[layer], s5_b_im[layer],
                          s5_c_re[layer], s5_c_im[layer], s5_d[layer], s5_log_step[layer], s5_w_glu[layer])

        merged = _merge([o_ret.reshape(t, w), o_rwkv.reshape(t, w), o_fox.reshape(t, w), o_s5],
                        w_branch[layer].astype(BF16), small, w_gate_up[layer].astype(BF16), b_gate[layer])
        mix = _mm(merged, w_out, b_layer=layer, name="w_out")
        h, hb = _res_ln([mix], h, ln_mix_g[layer], ln_mix_b[layer], mask_pad=None)

        j = layer // 2
        hb2 = hb.reshape(t, d)
        if layer % 2 == 0:
            act = _swiglu_up(hb2, ffn_w_gate[j].astype(BF16), ffn_w_up[j].astype(BF16))
            ffn = [_mm(act, ffn_w_down[j].astype(BF16), name="ffn_down")]
        else:
            ffn = _moe_ffn(h.reshape(t, d), hb2, moe_router[j], moe_router_b[j], j,
                           moe_w_gate, moe_w_up, moe_w_down)
        h, hb = _res_ln(ffn, h, ln_ffn_g[layer], ln_ffn_b[layer], mask_pad=(N_PAD, lp_real))
    return h[:, N_PAD + N_META:lp_real]
```

```python
import functools
import math

import jax
import jax.numpy as jnp
from jax import lax
from jax.experimental import pallas as pl
from jax.experimental.pallas import tpu as pltpu

F32 = jnp.float32
BF16 = jnp.bfloat16

D_MODEL = 4096
DEPTH = 4
N_META = 16
BLOCK = 128
N_PAD = (-N_META) % BLOCK
BRANCH_WIDTH = D_MODEL // 4
GATE_RANK = 256
RET_DK = 64
RET_DV = 128
RET_HEADS = BRANCH_WIDTH // RET_DV
RWKV_HEAD = 64
RWKV_HEADS = BRANCH_WIDTH // RWKV_HEAD
RWKV_GN_EPS = 64e-5
FOX_DH = 128
FOX_HEADS = BRANCH_WIDTH // FOX_DH
S5_GROUP = 16
S5_GROUPS = BRANCH_WIDTH // S5_GROUP
S5_STATE = 64
N_EXPERTS = 8
TOP_K = 2
ALPHA = (2.0 * DEPTH) ** 0.25
LN_EPS = 1e-5
NEG_INF = -1e30
LOG2E = math.log2(math.e)

V7X_VMEM_BUDGET = 58 * 1024 * 1024
LANE = 128
RWKV_CHUNK = 64
RWKV_QUAD = 4
RWKV_QW = RWKV_QUAD * RWKV_HEAD
FOX_TK = 128
FOX_SUB = 128
FOX_TQ = 768
ROW_TILES = (1408, 1536, 1280, 768, 640, 512, 256, 128)


def _pick(dim, candidates):
    for c in candidates:
        if dim % c == 0:
            return c
    return dim


def _params(sem, est_bytes):
    limit = int(min(V7X_VMEM_BUDGET, max(est_bytes * 5 // 4 + (2 << 20), 16 << 20)))
    return pltpu.CompilerParams(dimension_semantics=sem, vmem_limit_bytes=limit)


def _nbytes(shape, dtype):
    n = jnp.dtype(dtype).itemsize
    for s in shape:
        if s is not None:
            n *= s
    return n


def _mmf_kernel(*refs, nb, nx, nk, ep, precision):
    a_ref = refs[0]
    b_refs = refs[1:1 + nb]
    x_refs = refs[1 + nb:1 + nb + nx]
    o_ref = refs[1 + nb + nx]
    acc_refs = refs[2 + nb + nx:]
    a = a_ref[...]
    if precision is None and a.dtype != BF16:
        a = a.astype(BF16)
    def weight(b_ref):
        b = b_ref[...]
        return b.astype(BF16) if precision is None and b.dtype != BF16 else b

    dots = [jnp.dot(a, weight(b), preferred_element_type=F32, precision=precision) for b in b_refs]

    def finish(vals):
        o_ref[...] = ep(*vals, *[x[...] for x in x_refs]).astype(o_ref.dtype)

    if nk == 1:
        finish(dots)
        return
    k = pl.program_id(2)

    @pl.when(k == 0)
    def _():
        for acc, d in zip(acc_refs, dots):
            acc[...] = d

    @pl.when(k > 0)
    def _():
        for acc, d in zip(acc_refs, dots):
            acc[...] += d

    @pl.when(k == nk - 1)
    def _():
        finish([acc[...] for acc in acc_refs])


def _mmf(a_spec, b_specs, x_specs, ep, *, m, n, kd, tm, tn, tk, out_dtype, precision=None, name="mm"):
    nk = kd // tk
    specs = [a_spec] + list(b_specs) + list(x_specs)
    est = 2 * sum(_nbytes(s[1], s[0].dtype) for s in specs) + 2 * tm * tn * jnp.dtype(out_dtype).itemsize
    est += (len(b_specs) + 2) * tm * tn * 4
    scratch = [pltpu.VMEM((tm, tn), F32) for _ in b_specs] if nk > 1 else []
    return pl.pallas_call(
        functools.partial(_mmf_kernel, nb=len(b_specs), nx=len(x_specs), nk=nk, ep=ep, precision=precision),
        name=name,
        grid=(m // tm, n // tn, nk),
        in_specs=[pl.BlockSpec(s[1], s[2]) for s in specs],
        out_specs=pl.BlockSpec((tm, tn), lambda i, j, k: (i, j)),
        out_shape=jax.ShapeDtypeStruct((m, n), out_dtype),
        scratch_shapes=scratch,
        compiler_params=_params(("parallel", "parallel", "arbitrary"), est),
    )(*[s[0] for s in specs])


def _ident(x):
    return x


def _mm(a, b, out_dtype=F32, precision=None, ep=_ident, extras=(), a_blk=0, b_layer=None, name="mm"):
    m = a.shape[0]
    kd, n = b.shape[-2:]
    tm = _pick(m, ROW_TILES)
    tn = _pick(n, (1024, 512, 384, 256, 128))
    tk = kd if kd <= 4096 else _pick(kd, (2048, 1536, 1024, 512))
    if precision is not None:
        tm = _pick(m, (640, 512, 256, 128))
        tn = _pick(n, (512, 384, 256, 128))
        tk = kd if kd <= 2048 else _pick(kd, (2048, 1024, 512))
    nk = kd // tk
    ob = jnp.dtype(out_dtype).itemsize

    def need(tn_):
        return (2 * (tm * tk * a.dtype.itemsize + tk * tn_ * b.dtype.itemsize + tm * tn_ * ob)
                + (3 if nk > 1 else 2) * tm * tn_ * 4)

    while need(tn) > V7X_VMEM_BUDGET - (6 << 20) and tn % 256 == 0 and n % (tn // 2) == 0:
        tn //= 2
    a_spec = (a, (tm, tk), lambda i, j, k: (i, a_blk * nk + k))
    if b_layer is None:
        b_spec = (b, (tk, tn), lambda i, j, k: (k, j))
    else:
        b_spec = (b, (None, tk, tn), lambda i, j, k: (b_layer, k, j))
    x_specs = []
    for arr, kind, off in extras:
        if kind == "tile":
            x_specs.append((arr, (tm, tn), lambda i, j, k, off=off: (i, j + off)))
        else:
            x_specs.append((arr, (1, tn), lambda i, j, k, off=off: (0, j + off)))
    return _mmf(a_spec, [b_spec], x_specs, ep, m=m, n=n, kd=kd, tm=tm, tn=tn, tk=tk,
                out_dtype=out_dtype, precision=precision, name=name)


def _silu_mul(g, u):
    return g * jax.nn.sigmoid(g) * u


def _swiglu_up(h, wg, wu):
    m, d = h.shape
    f = wg.shape[1]
    tm = _pick(m, ROW_TILES)
    tn = _pick(f, (512, 256, 128))
    wmap = lambda i, j, k: (0, j)
    return _mmf((h, (tm, d), lambda i, j, k: (i, 0)),
                [(wg, (d, tn), wmap), (wu, (d, tn), wmap)], [], _silu_mul,
                m=m, n=f, kd=d, tm=tm, tn=tn, tk=d, out_dtype=BF16, name="swiglu_up")


def _glu_ep(val, gt):
    return val * jax.nn.sigmoid(gt)


def _glu(y, w_glu):
    m, kd = y.shape
    n = w_glu.shape[1] // 2
    tm = _pick(m, ROW_TILES)
    tn = _pick(n, (512, 256, 128))
    nj = n // tn
    return _mmf((y, (tm, kd), lambda i, j, k: (i, 0)),
                [(w_glu, (kd, tn), lambda i, j, k: (0, j)), (w_glu, (kd, tn), lambda i, j, k: (0, nj + j))],
                [], _glu_ep, m=m, n=n, kd=kd, tm=tm, tn=tn, tk=kd, out_dtype=BF16, name="s5_glu")


def _res_ln_kernel(*refs, ny, tm, mask_pad):
    y_refs = refs[:ny]
    r_ref, g_ref, b_ref, o_ref, ob_ref = refs[ny:]
    z = ALPHA * r_ref[...] + y_refs[0][...]
    for y_ref in y_refs[1:]:
        z = z + y_ref[...]
    mu = jnp.mean(z, axis=-1, keepdims=True)
    zc = z - mu
    var = jnp.mean(zc * zc, axis=-1, keepdims=True)
    out = zc * lax.rsqrt(var + LN_EPS) * g_ref[...] + b_ref[...]
    if mask_pad is not None:
        pos = pl.program_id(1) * tm + lax.broadcasted_iota(jnp.int32, (tm, 1), 0)
        out = jnp.where((pos >= mask_pad[0]) & (pos < mask_pad[1]), out, 0.0)
    o_ref[...] = out
    ob_ref[...] = out.astype(BF16)


def _res_ln(ys, res, g, b, mask_pad):
    bsz, lp, d = res.shape
    tm = _pick(lp, (256, 128, 64, 32, 16, 8))
    est = 2 * ((4 + len(ys)) * tm * d * 4) + 4 * tm * d * 4
    spec = pl.BlockSpec((None, tm, d), lambda bi, i: (bi, i, 0))
    vspec = pl.BlockSpec((1, d), lambda bi, i: (0, 0))
    return pl.pallas_call(
        functools.partial(_res_ln_kernel, ny=len(ys), tm=tm, mask_pad=mask_pad),
        name="res_ln",
        grid=(bsz, lp // tm),
        in_specs=[spec] * (len(ys) + 1) + [vspec, vspec],
        out_specs=[spec, spec],
        out_shape=[jax.ShapeDtypeStruct((bsz, lp, d), F32), jax.ShapeDtypeStruct((bsz, lp, d), BF16)],
        compiler_params=_params(("parallel", "parallel"), est),
    )(*[y.reshape(bsz, lp, d) for y in ys], res, g.reshape(1, d), b.reshape(1, d))


def _merge_kernel(o0, o1, o2, o3, wb_ref, glr_ref, wgu_ref, bg_ref, out_ref):
    glr = glr_ref[...].astype(BF16)
    acc = None
    for i, o_ref in enumerate((o0, o1, o2, o3)):
        gate = jax.nn.sigmoid(jnp.dot(glr, wgu_ref[i], preferred_element_type=F32) + bg_ref[i])
        term = gate * jnp.dot(o_ref[...], wb_ref[i], preferred_element_type=F32)
        acc = term if acc is None else acc + term
    out_ref[...] = acc.astype(out_ref.dtype)


def _merge(outs, w_branch, small, w_gate_up, b_gate):
    m, w = outs[0].shape
    nb, r, d = w_gate_up.shape
    tm = _pick(m, ROW_TILES)
    tn = _pick(d, (512, 256, 128))
    est = 2 * (nb * tm * w * 2 + nb * w * tn * 2 + tm * r * 4 + nb * r * tn * 2 + tm * tn * 2) + 4 * tm * tn * 4
    ospec = pl.BlockSpec((tm, w), lambda i, j: (i, 0))
    return pl.pallas_call(
        _merge_kernel,
        name="merge",
        grid=(m // tm, d // tn),
        in_specs=[ospec, ospec, ospec, ospec,
                  pl.BlockSpec((nb, w, tn), lambda i, j: (0, 0, j)),
                  pl.BlockSpec((tm, r), lambda i, j: (i, 0)),
                  pl.BlockSpec((nb, r, tn), lambda i, j: (0, 0, j)),
                  pl.BlockSpec((nb, 1, tn), lambda i, j: (0, 0, j))],
        out_specs=pl.BlockSpec((tm, tn), lambda i, j: (i, j)),
        out_shape=jax.ShapeDtypeStruct((m, d), BF16),
        compiler_params=_params(("parallel", "parallel"), est),
    )(*outs, w_branch, small, w_gate_up, b_gate.reshape(nb, 1, d))


def _ret_consts(lp):
    f32 = F32
    half = RET_DK // 2
    inv = 1.0 / (10000.0 ** jnp.linspace(0.0, 1.0, half, dtype=f32))
    ang = jnp.arange(lp, dtype=f32)[:, None] * inv[None, :]
    cos, sin = jnp.cos(ang), jnp.sin(ang)
    cos_t = jnp.tile(jnp.concatenate([cos, cos], axis=-1), (1, RET_HEADS))
    sin_t = jnp.tile(jnp.concatenate([-sin, sin], axis=-1), (1, RET_HEADS))
    log_gamma = jnp.log(1.0 - 2.0 ** (-5.0 - jnp.arange(RET_HEADS, dtype=f32)))
    pos = jnp.arange(BLOCK, dtype=f32)
    rel = pos[:, None] - pos[None, :]
    dmat = jnp.where(rel >= 0, jnp.exp(log_gamma[:, None, None] * jnp.maximum(rel, 0.0)), 0.0)
    k_decay = jnp.exp(log_gamma[:, None] * (BLOCK - 1.0 - pos)[None, :])
    q_decay = jnp.exp(log_gamma[:, None] * (pos + 1.0)[None, :])
    kd_t = jnp.repeat(k_decay.T, RET_DK, axis=1)
    qd_t = jnp.repeat(q_decay.T, RET_DK, axis=1)
    cd = jnp.exp(log_gamma * BLOCK)
    cd_t = jnp.broadcast_to(cd[:, None, None], (RET_HEADS, 1, RET_DV))
    return cos_t, sin_t, dmat, qd_t, kd_t, cd_t


def _ret_kernel(q_ref, k_ref, v_ref, g_ref, cos_ref, sin_ref, dmat_ref, qd_ref, kd_ref, cd_ref,
                o_ref, st_ref):
    @pl.when(pl.program_id(1) == 0)
    def _():
        st_ref[...] = jnp.zeros_like(st_ref)

    cos = cos_ref[...]
    sin = sin_ref[...]
    width = RET_HEADS * RET_DK
    lane = lax.broadcasted_iota(jnp.int32, (BLOCK, width), 1)
    first = (lane % RET_DK) < (RET_DK // 2)

    def rot(x):
        sw = jnp.where(first, pltpu.roll(x, width - RET_DK // 2, 1), pltpu.roll(x, RET_DK // 2, 1))
        return x * cos + sw * sin

    q = rot(q_ref[...])
    k = rot(k_ref[...]) * RET_DK ** -0.5
    qb = q.astype(BF16)
    kb = k.astype(BF16)
    qs = (q * qd_ref[...]).astype(BF16)
    ks = (k * kd_ref[...]).astype(BF16)
    for h in range(RET_HEADS):
        ksl = slice(h * RET_DK, (h + 1) * RET_DK)
        vsl = slice(h * RET_DV, (h + 1) * RET_DV)
        vh = v_ref[:, vsl].astype(BF16)
        s = lax.dot_general(qb[:, ksl], kb[:, ksl], (((1,), (1,)), ((), ())),
                            preferred_element_type=F32) * dmat_ref[h]
        st = st_ref[h]
        y = (jnp.dot(s.astype(BF16), vh, preferred_element_type=F32)
             + jnp.dot(qs[:, ksl], st.astype(BF16), preferred_element_type=F32))
        st_ref[h] = st * cd_ref[h] + lax.dot_general(ks[:, ksl], vh, (((0,), (0,)), ((), ())),
                                                     preferred_element_type=F32)
        mu = jnp.mean(y, axis=-1, keepdims=True)
        yc = y - mu
        var = jnp.mean(yc * yc, axis=-1, keepdims=True)
        gh = g_ref[:, vsl]
        o_ref[:, vsl] = (gh * jax.nn.sigmoid(gh) * yc * lax.rsqrt(var + 1e-6)).astype(o_ref.dtype)


def _retention(proj, consts):
    bsz, lp = proj.shape[:2]
    kw = RET_HEADS * RET_DK
    cos_t, sin_t, dmat, qd_t, kd_t, cd_t = consts
    est = 2 * BLOCK * 4 * (2 * kw + 2 * BRANCH_WIDTH + 4 * kw) + 2 * RET_HEADS * BLOCK * BLOCK * 4 + (8 << 20)
    tab = pl.BlockSpec((BLOCK, kw), lambda b, n: (n, 0))
    ctab = pl.BlockSpec((BLOCK, kw), lambda b, n: (0, 0))
    return pl.pallas_call(
        _ret_kernel,
        name="retention",
        grid=(bsz, lp // BLOCK),
        in_specs=[pl.BlockSpec((None, BLOCK, kw), lambda b, n: (b, n, 0)),
                  pl.BlockSpec((None, BLOCK, kw), lambda b, n: (b, n, 1)),
                  pl.BlockSpec((None, BLOCK, BRANCH_WIDTH), lambda b, n: (b, n, 1)),
                  pl.BlockSpec((None, BLOCK, BRANCH_WIDTH), lambda b, n: (b, n, 2)),
                  tab, tab,
                  pl.BlockSpec((RET_HEADS, BLOCK, BLOCK), lambda b, n: (0, 0, 0)),
                  ctab, ctab,
                  pl.BlockSpec((RET_HEADS, 1, RET_DV), lambda b, n: (0, 0, 0))],
        out_specs=pl.BlockSpec((None, BLOCK, BRANCH_WIDTH), lambda b, n: (b, n, 0)),
        out_shape=jax.ShapeDtypeStruct((bsz, lp, BRANCH_WIDTH), BF16),
        scratch_shapes=[pltpu.VMEM((RET_HEADS, RET_DK, RET_DV), F32)],
        compiler_params=_params(("parallel", "arbitrary"), est),
    )(proj, proj, proj, proj, cos_t, sin_t, dmat, qd_t, kd_t, cd_t)


def _split_dot(x, ones):
    hi = x.astype(BF16)
    lo = (x - hi.astype(F32)).astype(BF16)
    return (jnp.dot(hi, ones, preferred_element_type=F32) + jnp.dot(lo, ones, preferred_element_type=F32))


_RV_MU, _RV_W0, _RV_A0, _RV_V0, _RV_KK, _RV_KA, _RV_RK = 0, 6, 7, 8, 9, 10, 11
_RV_ROWS = 16


def _rwkv_pre_kernel(*refs, has_vres):
    (u_ref, up_ref, wr_ref, wk_ref, wv_ref, w1_ref, w2_ref, a1_ref, a2_ref, g1_ref, g2_ref,
     ones_ref, vec_ref) = refs[:13]
    pos = 13
    if has_vres:
        vf_ref, v1_ref, v2_ref = refs[13:16]
        pos = 16
    r_out, lw_out, k_out, v_out, a_out, b_out, gate_out, bonus_out = refs[pos:pos + 8]

    def vec(i):
        return vec_ref[i:i + 1, :]

    u = u_ref[...]
    tm = u.shape[0]
    prev_last = jnp.where(pl.program_id(1) == 0, 0.0, up_ref[7:8, :])
    row = lax.broadcasted_iota(jnp.int32, (tm, 1), 0)
    prev = jnp.where(row == 0, prev_last, pltpu.roll(u, 1, 0))
    xx = prev - u

    def mix(i):
        return (u + xx * vec(_RV_MU + i)).astype(BF16)

    def mm(x, w_ref):
        return jnp.dot(x, w_ref[...], preferred_element_type=F32)

    xr, xw, xk, xv, xa, xg = [mix(i) for i in range(6)]
    r = mm(xr, wr_ref)
    k = mm(xk, wk_ref)
    v = mm(xv, wv_ref)
    z = -(vec(_RV_W0) + mm(jnp.tanh(mm(xw, w1_ref)).astype(BF16), w2_ref))
    softplus = jnp.maximum(z, 0.0) + jnp.log(1.0 + jnp.exp(-jnp.abs(z)))
    lw_out[...] = -jnp.exp(-softplus - 0.5)
    if has_vres:
        mv = jax.nn.sigmoid(vec(_RV_V0) + mm(mm(xv, v1_ref).astype(BF16), v2_ref))
        v = v + (vf_ref[...] - v) * mv
    a = jax.nn.sigmoid(vec(_RV_A0) + mm(mm(xa, a1_ref).astype(BF16), a2_ref))
    gate_out[...] = mm(jax.nn.sigmoid(mm(xg, g1_ref)).astype(BF16), g2_ref)

    ones = ones_ref[...]
    kk = k * vec(_RV_KK)
    kk = kk / jnp.maximum(jnp.sqrt(_split_dot(kk * kk, ones)), 1e-12)
    k = k * (1.0 + (a - 1.0) * vec(_RV_KA))
    r_out[...] = r
    k_out[...] = k
    v_out[...] = v
    a_out[...] = -kk
    b_out[...] = kk * a
    bonus_out[...] = _split_dot(r * k * vec(_RV_RK), ones) * v


def _rwkv_pre(proj, u_blk, v_first, vres, mu, w_rkv, w0, w1, w2, a0, a1, a2, g1, g2, k_k, k_a, r_k):
    bsz, lp = proj.shape[:2]
    w = BRANCH_WIDTH
    tm = _pick(lp, (256, 128, 64, 32, 16, 8))
    has_vres = vres is not None

    def lora_pair(wa, wb):
        rp = -(-wa.shape[1] // LANE) * LANE
        return (jnp.pad(wa, ((0, 0), (0, rp - wa.shape[1]))).astype(BF16),
                jnp.pad(wb, ((0, rp - wb.shape[0]), (0, 0))).astype(BF16))

    w1p, w2p = lora_pair(w1, w2)
    a1p, a2p = lora_pair(a1, a2)
    g1p, g2p = lora_pair(g1, g2)
    hid = jnp.arange(w) // RWKV_HEAD
    ones = (hid[:, None] == hid[None, :]).astype(BF16)
    zero = jnp.zeros((w,), F32)
    rows = [mu[i] for i in range(6)] + [w0, a0, vres[0] if has_vres else zero, k_k, k_a, r_k.reshape(w)]
    vec = jnp.stack(rows + [zero] * (_RV_ROWS - len(rows)))

    tile = pl.BlockSpec((None, tm, w), lambda b, i: (b, i, 0))
    const = lambda arr: pl.BlockSpec(arr.shape, lambda b, i: (0,) * arr.ndim)
    weights = [w_rkv[0].astype(BF16), w_rkv[1].astype(BF16), w_rkv[2].astype(BF16),
               w1p, w2p, a1p, a2p, g1p, g2p, ones, vec]
    args = [proj, proj] + weights
    in_specs = [pl.BlockSpec((None, tm, w), lambda b, i: (b, i, u_blk)),
                pl.BlockSpec((None, 8, w), lambda b, i: (b, jnp.maximum(i * (tm // 8) - 1, 0), u_blk))]
    in_specs += [const(x) for x in weights]
    if has_vres:
        v1p, v2p = lora_pair(vres[1], vres[2])
        args += [v_first, v1p, v2p]
        in_specs += [tile, const(v1p), const(v2p)]
    est = 2 * sum(x.size * x.dtype.itemsize for x in weights) + 2 * 11 * tm * w * 4 + 40 * tm * w * 4
    outs = pl.pallas_call(
        functools.partial(_rwkv_pre_kernel, has_vres=has_vres),
        name="rwkv_pre",
        grid=(bsz, lp // tm),
        in_specs=in_specs,
        out_specs=[tile] * 8,
        out_shape=[jax.ShapeDtypeStruct((bsz, lp, w), F32)] * 8,
        compiler_params=_params(("parallel", "parallel"), est),
    )(*args)
    return outs


def _wkv_kernel(r_ref, lw_ref, k_ref, v_ref, a_ref, b_ref, gate_ref, bonus_ref, lnw_ref, lnb_ref,
                o_ref, ht_ref):
    c = RWKV_CHUNK
    wd = RWKV_QW

    @pl.when(pl.program_id(1) == 0)
    def _():
        ht_ref[...] = jnp.zeros_like(ht_ref)

    ti = lax.broadcasted_iota(jnp.int32, (c, c), 0)
    tj = lax.broadcasted_iota(jnp.int32, (c, c), 1)
    cum_all = jnp.dot((ti >= tj).astype(F32), lw_ref[...], preferred_element_type=F32,
                      precision=lax.Precision.HIGHEST)

    ri = lax.broadcasted_iota(jnp.int32, (wd, wd), 0)
    ci = lax.broadcasted_iota(jnp.int32, (wd, wd), 1)
    same = (ri // c) == (ci // RWKV_HEAD)
    strict = same & ((ci % c) < (ri % c))
    incl = same & ((ci % c) <= (ri % c))
    ones_blk = jnp.where(same, 1.0, 0.0).astype(BF16)

    def expand(x):
        xb = x.astype(BF16)
        return jnp.where(same, jnp.concatenate([xb] * RWKV_QUAD, axis=0), jnp.zeros((), BF16))

    def nt(x, y):
        return lax.dot_general(x, y, (((1,), (1,)), ((), ())), preferred_element_type=F32)

    def tn(x, y):
        return lax.dot_general(x, y, (((0,), (0,)), ((), ())), preferred_element_type=F32)

    def mm(x, y):
        return jnp.dot(x, y, preferred_element_type=F32)

    groups = range(r_ref.shape[1] // wd)
    sls = [slice(q * wd, (q + 1) * wd) for q in groups]
    ar, bt, kt, vb, bw, kw, w_last = [], [], [], [], [], [], []
    for sl in sls:
        lw = lw_ref[:, sl]
        cum = cum_all[:, sl]
        cum_last = cum[c - 1:c, :]
        e_neg = jnp.exp(-cum)
        e_rem = jnp.exp(cum_last - cum)
        k = k_ref[:, sl]
        b = b_ref[:, sl]
        ar.append(jnp.concatenate([expand(a_ref[:, sl] * jnp.exp(cum - lw)),
                                   expand(r_ref[:, sl] * jnp.exp(cum))], axis=0))
        bt.append(expand(b * e_neg))
        kt.append(expand(k * e_neg))
        vb.append(expand(v_ref[:, sl]))
        bw.append(expand(b * e_rem))
        kw.append(expand(k * e_rem))
        w_last.append(jnp.exp(cum_last))

    g_b = [nt(ar[q], bt[q]) for q in groups]
    g_k = [nt(ar[q], kt[q]) for q in groups]
    hts = [ht_ref[q] for q in groups]
    g_h = [nt(ar[q], hts[q].astype(BF16)) for q in groups]
    p = [jnp.where(strict, g_b[q][:wd], 0.0) for q in groups]
    a_ak = [jnp.where(strict, g_k[q][:wd], 0.0).astype(BF16) for q in groups]
    a_rb = [jnp.where(incl, g_b[q][wd:], 0.0).astype(BF16) for q in groups]
    a_rk = [jnp.where(incl, g_k[q][wd:], 0.0).astype(BF16) for q in groups]
    u = [g_h[q][:wd] + mm(a_ak[q], vb[q]) for q in groups]
    steps = int(math.log2(c))
    for i in range(steps):
        pb = [x.astype(BF16) for x in p]
        u = [u[q] + mm(pb[q], u[q].astype(BF16)) for q in groups]
        if i < steps - 1:
            p = [mm(pb[q], pb[q]) for q in groups]
    ub = [x.astype(BF16) for x in u]
    y_bd = [g_h[q][wd:] + mm(a_rb[q], ub[q]) + mm(a_rk[q], vb[q]) for q in groups]
    for q in groups:
        ht_ref[q] = hts[q] * w_last[q] + tn(ub[q], bw[q]) + tn(vb[q], kw[q])
    for q in groups:
        sl = sls[q]
        y = y_bd[q][0:c]
        for h in range(1, RWKV_QUAD):
            y = y + y_bd[q][h * c:(h + 1) * c]
        mean = _split_dot(y, ones_blk) * (1.0 / RWKV_HEAD)
        yc = y - mean
        var = _split_dot(yc * yc, ones_blk) * (1.0 / RWKV_HEAD)
        out = yc * lax.rsqrt(var + RWKV_GN_EPS) * lnw_ref[:, sl] + lnb_ref[:, sl] + bonus_ref[:, sl]
        o_ref[:, sl] = (out * gate_ref[:, sl]).astype(o_ref.dtype)


def _wkv7(r, lw, k, v, a, b, gate, bonus, ln_w, ln_b):
    bsz, lp, w = r.shape
    c = RWKV_CHUNK
    spec = pl.BlockSpec((None, c, w), lambda bi, n: (bi, n, 0))
    vspec = pl.BlockSpec((1, w), lambda bi, n: (0, 0))
    est = 2 * 9 * c * w * 4 + 4 * 40 * RWKV_QW * RWKV_QW * 4
    return pl.pallas_call(
        _wkv_kernel,
        name="wkv7",
        grid=(bsz, lp // c),
        in_specs=[spec] * 8 + [vspec, vspec],
        out_specs=spec,
        out_shape=jax.ShapeDtypeStruct((bsz, lp, w), BF16),
        scratch_shapes=[pltpu.VMEM((w // RWKV_QW, RWKV_QW, RWKV_QW), F32)],
        compiler_params=_params(("parallel", "arbitrary"), est),
    )(r, lw, k, v, a, b, gate, bonus, ln_w.reshape(1, w), ln_b.reshape(1, w))


def _fox_kernel(q_ref, k_ref, v_ref, ck_ref, o_ref, acc_ref, st_a, st_b, pt_a, pt_b, qt_ref):
    tk, sw = FOX_TK, FOX_SUB
    tq = q_ref.shape[0]
    nsub = tq // sw
    qi = pl.program_id(2)
    qt_ref[...] = q_ref[...].astype(F32).T.astype(BF16)
    head_lane = lax.broadcasted_iota(jnp.int32, (tk, FOX_HEADS), 1) == pl.program_id(1)
    tri = (lax.broadcasted_iota(jnp.int32, (tk, sw), 0) <= lax.broadcasted_iota(jnp.int32, (tk, sw), 1))
    ones = tuple(jnp.ones((1, sw), F32) for _ in range(nsub))

    def issue_scores(kb, st_out, first_sub=0):
        ks = pl.multiple_of(kb * tk, tk)
        st_out[:, first_sub * sw:] = jnp.dot(k_ref[pl.ds(ks, tk), :], qt_ref[:, first_sub * sw:],
                                             preferred_element_type=F32)

    def issue_pv(kb, pt_in, alphas, first_sub=0):
        ks = pl.multiple_of(kb * tk, tk)
        pv = lax.dot_general(v_ref[pl.ds(ks, tk), :], pt_in[:, first_sub * sw:], (((0,), (0,)), ((), ())),
                             preferred_element_type=F32)
        for r in range(first_sub, nsub):
            acc_ref[r] = alphas[r] * acc_ref[r] + pv[:, (r - first_sub) * sw:(r - first_sub + 1) * sw]

    def softmax(kb, st_in, pt_out, ml, first_sub=0, tri_sub=None):
        ck_heads = ck_ref[pl.ds(pl.multiple_of(kb * tk, tk), tk), :]
        ck = jnp.sum(jnp.where(head_lane, ck_heads, 0.0), axis=1, keepdims=True)
        ck = jnp.broadcast_to(ck, (tk, sw))
        ml, alphas = list(ml), list(ones)
        for r in range(first_sub, nsub):
            s = st_in[:, r * sw:(r + 1) * sw] - ck
            if r == tri_sub:
                s = jnp.where(tri, s, NEG_INF)
            m, l = ml[r]
            m_new = jnp.maximum(m, jnp.max(s, axis=0, keepdims=True))
            alphas[r] = jnp.exp2(m - m_new)
            p = jnp.exp2(s - m_new)
            ml[r] = (m_new, alphas[r] * l + jnp.sum(p, axis=0, keepdims=True))
            pt_out[:, r * sw:(r + 1) * sw] = p.astype(BF16)
        return tuple(ml), tuple(alphas)

    bufs = ((st_a, pt_a), (st_b, pt_b))

    def step(kb, parity, alphas_prev, ml, first_sub=0, tri_sub=None, prev_first=0, next_first=0):
        (st_cur, pt_cur), (st_nxt, pt_prv) = bufs[parity], bufs[1 - parity]
        if next_first is not None:
            issue_scores(kb + 1, st_nxt, next_first)
        issue_pv(jnp.maximum(kb - 1, 0), pt_prv, alphas_prev, prev_first)
        return softmax(kb, st_cur, pt_cur, ml, first_sub, tri_sub)

    def body(i, carry):
        alphas, ml = carry
        ml, alphas = step(2 * i, 0, alphas, ml)
        ml, alphas = step(2 * i + 1, 1, alphas, ml)
        return alphas, ml

    acc_ref[...] = jnp.zeros_like(acc_ref)
    pt_b[...] = jnp.zeros_like(pt_b)
    issue_scores(0, st_a)
    n_main = qi * (tq // tk)
    init_ml = tuple((jnp.full((1, sw), NEG_INF, F32), jnp.zeros((1, sw), F32)) for _ in range(nsub))
    alphas, ml = lax.fori_loop(0, n_main // 2, body, (ones, init_ml))
    ndiag = tq // tk
    for d in range(ndiag):
        ml, alphas = step(n_main + d, d % 2, alphas, ml, first_sub=d, tri_sub=d, prev_first=max(d - 1, 0),
                          next_first=d + 1 if d + 1 < ndiag else None)
    issue_pv(n_main + ndiag - 1, bufs[(ndiag - 1) % 2][1], alphas, ndiag - 1)
    for r in range(nsub):
        o_ref[r * sw:(r + 1) * sw, :] = (acc_ref[r] / ml[r][1]).T.astype(o_ref.dtype)


def _fox(qkv, c, lp_real):
    bsz, lp = qkv.shape[:2]
    h, dh, w = FOX_HEADS, FOX_DH, BRANCH_WIDTH
    tq = FOX_TQ
    pos = jnp.arange(lp)
    ck = jnp.where(((pos >= N_PAD) & (pos < lp_real))[None, :, None], c, -NEG_INF) * LOG2E
    est = 2 * (2 * tq * dh * 2 + 2 * lp * dh * 2 + lp * LANE * 4) + (16 << 20)
    return pl.pallas_call(
        _fox_kernel,
        name="fox",
        grid=(bsz, h, lp // tq),
        in_specs=[pl.BlockSpec((None, tq, dh), lambda b, hh, i: (b, i, hh)),
                  pl.BlockSpec((None, lp, dh), lambda b, hh, i: (b, 0, h + hh)),
                  pl.BlockSpec((None, lp, dh), lambda b, hh, i: (b, 0, 2 * h + hh)),
                  pl.BlockSpec((None, lp, h), lambda b, hh, i: (b, 0, 0))],
        out_specs=pl.BlockSpec((None, tq, dh), lambda b, hh, i: (b, i, hh)),
        out_shape=jax.ShapeDtypeStruct((bsz, lp, w), BF16),
        scratch_shapes=[pltpu.VMEM((tq // FOX_SUB, dh, FOX_SUB), F32),
                        pltpu.VMEM((FOX_TK, tq), F32), pltpu.VMEM((FOX_TK, tq), F32),
                        pltpu.VMEM((FOX_TK, tq), BF16), pltpu.VMEM((FOX_TK, tq), BF16),
                        pltpu.VMEM((dh, tq), BF16)],
        compiler_params=_params(("parallel", "parallel", "parallel"), est),
    )(qkv, qkv, qkv, ck)


S5_CHUNK = 8
S5_SLAB = LANE // S5_GROUP
S5_NSLAB = S5_GROUPS // S5_SLAB


def _s5_tables(a_re, a_im, b_re, b_im, c_re, c_im, log_step):
    hp = lax.Precision.HIGHEST
    g, p, cg, n, sg, ns = S5_GROUPS, S5_STATE, S5_GROUP, S5_CHUNK, S5_SLAB, S5_NSLAB
    dt = jnp.exp(log_step)[:, None]
    mag = jnp.exp(a_re * dt)
    bar_re = mag * jnp.cos(a_im * dt)
    bar_im = mag * jnp.sin(a_im * dt)
    den = a_re * a_re + a_im * a_im
    num_re = bar_re - 1.0
    coef_re = (num_re * a_re + bar_im * a_im) / den
    coef_im = (bar_im * a_re - num_re * a_im) / den
    bb_re = coef_re[..., None] * b_re - coef_im[..., None] * b_im
    bb_im = coef_re[..., None] * b_im + coef_im[..., None] * b_re
    pw_re, pw_im = [jnp.ones_like(bar_re)], [jnp.zeros_like(bar_im)]
    for _ in range(n):
        pw_re, pw_im = (pw_re + [pw_re[-1] * bar_re - pw_im[-1] * bar_im],
                        pw_im + [pw_re[-1] * bar_im + pw_im[-1] * bar_re])
    row_group = jnp.arange(LANE) // cg

    def block_diag(v):
        width = v.shape[-1]
        col_group = jnp.arange(sg * width) // width
        return jnp.where(row_group[:, None] == col_group[None, :], jnp.tile(v, (1,) * (v.ndim - 1) + (sg,)), 0.0)

    w_re = jnp.stack([pw_re[n - 1 - s][..., None] * bb_re - pw_im[n - 1 - s][..., None] * bb_im for s in range(n)])
    w_im = jnp.stack([pw_re[n - 1 - s][..., None] * bb_im + pw_im[n - 1 - s][..., None] * bb_re for s in range(n)])
    gm = jnp.concatenate([block_diag(x.transpose(0, 1, 3, 2).reshape(n, ns, LANE, p)) for x in (w_re, w_im)],
                         axis=-1).transpose(1, 0, 2, 3)

    ca_re = jnp.stack([c_re * pw_re[tau][:, None, :] - c_im * pw_im[tau][:, None, :] for tau in range(n + 1)])
    ca_im = jnp.stack([c_re * pw_im[tau][:, None, :] + c_im * pw_re[tau][:, None, :] for tau in range(n + 1)])
    k = (jnp.einsum('tgcp,gpd->tgcd', ca_re[:n], bb_re, precision=hp)
         - jnp.einsum('tgcp,gpd->tgcd', ca_im[:n], bb_im, precision=hp))
    k_bd = block_diag(k.transpose(0, 1, 3, 2).reshape(n, ns, LANE, cg))
    lag = jnp.arange(n)[None, :] - jnp.arange(n)[:, None]
    mm = jnp.where((lag >= 0)[:, :, None, None, None], k_bd[jnp.clip(lag, 0, n - 1)], 0.0)
    mm = mm.transpose(2, 0, 3, 1, 4).reshape(ns, n, LANE, n * LANE)

    pmt = jnp.concatenate([block_diag(x.reshape(n, ns, LANE, p)) for x in (ca_re[1:], -ca_im[1:])], axis=-1)
    pmt = pmt.transpose(1, 0, 2, 3).reshape(ns, n * LANE, 2 * sg * p)
    a8 = jnp.concatenate([pw_re[n].reshape(ns, sg * p), pw_im[n].reshape(ns, sg * p)], axis=1)
    return gm.astype(BF16), mm.astype(BF16), pmt.astype(BF16), a8


def _chunk_rows(ref, s, nb):
    return ref[pl.ds(s, nb, stride=S5_CHUNK), :]


def _s5_in_kernel(u_ref, gm_ref, z_ref):
    nb = z_ref.shape[0]
    acc = None
    for s in range(S5_CHUNK):
        d = jnp.dot(_chunk_rows(u_ref, s, nb).astype(BF16), gm_ref[s], preferred_element_type=F32)
        acc = d if acc is None else acc + d
    z_ref[...] = acc


def _s5_state_kernel(z_ref, a_ref, h_ref, st_ref):
    @pl.when(pl.program_id(1) == 0)
    def _():
        st_ref[...] = jnp.zeros_like(st_ref)

    ns, nb, sw2 = z_ref.shape
    sw = sw2 // 2
    coef = [(a_ref[j:j + 1, :sw], a_ref[j:j + 1, sw:]) for j in range(ns)]

    def body(n, carry):
        out = []
        for j in range(ns):
            hr, hi = carry[j]
            h_ref[j, pl.ds(n, 1), :] = jnp.concatenate([hr, hi], axis=1)
            z = z_ref[j, pl.ds(n, 1), :]
            ar, ai = coef[j]
            out.append((ar * hr - ai * hi + z[:, :sw], ar * hi + ai * hr + z[:, sw:]))
        return tuple(out)

    init = tuple((st_ref[j:j + 1, :sw], st_ref[j:j + 1, sw:]) for j in range(ns))
    fin = lax.fori_loop(0, nb, body, init)
    for j in range(ns):
        st_ref[j:j + 1, :sw] = fin[j][0]
        st_ref[j:j + 1, sw:] = fin[j][1]


def _s5_out_kernel(u_ref, h_ref, mm_ref, pm_ref, d_ref, o_ref, y_ref):
    nb = h_ref.shape[0]
    acc = lax.dot_general(h_ref[...].astype(BF16), pm_ref[...], (((1,), (1,)), ((), ())),
                          preferred_element_type=F32)
    for s in range(S5_CHUNK):
        acc = acc + jnp.dot(_chunk_rows(u_ref, s, nb).astype(BF16), mm_ref[s], preferred_element_type=F32)
    for t in range(S5_CHUNK):
        y_ref[pl.ds(t, nb, stride=S5_CHUNK), :] = acc[:, t * LANE:(t + 1) * LANE]
    o_ref[...] = jax.nn.gelu(y_ref[...] + d_ref[...] * u_ref[...]).astype(o_ref.dtype)


def _s5_branch(proj, u_blk, a_re, a_im, b_re, b_im, c_re, c_im, d_skip, log_step, w_glu):
    bsz, lp = proj.shape[:2]
    w, n, ns = BRANCH_WIDTH, S5_CHUNK, S5_NSLAB
    sw2 = 2 * S5_SLAB * S5_STATE
    tb = _pick(lp, ROW_TILES)
    nb = tb // n
    nchunk = lp // n
    gm, mm, pm, a8 = _s5_tables(a_re, a_im, b_re, b_im, c_re, c_im, log_step)
    u_spec = pl.BlockSpec((None, tb, LANE), lambda b, j, i: (b, i, u_blk * ns + j))
    st_spec = pl.BlockSpec((None, None, nb, sw2), lambda b, j, i: (b, j, i, 0))
    z = pl.pallas_call(
        _s5_in_kernel,
        name="s5_in",
        grid=(bsz, ns, lp // tb),
        in_specs=[u_spec, pl.BlockSpec((None, n, LANE, sw2), lambda b, j, i: (j, 0, 0, 0))],
        out_specs=st_spec,
        out_shape=jax.ShapeDtypeStruct((bsz, ns, nchunk, sw2), F32),
        compiler_params=_params(("parallel", "parallel", "parallel"),
                                2 * (tb * LANE * 4 + n * LANE * sw2 * 2 + nb * sw2 * 4) + 2 * nb * sw2 * 4),
    )(proj, gm)
    nbb = _pick(nchunk, (96, 64, 32, 16, 8))
    blk = pl.BlockSpec((None, ns, nbb, sw2), lambda b, i: (b, 0, i, 0))
    h_start = pl.pallas_call(
        _s5_state_kernel,
        name="s5_state",
        grid=(bsz, nchunk // nbb),
        in_specs=[blk, pl.BlockSpec((ns, sw2), lambda b, i: (0, 0))],
        out_specs=blk,
        out_shape=jax.ShapeDtypeStruct((bsz, ns, nchunk, sw2), F32),
        scratch_shapes=[pltpu.VMEM((ns, sw2), F32)],
        compiler_params=_params(("parallel", "arbitrary"), 4 * ns * nbb * sw2 * 4),
    )(z, a8)
    y = pl.pallas_call(
        _s5_out_kernel,
        name="s5_out",
        grid=(bsz, ns, lp // tb),
        in_specs=[u_spec, st_spec,
                  pl.BlockSpec((None, n, LANE, n * LANE), lambda b, j, i: (j, 0, 0, 0)),
                  pl.BlockSpec((None, n * LANE, sw2), lambda b, j, i: (j, 0, 0)),
                  pl.BlockSpec((1, LANE), lambda b, j, i: (0, j))],
        out_specs=pl.BlockSpec((None, tb, LANE), lambda b, j, i: (b, i, j)),
        out_shape=jax.ShapeDtypeStruct((bsz, lp, w), BF16),
        scratch_shapes=[pltpu.VMEM((tb, LANE), F32)],
        compiler_params=_params(("parallel", "parallel", "parallel"),
                                2 * (tb * LANE * 4 + nb * sw2 * 4 + n * LANE * n * LANE * 2 + sw2 * n * LANE * 2
                                     + tb * LANE * 2) + 3 * nb * n * LANE * 4),
    )(proj, h_start, mm, pm, d_skip.reshape(1, w))
    return _glu(y.reshape(bsz * lp, w), w_glu.astype(BF16))


def _pad_cols(w, n):
    return jnp.pad(w, ((0, 0), (0, n - w.shape[1])))


def _scale_cols(x, s):
    return x * s


MOE_TM = 512


def _new_expert(te_ref, i):
    return (i == 0) | (te_ref[i] != te_ref[jnp.maximum(i - 1, 0)])


def _moe_up_kernel(te_ref, x_ref, wg_ref, wu_ref, s_ref, o_ref, wgb_ref, wub_ref, *, ni):
    i = pl.program_id(1)

    @pl.when(_new_expert(te_ref, i))
    def _():
        wgb_ref[...] = wg_ref[...].astype(BF16)
        wub_ref[...] = wu_ref[...].astype(BF16)

    @pl.when(i < te_ref[ni])
    def _():
        x = x_ref[...]
        g = jnp.dot(x, wgb_ref[...], preferred_element_type=F32)
        u = jnp.dot(x, wub_ref[...], preferred_element_type=F32)
        o_ref[...] = (g * jax.nn.sigmoid(g) * u * s_ref[...]).astype(o_ref.dtype)

    @pl.when(i >= te_ref[ni])
    def _():
        o_ref[...] = jnp.zeros_like(o_ref)


def _moe_down_kernel(te_ref, a_ref, w_ref, o_ref, wb_ref, *, ni):
    i = pl.program_id(1)

    @pl.when(_new_expert(te_ref, i))
    def _():
        wb_ref[...] = w_ref[...].astype(BF16)

    @pl.when(i < te_ref[ni])
    def _():
        o_ref[...] = jnp.dot(a_ref[...], wb_ref[...], preferred_element_type=F32).astype(o_ref.dtype)

    @pl.when(i >= te_ref[ni])
    def _():
        o_ref[...] = jnp.zeros_like(o_ref)


def _moe_experts(xs, row_scale, tile_info, layer, w_gate, w_up, w_down):
    r, d = xs.shape
    f = w_gate.shape[-1]
    tm = MOE_TM
    ni = r // tm
    tn = _pick(f, (512, 256, 128))
    wmap = lambda j, i, te: (layer, te[i], 0, j)
    act = pl.pallas_call(
        functools.partial(_moe_up_kernel, ni=ni),
        name="moe_up",
        grid_spec=pltpu.PrefetchScalarGridSpec(
            num_scalar_prefetch=1, grid=(f // tn, ni),
            in_specs=[pl.BlockSpec((tm, d), lambda j, i, te: (i, 0)),
                      pl.BlockSpec((None, None, d, tn), wmap), pl.BlockSpec((None, None, d, tn), wmap),
                      pl.BlockSpec((tm, 1), lambda j, i, te: (i, 0))],
            out_specs=pl.BlockSpec((tm, tn), lambda j, i, te: (i, j)),
            scratch_shapes=[pltpu.VMEM((d, tn), BF16), pltpu.VMEM((d, tn), BF16)]),
        out_shape=jax.ShapeDtypeStruct((r, f), BF16),
        compiler_params=_params(("arbitrary", "arbitrary"),
                                2 * (tm * d * 2 + 2 * d * tn * 4 + tm * tn * 2) + 2 * d * tn * 2 + 4 * tm * tn * 4),
    )(tile_info, xs, w_gate, w_up, row_scale)
    tn2 = _pick(d, (1024, 512, 256, 128))
    return pl.pallas_call(
        functools.partial(_moe_down_kernel, ni=ni),
        name="moe_down",
        grid_spec=pltpu.PrefetchScalarGridSpec(
            num_scalar_prefetch=1, grid=(d // tn2, ni),
            in_specs=[pl.BlockSpec((tm, f), lambda j, i, te: (i, 0)),
                      pl.BlockSpec((None, None, f, tn2), wmap)],
            out_specs=pl.BlockSpec((tm, tn2), lambda j, i, te: (i, j)),
            scratch_shapes=[pltpu.VMEM((f, tn2), BF16)]),
        out_shape=jax.ShapeDtypeStruct((r, d), F32),
        compiler_params=_params(("arbitrary", "arbitrary"),
                                2 * (tm * f * 2 + f * tn2 * 4 + tm * tn2 * 4) + f * tn2 * 2 + 2 * tm * tn2 * 4),
    )(tile_info, act, w_down)


def _moe_ffn(h32, hb, w_router, b_router, layer, w_gate, w_up, w_down):
    t, d = hb.shape
    e = w_gate.shape[1]
    tm = MOE_TM
    logits = _mm(h32, _pad_cols(w_router, LANE), precision=lax.Precision.HIGHEST,
                 name="router")[:, :N_EXPERTS] + b_router
    top_vals, top_idx = lax.top_k(logits, TOP_K)
    weights = jax.nn.softmax(top_vals, axis=-1)

    n_assign = t * TOP_K
    n_rows = n_assign + e * tm
    flat_e = top_idx.reshape(n_assign).astype(jnp.int32)
    onehot = jax.nn.one_hot(flat_e, e, dtype=jnp.int32)
    seen = jnp.cumsum(onehot, axis=0)
    counts = seen[-1]
    padded = (counts + tm - 1) // tm * tm
    group_end = jnp.cumsum(padded)
    group_start = group_end - padded
    sorted_start = jnp.cumsum(counts) - counts
    dest = group_start[flat_e] + jnp.sum(seen * onehot, axis=1) - 1
    order = jnp.argsort(flat_e, stable=True).astype(jnp.int32)
    rows = jnp.arange(n_rows, dtype=jnp.int32)
    row_e = jnp.minimum(jnp.searchsorted(group_end, rows, side="right"), e - 1).astype(jnp.int32)
    rank = rows - group_start[row_e]
    used = rank < counts[row_e]
    assign = order[jnp.where(used, sorted_start[row_e] + rank, 0)]
    src_tok = jnp.where(used, assign // TOP_K, 0)
    row_scale = jnp.where(used, weights.reshape(n_assign)[assign], 0.0)
    tile_info = jnp.concatenate([row_e[::tm], (group_end[-1:] // tm).astype(jnp.int32)])

    xs = jnp.take(hb, src_tok, axis=0)
    y = _moe_experts(xs, row_scale[:, None], tile_info, layer, w_gate, w_up, w_down)
    dest = dest.reshape(t, TOP_K)
    return [jnp.take(y, dest[:, s], axis=0) for s in range(TOP_K)]


def kernel(x, meta_tokens, w_in, b_forget, w_gate_down, w_gate_up, b_gate, w_branch, w_out, ln_mix_g, ln_mix_b, ln_ffn_g, ln_ffn_b, rwkv_mu, rwkv_w_rkv, rwkv_w0, rwkv_w1, rwkv_w2, rwkv_a0, rwkv_a1, rwkv_a2, rwkv_v0, rwkv_v1, rwkv_v2, rwkv_g1, rwkv_g2, rwkv_k_k, rwkv_k_a, rwkv_r_k, rwkv_ln_w, rwkv_ln_b, s5_a_re, s5_a_im, s5_b_re, s5_b_im, s5_c_re, s5_c_im, s5_d, s5_log_step, s5_w_glu, ffn_w_gate, ffn_w_up, ffn_w_down, moe_router, moe_router_b, moe_w_gate, moe_w_up, moe_w_down):
    bsz, seq, d = x.shape
    depth = w_in.shape[0]
    w = BRANCH_WIDTH
    lp_real = N_PAD + N_META + seq
    lp = -(-lp_real // FOX_TQ) * FOX_TQ
    t = bsz * lp
    meta = jnp.broadcast_to(meta_tokens.astype(x.dtype)[None], (bsz, N_META, d))
    h = jnp.concatenate([jnp.zeros((bsz, N_PAD, d), x.dtype), meta, x,
                         jnp.zeros((bsz, lp - lp_real, d), x.dtype)], axis=1)
    hb = h.astype(BF16)
    pos = jnp.arange(lp)
    real = ((pos >= N_PAD) & (pos < lp_real))[None, :, None]
    f_lo = 7 * w
    q_scale = jnp.concatenate([jnp.full((1, w), FOX_DH ** -0.5 * LOG2E, F32), jnp.ones((1, 2 * w), F32)], axis=1)
    ret_consts = _ret_consts(lp)
    v_first = None
    for layer in range(depth):
        wl = w_in[layer]
        w_f32 = jnp.concatenate([wl[:, :4 * w], wl[:, f_lo + FOX_HEADS:]], axis=1).astype(BF16)
        w_fox = wl[:, 4 * w:f_lo].astype(BF16)
        w_small = _pad_cols(jnp.concatenate([w_gate_down[layer], wl[:, f_lo:f_lo + FOX_HEADS]], axis=1),
                            GATE_RANK + LANE).astype(BF16)
        hb2 = hb.reshape(t, d)
        proj = _mm(hb2, w_f32, name="proj").reshape(bsz, lp, 5 * w)
        qkv = _mm(hb2, w_fox, out_dtype=BF16, ep=_scale_cols, extras=((q_scale, "row", 0),),
                  name="proj_fox").reshape(bsz, lp, 3 * w)
        small = _mm(hb2, w_small, name="proj_small")
        f_logit = small[:, GATE_RANK:GATE_RANK + FOX_HEADS].reshape(bsz, lp, FOX_HEADS)

        o_ret = _retention(proj, ret_consts)

        vres = None if layer == 0 else (rwkv_v0[layer - 1], rwkv_v1[layer - 1], rwkv_v2[layer - 1])
        r, lw, k, v, a, b, gate, bonus = _rwkv_pre(
            proj, 3, v_first, vres, rwkv_mu[layer], rwkv_w_rkv[layer], rwkv_w0[layer], rwkv_w1[layer],
            rwkv_w2[layer], rwkv_a0[layer], rwkv_a1[layer], rwkv_a2[layer], rwkv_g1[layer], rwkv_g2[layer],
            rwkv_k_k[layer], rwkv_k_a[layer], rwkv_r_k[layer])
        if layer == 0:
            v_first = v
        o_rwkv = _wkv7(r, lw, k, v, a, b, gate, bonus, rwkv_ln_w[layer], rwkv_ln_b[layer])

        log_f = jnp.where(real, jax.nn.log_sigmoid(f_logit + b_forget[layer]), 0.0)
        c = jnp.cumsum(log_f, axis=1)
        o_fox = _fox(qkv, c, lp_real)

        o_s5 = _s5_branch(proj, 4, s5_a_re[layer], s5_a_im[layer], s5_b_re[layer], s5_b_im[layer],
                          s5_c_re[layer], s5_c_im[layer], s5_d[layer], s5_log_step[layer], s5_w_glu[layer])

        merged = _merge([o_ret.reshape(t, w), o_rwkv.reshape(t, w), o_fox.reshape(t, w), o_s5],
                        w_branch[layer].astype(BF16), small, w_gate_up[layer].astype(BF16), b_gate[layer])
        mix = _mm(merged, w_out, b_layer=layer, name="w_out")
        h, hb = _res_ln([mix], h, ln_mix_g[layer], ln_mix_b[layer], mask_pad=None)

        j = layer // 2
        hb2 = hb.reshape(t, d)
        if layer % 2 == 0:
            act = _swiglu_up(hb2, ffn_w_gate[j].astype(BF16), ffn_w_up[j].astype(BF16))
            ffn = [_mm(act, ffn_w_down[j].astype(BF16), name="ffn_down")]
        else:
            ffn = _moe_ffn(h.reshape(t, d), hb2, moe_router[j], moe_router_b[j], j,
                           moe_w_gate, moe_w_up, moe_w_down)
        h, hb = _res_ln(ffn, h, ln_ffn_g[layer], ln_ffn_b[layer], mask_pad=(N_PAD, lp_real))
    return h[:, N_PAD + N_META:lp_real]
```

```python
import functools
import math

import jax
import jax.numpy as jnp
from jax import lax
from jax.experimental import pallas as pl
from jax.experimental.pallas import tpu as pltpu

F32 = jnp.float32
BF16 = jnp.bfloat16

D_MODEL = 4096
DEPTH = 4
N_META = 16
BLOCK = 128
N_PAD = (-N_META) % BLOCK
BRANCH_WIDTH = D_MODEL // 4
GATE_RANK = 256
RET_DK = 64
RET_DV = 128
RET_HEADS = BRANCH_WIDTH // RET_DV
RWKV_HEAD = 64
RWKV_HEADS = BRANCH_WIDTH // RWKV_HEAD
RWKV_GN_EPS = 64e-5
FOX_DH = 128
FOX_HEADS = BRANCH_WIDTH // FOX_DH
S5_GROUP = 16
S5_GROUPS = BRANCH_WIDTH // S5_GROUP
S5_STATE = 64
N_EXPERTS = 8
TOP_K = 2
ALPHA = (2.0 * DEPTH) ** 0.25
LN_EPS = 1e-5
NEG_INF = -1e30
LOG2E = math.log2(math.e)

V7X_VMEM_BUDGET = 58 * 1024 * 1024
LANE = 128
RWKV_CHUNK = 64
RWKV_QUAD = 4
RWKV_QW = RWKV_QUAD * RWKV_HEAD
FOX_TK = 128
FOX_SUB = 128
FOX_TQ = 768
ROW_TILES = (1408, 1536, 1280, 768, 640, 512, 256, 128)


def _pick(dim, candidates):
    for c in candidates:
        if dim % c == 0:
            return c
    return dim


def _params(sem, est_bytes):
    limit = int(min(V7X_VMEM_BUDGET, max(est_bytes * 5 // 4 + (2 << 20), 16 << 20)))
    return pltpu.CompilerParams(dimension_semantics=sem, vmem_limit_bytes=limit)


def _nbytes(shape, dtype):
    n = jnp.dtype(dtype).itemsize
    for s in shape:
        if s is not None:
            n *= s
    return n


def _mmf_kernel(*refs, nb, nx, nk, ep, precision):
    a_ref = refs[0]
    b_refs = refs[1:1 + nb]
    x_refs = refs[1 + nb:1 + nb + nx]
    o_ref = refs[1 + nb + nx]
    acc_refs = refs[2 + nb + nx:]
    a = a_ref[...]
    if precision is None and a.dtype != BF16:
        a = a.astype(BF16)
    def weight(b_ref):
        b = b_ref[...]
        return b.astype(BF16) if precision is None and b.dtype != BF16 else b

    dots = [jnp.dot(a, weight(b), preferred_element_type=F32, precision=precision) for b in b_refs]

    def finish(vals):
        o_ref[...] = ep(*vals, *[x[...] for x in x_refs]).astype(o_ref.dtype)

    if nk == 1:
        finish(dots)
        return
    k = pl.program_id(2)

    @pl.when(k == 0)
    def _():
        for acc, d in zip(acc_refs, dots):
            acc[...] = d

    @pl.when(k > 0)
    def _():
        for acc, d in zip(acc_refs, dots):
            acc[...] += d

    @pl.when(k == nk - 1)
    def _():
        finish([acc[...] for acc in acc_refs])


def _mmf(a_spec, b_specs, x_specs, ep, *, m, n, kd, tm, tn, tk, out_dtype, precision=None, name="mm"):
    nk = kd // tk
    specs = [a_spec] + list(b_specs) + list(x_specs)
    est = 2 * sum(_nbytes(s[1], s[0].dtype) for s in specs) + 2 * tm * tn * jnp.dtype(out_dtype).itemsize
    est += (len(b_specs) + 2) * tm * tn * 4
    scratch = [pltpu.VMEM((tm, tn), F32) for _ in b_specs] if nk > 1 else []
    return pl.pallas_call(
        functools.partial(_mmf_kernel, nb=len(b_specs), nx=len(x_specs), nk=nk, ep=ep, precision=precision),
        name=name,
        grid=(m // tm, n // tn, nk),
        in_specs=[pl.BlockSpec(s[1], s[2]) for s in specs],
        out_specs=pl.BlockSpec((tm, tn), lambda i, j, k: (i, j)),
        out_shape=jax.ShapeDtypeStruct((m, n), out_dtype),
        scratch_shapes=scratch,
        compiler_params=_params(("parallel", "parallel", "arbitrary"), est),
    )(*[s[0] for s in specs])


def _ident(x):
    return x


def _mm(a, b, out_dtype=F32, precision=None, ep=_ident, extras=(), a_blk=0, b_layer=None, name="mm"):
    m = a.shape[0]
    kd, n = b.shape[-2:]
    tm = _pick(m, ROW_TILES)
    tn = _pick(n, (1024, 512, 384, 256, 128))
    tk = kd if kd <= 4096 else _pick(kd, (2048, 1536, 1024, 512))
    if precision is not None:
        tm = _pick(m, (640, 512, 256, 128))
        tn = _pick(n, (512, 384, 256, 128))
        tk = kd if kd <= 2048 else _pick(kd, (2048, 1024, 512))
    nk = kd // tk
    ob = jnp.dtype(out_dtype).itemsize

    def need(tn_):
        return (2 * (tm * tk * a.dtype.itemsize + tk * tn_ * b.dtype.itemsize + tm * tn_ * ob)
                + (3 if nk > 1 else 2) * tm * tn_ * 4)

    while need(tn) > V7X_VMEM_BUDGET - (6 << 20) and tn % 256 == 0 and n % (tn // 2) == 0:
        tn //= 2
    a_spec = (a, (tm, tk), lambda i, j, k: (i, a_blk * nk + k))
    if b_layer is None:
        b_spec = (b, (tk, tn), lambda i, j, k: (k, j))
    else:
        b_spec = (b, (None, tk, tn), lambda i, j, k: (b_layer, k, j))
    x_specs = []
    for arr, kind, off in extras:
        if kind == "tile":
            x_specs.append((arr, (tm, tn), lambda i, j, k, off=off: (i, j + off)))
        else:
            x_specs.append((arr, (1, tn), lambda i, j, k, off=off: (0, j + off)))
    return _mmf(a_spec, [b_spec], x_specs, ep, m=m, n=n, kd=kd, tm=tm, tn=tn, tk=tk,
                out_dtype=out_dtype, precision=precision, name=name)


def _silu_mul(g, u):
    return g * jax.nn.sigmoid(g) * u


def _swiglu_up(h, wg, wu):
    m, d = h.shape
    f = wg.shape[1]
    tm = _pick(m, ROW_TILES)
    tn = _pick(f, (512, 256, 128))
    wmap = lambda i, j, k: (0, j)
    return _mmf((h, (tm, d), lambda i, j, k: (i, 0)),
                [(wg, (d, tn), wmap), (wu, (d, tn), wmap)], [], _silu_mul,
                m=m, n=f, kd=d, tm=tm, tn=tn, tk=d, out_dtype=BF16, name="swiglu_up")


def _glu_ep(val, gt):
    return val * jax.nn.sigmoid(gt)


def _glu(y, w_glu):
    m, kd = y.shape
    n = w_glu.shape[1] // 2
    tm = _pick(m, ROW_TILES)
    tn = _pick(n, (512, 256, 128))
    nj = n // tn
    return _mmf((y, (tm, kd), lambda i, j, k: (i, 0)),
                [(w_glu, (kd, tn), lambda i, j, k: (0, j)), (w_glu, (kd, tn), lambda i, j, k: (0, nj + j))],
                [], _glu_ep, m=m, n=n, kd=kd, tm=tm, tn=tn, tk=kd, out_dtype=BF16, name="s5_glu")


def _res_ln_kernel(*refs, ny, tm, mask_pad):
    y_refs = refs[:ny]
    r_ref, g_ref, b_ref, o_ref, ob_ref = refs[ny:]
    z = ALPHA * r_ref[...] + y_refs[0][...]
    for y_ref in y_refs[1:]:
        z = z + y_ref[...]
    mu = jnp.mean(z, axis=-1, keepdims=True)
    zc = z - mu
    var = jnp.mean(zc * zc, axis=-1, keepdims=True)
    out = zc * lax.rsqrt(var + LN_EPS) * g_ref[...] + b_ref[...]
    if mask_pad is not None:
        pos = pl.program_id(1) * tm + lax.broadcasted_iota(jnp.int32, (tm, 1), 0)
        out = jnp.where((pos >= mask_pad[0]) & (pos < mask_pad[1]), out, 0.0)
    o_ref[...] = out
    ob_ref[...] = out.astype(BF16)


def _res_ln(ys, res, g, b, mask_pad):
    bsz, lp, d = res.shape
    tm = _pick(lp, (256, 128, 64, 32, 16, 8))
    est = 2 * ((4 + len(ys)) * tm * d * 4) + 4 * tm * d * 4
    spec = pl.BlockSpec((None, tm, d), lambda bi, i: (bi, i, 0))
    vspec = pl.BlockSpec((1, d), lambda bi, i: (0, 0))
    return pl.pallas_call(
        functools.partial(_res_ln_kernel, ny=len(ys), tm=tm, mask_pad=mask_pad),
        name="res_ln",
        grid=(bsz, lp // tm),
        in_specs=[spec] * (len(ys) + 1) + [vspec, vspec],
        out_specs=[spec, spec],
        out_shape=[jax.ShapeDtypeStruct((bsz, lp, d), F32), jax.ShapeDtypeStruct((bsz, lp, d), BF16)],
        compiler_params=_params(("parallel", "parallel"), est),
    )(*[y.reshape(bsz, lp, d) for y in ys], res, g.reshape(1, d), b.reshape(1, d))


def _merge_kernel(o0, o1, o2, o3, wb_ref, glr_ref, wgu_ref, bg_ref, out_ref):
    glr = glr_ref[...].astype(BF16)
    acc = None
    for i, o_ref in enumerate((o0, o1, o2, o3)):
        gate = jax.nn.sigmoid(jnp.dot(glr, wgu_ref[i], preferred_element_type=F32) + bg_ref[i])
        term = gate * jnp.dot(o_ref[...], wb_ref[i], preferred_element_type=F32)
        acc = term if acc is None else acc + term
    out_ref[...] = acc.astype(out_ref.dtype)


def _merge(outs, w_branch, small, w_gate_up, b_gate):
    m, w = outs[0].shape
    nb, r, d = w_gate_up.shape
    tm = _pick(m, ROW_TILES)
    tn = _pick(d, (512, 256, 128))
    est = 2 * (nb * tm * w * 2 + nb * w * tn * 2 + tm * r * 4 + nb * r * tn * 2 + tm * tn * 2) + 4 * tm * tn * 4
    ospec = pl.BlockSpec((tm, w), lambda i, j: (i, 0))
    return pl.pallas_call(
        _merge_kernel,
        name="merge",
        grid=(m // tm, d // tn),
        in_specs=[ospec, ospec, ospec, ospec,
                  pl.BlockSpec((nb, w, tn), lambda i, j: (0, 0, j)),
                  pl.BlockSpec((tm, r), lambda i, j: (i, 0)),
                  pl.BlockSpec((nb, r, tn), lambda i, j: (0, 0, j)),
                  pl.BlockSpec((nb, 1, tn), lambda i, j: (0, 0, j))],
        out_specs=pl.BlockSpec((tm, tn), lambda i, j: (i, j)),
        out_shape=jax.ShapeDtypeStruct((m, d), BF16),
        compiler_params=_params(("parallel", "parallel"), est),
    )(*outs, w_branch, small, w_gate_up, b_gate.reshape(nb, 1, d))


def _ret_consts(lp):
    f32 = F32
    half = RET_DK // 2
    inv = 1.0 / (10000.0 ** jnp.linspace(0.0, 1.0, half, dtype=f32))
    ang = jnp.arange(lp, dtype=f32)[:, None] * inv[None, :]
    cos, sin = jnp.cos(ang), jnp.sin(ang)
    cos_t = jnp.tile(jnp.concatenate([cos, cos], axis=-1), (1, RET_HEADS))
    sin_t = jnp.tile(jnp.concatenate([-sin, sin], axis=-1), (1, RET_HEADS))
    log_gamma = jnp.log(1.0 - 2.0 ** (-5.0 - jnp.arange(RET_HEADS, dtype=f32)))
    pos = jnp.arange(BLOCK, dtype=f32)
    rel = pos[:, None] - pos[None, :]
    dmat = jnp.where(rel >= 0, jnp.exp(log_gamma[:, None, None] * jnp.maximum(rel, 0.0)), 0.0)
    k_decay = jnp.exp(log_gamma[:, None] * (BLOCK - 1.0 - pos)[None, :])
    q_decay = jnp.exp(log_gamma[:, None] * (pos + 1.0)[None, :])
    kd_t = jnp.repeat(k_decay.T, RET_DK, axis=1)
    qd_t = jnp.repeat(q_decay.T, RET_DK, axis=1)
    cd = jnp.exp(log_gamma * BLOCK)
    cd_t = jnp.broadcast_to(cd[:, None, None], (RET_HEADS, 1, RET_DV))
    return cos_t, sin_t, dmat, qd_t, kd_t, cd_t


def _ret_kernel(q_ref, k_ref, v_ref, g_ref, cos_ref, sin_ref, dmat_ref, qd_ref, kd_ref, cd_ref,
                o_ref, st_ref):
    @pl.when(pl.program_id(1) == 0)
    def _():
        st_ref[...] = jnp.zeros_like(st_ref)

    cos = cos_ref[...]
    sin = sin_ref[...]
    width = RET_HEADS * RET_DK
    lane = lax.broadcasted_iota(jnp.int32, (BLOCK, width), 1)
    first = (lane % RET_DK) < (RET_DK // 2)

    def rot(x):
        sw = jnp.where(first, pltpu.roll(x, width - RET_DK // 2, 1), pltpu.roll(x, RET_DK // 2, 1))
        return x * cos + sw * sin

    q = rot(q_ref[...])
    k = rot(k_ref[...]) * RET_DK ** -0.5
    qb = q.astype(BF16)
    kb = k.astype(BF16)
    qs = (q * qd_ref[...]).astype(BF16)
    ks = (k * kd_ref[...]).astype(BF16)
    for h in range(RET_HEADS):
        ksl = slice(h * RET_DK, (h + 1) * RET_DK)
        vsl = slice(h * RET_DV, (h + 1) * RET_DV)
        vh = v_ref[:, vsl].astype(BF16)
        s = lax.dot_general(qb[:, ksl], kb[:, ksl], (((1,), (1,)), ((), ())),
                            preferred_element_type=F32) * dmat_ref[h]
        st = st_ref[h]
        y = (jnp.dot(s.astype(BF16), vh, preferred_element_type=F32)
             + jnp.dot(qs[:, ksl], st.astype(BF16), preferred_element_type=F32))
        st_ref[h] = st * cd_ref[h] + lax.dot_general(ks[:, ksl], vh, (((0,), (0,)), ((), ())),
                                                     preferred_element_type=F32)
        mu = jnp.mean(y, axis=-1, keepdims=True)
        yc = y - mu
        var = jnp.mean(yc * yc, axis=-1, keepdims=True)
        gh = g_ref[:, vsl]
        o_ref[:, vsl] = (gh * jax.nn.sigmoid(gh) * yc * lax.rsqrt(var + 1e-6)).astype(o_ref.dtype)


def _retention(proj, consts):
    bsz, lp = proj.shape[:2]
    kw = RET_HEADS * RET_DK
    cos_t, sin_t, dmat, qd_t, kd_t, cd_t = consts
    est = 2 * BLOCK * 4 * (2 * kw + 2 * BRANCH_WIDTH + 4 * kw) + 2 * RET_HEADS * BLOCK * BLOCK * 4 + (8 << 20)
    tab = pl.BlockSpec((BLOCK, kw), lambda b, n: (n, 0))
    ctab = pl.BlockSpec((BLOCK, kw), lambda b, n: (0, 0))
    return pl.pallas_call(
        _ret_kernel,
        name="retention",
        grid=(bsz, lp // BLOCK),
        in_specs=[pl.BlockSpec((None, BLOCK, kw), lambda b, n: (b, n, 0)),
                  pl.BlockSpec((None, BLOCK, kw), lambda b, n: (b, n, 1)),
                  pl.BlockSpec((None, BLOCK, BRANCH_WIDTH), lambda b, n: (b, n, 1)),
                  pl.BlockSpec((None, BLOCK, BRANCH_WIDTH), lambda b, n: (b, n, 2)),
                  tab, tab,
                  pl.BlockSpec((RET_HEADS, BLOCK, BLOCK), lambda b, n: (0, 0, 0)),
                  ctab, ctab,
                  pl.BlockSpec((RET_HEADS, 1, RET_DV), lambda b, n: (0, 0, 0))],
        out_specs=pl.BlockSpec((None, BLOCK, BRANCH_WIDTH), lambda b, n: (b, n, 0)),
        out_shape=jax.ShapeDtypeStruct((bsz, lp, BRANCH_WIDTH), BF16),
        scratch_shapes=[pltpu.VMEM((RET_HEADS, RET_DK, RET_DV), F32)],
        compiler_params=_params(("parallel", "arbitrary"), est),
    )(proj, proj, proj, proj, cos_t, sin_t, dmat, qd_t, kd_t, cd_t)


def _split_dot(x, ones):
    hi = x.astype(BF16)
    lo = (x - hi.astype(F32)).astype(BF16)
    return (jnp.dot(hi, ones, preferred_element_type=F32) + jnp.dot(lo, ones, preferred_element_type=F32))


_RV_MU, _RV_W0, _RV_A0, _RV_V0, _RV_KK, _RV_KA, _RV_RK = 0, 6, 7, 8, 9, 10, 11
_RV_ROWS = 16


def _rwkv_pre_kernel(*refs, has_vres):
    (u_ref, up_ref, wr_ref, wk_ref, wv_ref, w1_ref, w2_ref, a1_ref, a2_ref, g1_ref, g2_ref,
     ones_ref, vec_ref) = refs[:13]
    pos = 13
    if has_vres:
        vf_ref, v1_ref, v2_ref = refs[13:16]
        pos = 16
    r_out, lw_out, k_out, v_out, a_out, b_out, gate_out, bonus_out = refs[pos:pos + 8]

    def vec(i):
        return vec_ref[i:i + 1, :]

    u = u_ref[...]
    tm = u.shape[0]
    prev_last = jnp.where(pl.program_id(1) == 0, 0.0, up_ref[7:8, :])
    row = lax.broadcasted_iota(jnp.int32, (tm, 1), 0)
    prev = jnp.where(row == 0, prev_last, pltpu.roll(u, 1, 0))
    xx = prev - u

    def mix(i):
        return (u + xx * vec(_RV_MU + i)).astype(BF16)

    def mm(x, w_ref):
        return jnp.dot(x, w_ref[...], preferred_element_type=F32)

    xr, xw, xk, xv, xa, xg = [mix(i) for i in range(6)]
    r = mm(xr, wr_ref)
    k = mm(xk, wk_ref)
    v = mm(xv, wv_ref)
    z = -(vec(_RV_W0) + mm(jnp.tanh(mm(xw, w1_ref)).astype(BF16), w2_ref))
    softplus = jnp.maximum(z, 0.0) + jnp.log(1.0 + jnp.exp(-jnp.abs(z)))
    lw_out[...] = -jnp.exp(-softplus - 0.5)
    if has_vres:
        mv = jax.nn.sigmoid(vec(_RV_V0) + mm(mm(xv, v1_ref).astype(BF16), v2_ref))
        v = v + (vf_ref[...] - v) * mv
    a = jax.nn.sigmoid(vec(_RV_A0) + mm(mm(xa, a1_ref).astype(BF16), a2_ref))
    gate_out[...] = mm(jax.nn.sigmoid(mm(xg, g1_ref)).astype(BF16), g2_ref)

    ones = ones_ref[...]
    kk = k * vec(_RV_KK)
    kk = kk / jnp.maximum(jnp.sqrt(_split_dot(kk * kk, ones)), 1e-12)
    k = k * (1.0 + (a - 1.0) * vec(_RV_KA))
    r_out[...] = r
    k_out[...] = k
    v_out[...] = v
    a_out[...] = -kk
    b_out[...] = kk * a
    bonus_out[...] = _split_dot(r * k * vec(_RV_RK), ones) * v


def _rwkv_pre(proj, u_blk, v_first, vres, mu, w_rkv, w0, w1, w2, a0, a1, a2, g1, g2, k_k, k_a, r_k):
    bsz, lp = proj.shape[:2]
    w = BRANCH_WIDTH
    tm = _pick(lp, (256, 128, 64, 32, 16, 8))
    has_vres = vres is not None

    def lora_pair(wa, wb):
        rp = -(-wa.shape[1] // LANE) * LANE
        return (jnp.pad(wa, ((0, 0), (0, rp - wa.shape[1]))).astype(BF16),
                jnp.pad(wb, ((0, rp - wb.shape[0]), (0, 0))).astype(BF16))

    w1p, w2p = lora_pair(w1, w2)
    a1p, a2p = lora_pair(a1, a2)
    g1p, g2p = lora_pair(g1, g2)
    hid = jnp.arange(w) // RWKV_HEAD
    ones = (hid[:, None] == hid[None, :]).astype(BF16)
    zero = jnp.zeros((w,), F32)
    rows = [mu[i] for i in range(6)] + [w0, a0, vres[0] if has_vres else zero, k_k, k_a, r_k.reshape(w)]
    vec = jnp.stack(rows + [zero] * (_RV_ROWS - len(rows)))

    tile = pl.BlockSpec((None, tm, w), lambda b, i: (b, i, 0))
    const = lambda arr: pl.BlockSpec(arr.shape, lambda b, i: (0,) * arr.ndim)
    weights = [w_rkv[0].astype(BF16), w_rkv[1].astype(BF16), w_rkv[2].astype(BF16),
               w1p, w2p, a1p, a2p, g1p, g2p, ones, vec]
    args = [proj, proj] + weights
    in_specs = [pl.BlockSpec((None, tm, w), lambda b, i: (b, i, u_blk)),
                pl.BlockSpec((None, 8, w), lambda b, i: (b, jnp.maximum(i * (tm // 8) - 1, 0), u_blk))]
    in_specs += [const(x) for x in weights]
    if has_vres:
        v1p, v2p = lora_pair(vres[1], vres[2])
        args += [v_first, v1p, v2p]
        in_specs += [tile, const(v1p), const(v2p)]
    est = 2 * sum(x.size * x.dtype.itemsize for x in weights) + 2 * 11 * tm * w * 4 + 40 * tm * w * 4
    outs = pl.pallas_call(
        functools.partial(_rwkv_pre_kernel, has_vres=has_vres),
        name="rwkv_pre",
        grid=(bsz, lp // tm),
        in_specs=in_specs,
        out_specs=[tile] * 8,
        out_shape=[jax.ShapeDtypeStruct((bsz, lp, w), F32)] * 8,
        compiler_params=_params(("parallel", "parallel"), est),
    )(*args)
    return outs


def _wkv_kernel(r_ref, lw_ref, k_ref, v_ref, a_ref, b_ref, gate_ref, bonus_ref, lnw_ref, lnb_ref,
                o_ref, ht_ref):
    c = RWKV_CHUNK
    wd = RWKV_QW

    @pl.when(pl.program_id(1) == 0)
    def _():
        ht_ref[...] = jnp.zeros_like(ht_ref)

    ti = lax.broadcasted_iota(jnp.int32, (c, c), 0)
    tj = lax.broadcasted_iota(jnp.int32, (c, c), 1)
    cum_all = jnp.dot((ti >= tj).astype(F32), lw_ref[...], preferred_element_type=F32,
                      precision=lax.Precision.HIGHEST)

    ri = lax.broadcasted_iota(jnp.int32, (wd, wd), 0)
    ci = lax.broadcasted_iota(jnp.int32, (wd, wd), 1)
    same = (ri // c) == (ci // RWKV_HEAD)
    strict = same & ((ci % c) < (ri % c))
    incl = same & ((ci % c) <= (ri % c))
    ones_blk = jnp.where(same, 1.0, 0.0).astype(BF16)

    def expand(x):
        xb = x.astype(BF16)
        return jnp.where(same, jnp.concatenate([xb] * RWKV_QUAD, axis=0), jnp.zeros((), BF16))

    def nt(x, y):
        return lax.dot_general(x, y, (((1,), (1,)), ((), ())), preferred_element_type=F32)

    def tn(x, y):
        return lax.dot_general(x, y, (((0,), (0,)), ((), ())), preferred_element_type=F32)

    def mm(x, y):
        return jnp.dot(x, y, preferred_element_type=F32)

    groups = range(r_ref.shape[1] // wd)
    sls = [slice(q * wd, (q + 1) * wd) for q in groups]
    ar, bt, kt, vb, bw, kw, w_last = [], [], [], [], [], [], []
    for sl in sls:
        lw = lw_ref[:, sl]
        cum = cum_all[:, sl]
        cum_last = cum[c - 1:c, :]
        e_neg = jnp.exp(-cum)
        e_rem = jnp.exp(cum_last - cum)
        k = k_ref[:, sl]
        b = b_ref[:, sl]
        ar.append(jnp.concatenate([expand(a_ref[:, sl] * jnp.exp(cum - lw)),
                                   expand(r_ref[:, sl] * jnp.exp(cum))], axis=0))
        bt.append(expand(b * e_neg))
        kt.append(expand(k * e_neg))
        vb.append(expand(v_ref[:, sl]))
        bw.append(expand(b * e_rem))
        kw.append(expand(k * e_rem))
        w_last.append(jnp.exp(cum_last))

    g_b = [nt(ar[q], bt[q]) for q in groups]
    g_k = [nt(ar[q], kt[q]) for q in groups]
    hts = [ht_ref[q] for q in groups]
    g_h = [nt(ar[q], hts[q].astype(BF16)) for q in groups]
    p = [jnp.where(strict, g_b[q][:wd], 0.0) for q in groups]
    a_ak = [jnp.where(strict, g_k[q][:wd], 0.0).astype(BF16) for q in groups]
    a_rb = [jnp.where(incl, g_b[q][wd:], 0.0).astype(BF16) for q in groups]
    a_rk = [jnp.where(incl, g_k[q][wd:], 0.0).astype(BF16) for q in groups]
    u = [g_h[q][:wd] + mm(a_ak[q], vb[q]) for q in groups]
    steps = int(math.log2(c))
    for i in range(steps):
        pb = [x.astype(BF16) for x in p]
        u = [u[q] + mm(pb[q], u[q].astype(BF16)) for q in groups]
        if i < steps - 1:
            p = [mm(pb[q], pb[q]) for q in groups]
    ub = [x.astype(BF16) for x in u]
    y_bd = [g_h[q][wd:] + mm(a_rb[q], ub[q]) + mm(a_rk[q], vb[q]) for q in groups]
    for q in groups:
        ht_ref[q] = hts[q] * w_last[q] + tn(ub[q], bw[q]) + tn(vb[q], kw[q])
    for q in groups:
        sl = sls[q]
        y = y_bd[q][0:c]
        for h in range(1, RWKV_QUAD):
            y = y + y_bd[q][h * c:(h + 1) * c]
        mean = _split_dot(y, ones_blk) * (1.0 / RWKV_HEAD)
        yc = y - mean
        var = _split_dot(yc * yc, ones_blk) * (1.0 / RWKV_HEAD)
        out = yc * lax.rsqrt(var + RWKV_GN_EPS) * lnw_ref[:, sl] + lnb_ref[:, sl] + bonus_ref[:, sl]
        o_ref[:, sl] = (out * gate_ref[:, sl]).astype(o_ref.dtype)


def _wkv7(r, lw, k, v, a, b, gate, bonus, ln_w, ln_b):
    bsz, lp, w = r.shape
    c = RWKV_CHUNK
    spec = pl.BlockSpec((None, c, w), lambda bi, n: (bi, n, 0))
    vspec = pl.BlockSpec((1, w), lambda bi, n: (0, 0))
    est = 2 * 9 * c * w * 4 + 4 * 40 * RWKV_QW * RWKV_QW * 4
    return pl.pallas_call(
        _wkv_kernel,
        name="wkv7",
        grid=(bsz, lp // c),
        in_specs=[spec] * 8 + [vspec, vspec],
        out_specs=spec,
        out_shape=jax.ShapeDtypeStruct((bsz, lp, w), BF16),
        scratch_shapes=[pltpu.VMEM((w // RWKV_QW, RWKV_QW, RWKV_QW), F32)],
        compiler_params=_params(("parallel", "arbitrary"), est),
    )(r, lw, k, v, a, b, gate, bonus, ln_w.reshape(1, w), ln_b.reshape(1, w))


def _fox_kernel(q_ref, k_ref, v_ref, ck_ref, o_ref, acc_ref, st_a, st_b, pt_a, pt_b, qt_ref):
    tk, sw = FOX_TK, FOX_SUB
    tq = q_ref.shape[0]
    nsub = tq // sw
    qi = pl.program_id(2)
    qt_ref[...] = q_ref[...].astype(F32).T.astype(BF16)
    head_lane = lax.broadcasted_iota(jnp.int32, (tk, FOX_HEADS), 1) == pl.program_id(1)
    tri = (lax.broadcasted_iota(jnp.int32, (tk, sw), 0) <= lax.broadcasted_iota(jnp.int32, (tk, sw), 1))
    ones = tuple(jnp.ones((1, sw), F32) for _ in range(nsub))

    def issue_scores(kb, st_out, first_sub=0):
        ks = pl.multiple_of(kb * tk, tk)
        st_out[:, first_sub * sw:] = jnp.dot(k_ref[pl.ds(ks, tk), :], qt_ref[:, first_sub * sw:],
                                             preferred_element_type=F32)

    def issue_pv(kb, pt_in, alphas, first_sub=0):
        ks = pl.multiple_of(kb * tk, tk)
        pv = lax.dot_general(v_ref[pl.ds(ks, tk), :], pt_in[:, first_sub * sw:], (((0,), (0,)), ((), ())),
                             preferred_element_type=F32)
        for r in range(first_sub, nsub):
            acc_ref[r] = alphas[r] * acc_ref[r] + pv[:, (r - first_sub) * sw:(r - first_sub + 1) * sw]

    def softmax(kb, st_in, pt_out, ml, first_sub=0, tri_sub=None):
        ck_heads = ck_ref[pl.ds(pl.multiple_of(kb * tk, tk), tk), :]
        ck = jnp.sum(jnp.where(head_lane, ck_heads, 0.0), axis=1, keepdims=True)
        ck = jnp.broadcast_to(ck, (tk, sw))
        ml, alphas = list(ml), list(ones)
        for r in range(first_sub, nsub):
            s = st_in[:, r * sw:(r + 1) * sw] - ck
            if r == tri_sub:
                s = jnp.where(tri, s, NEG_INF)
            m, l = ml[r]
            m_new = jnp.maximum(m, jnp.max(s, axis=0, keepdims=True))
            alphas[r] = jnp.exp2(m - m_new)
            p = jnp.exp2(s - m_new)
            ml[r] = (m_new, alphas[r] * l + jnp.sum(p, axis=0, keepdims=True))
            pt_out[:, r * sw:(r + 1) * sw] = p.astype(BF16)
        return tuple(ml), tuple(alphas)

    bufs = ((st_a, pt_a), (st_b, pt_b))

    def step(kb, parity, alphas_prev, ml, first_sub=0, tri_sub=None, prev_first=0, next_first=0):
        (st_cur, pt_cur), (st_nxt, pt_prv) = bufs[parity], bufs[1 - parity]
        if next_first is not None:
            issue_scores(kb + 1, st_nxt, next_first)
        issue_pv(jnp.maximum(kb - 1, 0), pt_prv, alphas_prev, prev_first)
        return softmax(kb, st_cur, pt_cur, ml, first_sub, tri_sub)

    def body(i, carry):
        alphas, ml = carry
        ml, alphas = step(2 * i, 0, alphas, ml)
        ml, alphas = step(2 * i + 1, 1, alphas, ml)
        return alphas, ml

    acc_ref[...] = jnp.zeros_like(acc_ref)
    pt_b[...] = jnp.zeros_like(pt_b)
    issue_scores(0, st_a)
    n_main = qi * (tq // tk)
    init_ml = tuple((jnp.full((1, sw), NEG_INF, F32), jnp.zeros((1, sw), F32)) for _ in range(nsub))
    alphas, ml = lax.fori_loop(0, n_main // 2, body, (ones, init_ml))
    ndiag = tq // tk
    for d in range(ndiag):
        ml, alphas = step(n_main + d, d % 2, alphas, ml, first_sub=d, tri_sub=d, prev_first=max(d - 1, 0),
                          next_first=d + 1 if d + 1 < ndiag else None)
    issue_pv(n_main + ndiag - 1, bufs[(ndiag - 1) % 2][1], alphas, ndiag - 1)
    for r in range(nsub):
        o_ref[r * sw:(r + 1) * sw, :] = (acc_ref[r] / ml[r][1]).T.astype(o_ref.dtype)


def _fox(qkv, c, lp_real):
    bsz, lp = qkv.shape[:2]
    h, dh, w = FOX_HEADS, FOX_DH, BRANCH_WIDTH
    tq = FOX_TQ
    pos = jnp.arange(lp)
    ck = jnp.where(((pos >= N_PAD) & (pos < lp_real))[None, :, None], c, -NEG_INF) * LOG2E
    est = 2 * (2 * tq * dh * 2 + 2 * lp * dh * 2 + lp * LANE * 4) + (16 << 20)
    return pl.pallas_call(
        _fox_kernel,
        name="fox",
        grid=(bsz, h, lp // tq),
        in_specs=[pl.BlockSpec((None, tq, dh), lambda b, hh, i: (b, i, hh)),
                  pl.BlockSpec((None, lp, dh), lambda b, hh, i: (b, 0, h + hh)),
                  pl.BlockSpec((None, lp, dh), lambda b, hh, i: (b, 0, 2 * h + hh)),
                  pl.BlockSpec((None, lp, h), lambda b, hh, i: (b, 0, 0))],
        out_specs=pl.BlockSpec((None, tq, dh), lambda b, hh, i: (b, i, hh)),
        out_shape=jax.ShapeDtypeStruct((bsz, lp, w), BF16),
        scratch_shapes=[pltpu.VMEM((tq // FOX_SUB, dh, FOX_SUB), F32),
                        pltpu.VMEM((FOX_TK, tq), F32), pltpu.VMEM((FOX_TK, tq), F32),
                        pltpu.VMEM((FOX_TK, tq), BF16), pltpu.VMEM((FOX_TK, tq), BF16),
                        pltpu.VMEM((dh, tq), BF16)],
        compiler_params=_params(("parallel", "parallel", "parallel"), est),
    )(qkv, qkv, qkv, ck)


S5_CHUNK = 8
S5_SLAB = LANE // S5_GROUP
S5_NSLAB = S5_GROUPS // S5_SLAB


def _s5_tables(a_re, a_im, b_re, b_im, c_re, c_im, log_step):
    hp = lax.Precision.HIGHEST
    g, p, cg, n, sg, ns = S5_GROUPS, S5_STATE, S5_GROUP, S5_CHUNK, S5_SLAB, S5_NSLAB
    dt = jnp.exp(log_step)[:, None]
    mag = jnp.exp(a_re * dt)
    bar_re = mag * jnp.cos(a_im * dt)
    bar_im = mag * jnp.sin(a_im * dt)
    den = a_re * a_re + a_im * a_im
    num_re = bar_re - 1.0
    coef_re = (num_re * a_re + bar_im * a_im) / den
    coef_im = (bar_im * a_re - num_re * a_im) / den
    bb_re = coef_re[..., None] * b_re - coef_im[..., None] * b_im
    bb_im = coef_re[..., None] * b_im + coef_im[..., None] * b_re
    pw_re, pw_im = [jnp.ones_like(bar_re)], [jnp.zeros_like(bar_im)]
    for _ in range(n):
        pw_re, pw_im = (pw_re + [pw_re[-1] * bar_re - pw_im[-1] * bar_im],
                        pw_im + [pw_re[-1] * bar_im + pw_im[-1] * bar_re])
    row_group = jnp.arange(LANE) // cg

    def block_diag(v):
        width = v.shape[-1]
        col_group = jnp.arange(sg * width) // width
        return jnp.where(row_group[:, None] == col_group[None, :], jnp.tile(v, (1,) * (v.ndim - 1) + (sg,)), 0.0)

    w_re = jnp.stack([pw_re[n - 1 - s][..., None] * bb_re - pw_im[n - 1 - s][..., None] * bb_im for s in range(n)])
    w_im = jnp.stack([pw_re[n - 1 - s][..., None] * bb_im + pw_im[n - 1 - s][..., None] * bb_re for s in range(n)])
    gm = jnp.concatenate([block_diag(x.transpose(0, 1, 3, 2).reshape(n, ns, LANE, p)) for x in (w_re, w_im)],
                         axis=-1).transpose(1, 0, 2, 3)

    ca_re = jnp.stack([c_re * pw_re[tau][:, None, :] - c_im * pw_im[tau][:, None, :] for tau in range(n + 1)])
    ca_im = jnp.stack([c_re * pw_im[tau][:, None, :] + c_im * pw_re[tau][:, None, :] for tau in range(n + 1)])
    k = (jnp.einsum('tgcp,gpd->tgcd', ca_re[:n], bb_re, precision=hp)
         - jnp.einsum('tgcp,gpd->tgcd', ca_im[:n], bb_im, precision=hp))
    k_bd = block_diag(k.transpose(0, 1, 3, 2).reshape(n, ns, LANE, cg))
    lag = jnp.arange(n)[None, :] - jnp.arange(n)[:, None]
    mm = jnp.where((lag >= 0)[:, :, None, None, None], k_bd[jnp.clip(lag, 0, n - 1)], 0.0)
    mm = mm.transpose(2, 0, 3, 1, 4).reshape(ns, n, LANE, n * LANE)

    pmt = jnp.concatenate([block_diag(x.reshape(n, ns, LANE, p)) for x in (ca_re[1:], -ca_im[1:])], axis=-1)
    pmt = pmt.transpose(1, 0, 2, 3).reshape(ns, n * LANE, 2 * sg * p)
    a8 = jnp.concatenate([pw_re[n].reshape(ns, sg * p), pw_im[n].reshape(ns, sg * p)], axis=1)
    return gm.astype(BF16), mm.astype(BF16), pmt.astype(BF16), a8


def _chunk_rows(ref, s, nb):
    return ref[pl.ds(s, nb, stride=S5_CHUNK), :]


def _s5_in_kernel(u_ref, gm_ref, z_ref):
    nb = z_ref.shape[0]
    acc = None
    for s in range(S5_CHUNK):
        d = jnp.dot(_chunk_rows(u_ref, s, nb).astype(BF16), gm_ref[s], preferred_element_type=F32)
        acc = d if acc is None else acc + d
    z_ref[...] = acc


def _s5_state_kernel(z_ref, a_ref, h_ref, st_ref):
    @pl.when(pl.program_id(1) == 0)
    def _():
        st_ref[...] = jnp.zeros_like(st_ref)

    ns, nb, sw2 = z_ref.shape
    sw = sw2 // 2
    coef = [(a_ref[j:j + 1, :sw], a_ref[j:j + 1, sw:]) for j in range(ns)]

    def body(n, carry):
        out = []
        for j in range(ns):
            hr, hi = carry[j]
            h_ref[j, pl.ds(n, 1), :] = jnp.concatenate([hr, hi], axis=1)
            z = z_ref[j, pl.ds(n, 1), :]
            ar, ai = coef[j]
            out.append((ar * hr - ai * hi + z[:, :sw], ar * hi + ai * hr + z[:, sw:]))
        return tuple(out)

    init = tuple((st_ref[j:j + 1, :sw], st_ref[j:j + 1, sw:]) for j in range(ns))
    fin = lax.fori_loop(0, nb, body, init)
    for j in range(ns):
        st_ref[j:j + 1, :sw] = fin[j][0]
        st_ref[j:j + 1, sw:] = fin[j][1]


def _s5_out_kernel(u_ref, h_ref, mm_ref, pm_ref, d_ref, o_ref, y_ref):
    nb = h_ref.shape[0]
    acc = lax.dot_general(h_ref[...].astype(BF16), pm_ref[...], (((1,), (1,)), ((), ())),
                          preferred_element_type=F32)
    for s in range(S5_CHUNK):
        acc = acc + jnp.dot(_chunk_rows(u_ref, s, nb).astype(BF16), mm_ref[s], preferred_element_type=F32)
    for t in range(S5_CHUNK):
        y_ref[pl.ds(t, nb, stride=S5_CHUNK), :] = acc[:, t * LANE:(t + 1) * LANE]
    o_ref[...] = jax.nn.gelu(y_ref[...] + d_ref[...] * u_ref[...]).astype(o_ref.dtype)


def _s5_branch(proj, u_blk, a_re, a_im, b_re, b_im, c_re, c_im, d_skip, log_step, w_glu):
    bsz, lp = proj.shape[:2]
    w, n, ns = BRANCH_WIDTH, S5_CHUNK, S5_NSLAB
    sw2 = 2 * S5_SLAB * S5_STATE
    tb = _pick(lp, ROW_TILES)
    nb = tb // n
    nchunk = lp // n
    gm, mm, pm, a8 = _s5_tables(a_re, a_im, b_re, b_im, c_re, c_im, log_step)
    u_spec = pl.BlockSpec((None, tb, LANE), lambda b, j, i: (b, i, u_blk * ns + j))
    st_spec = pl.BlockSpec((None, None, nb, sw2), lambda b, j, i: (b, j, i, 0))
    z = pl.pallas_call(
        _s5_in_kernel,
        name="s5_in",
        grid=(bsz, ns, lp // tb),
        in_specs=[u_spec, pl.BlockSpec((None, n, LANE, sw2), lambda b, j, i: (j, 0, 0, 0))],
        out_specs=st_spec,
        out_shape=jax.ShapeDtypeStruct((bsz, ns, nchunk, sw2), F32),
        compiler_params=_params(("parallel", "parallel", "parallel"),
                                2 * (tb * LANE * 4 + n * LANE * sw2 * 2 + nb * sw2 * 4) + 2 * nb * sw2 * 4),
    )(proj, gm)
    nbb = _pick(nchunk, (96, 64, 32, 16, 8))
    blk = pl.BlockSpec((None, ns, nbb, sw2), lambda b, i: (b, 0, i, 0))
    h_start = pl.pallas_call(
        _s5_state_kernel,
        name="s5_state",
        grid=(bsz, nchunk // nbb),
        in_specs=[blk, pl.BlockSpec((ns, sw2), lambda b, i: (0, 0))],
        out_specs=blk,
        out_shape=jax.ShapeDtypeStruct((bsz, ns, nchunk, sw2), F32),
        scratch_shapes=[pltpu.VMEM((ns, sw2), F32)],
        compiler_params=_params(("parallel", "arbitrary"), 4 * ns * nbb * sw2 * 4),
    )(z, a8)
    y = pl.pallas_call(
        _s5_out_kernel,
        name="s5_out",
        grid=(bsz, ns, lp // tb),
        in_specs=[u_spec, st_spec,
                  pl.BlockSpec((None, n, LANE, n * LANE), lambda b, j, i: (j, 0, 0, 0)),
                  pl.BlockSpec((None, n * LANE, sw2), lambda b, j, i: (j, 0, 0)),
                  pl.BlockSpec((1, LANE), lambda b, j, i: (0, j))],
        out_specs=pl.BlockSpec((None, tb, LANE), lambda b, j, i: (b, i, j)),
        out_shape=jax.ShapeDtypeStruct((bsz, lp, w), BF16),
        scratch_shapes=[pltpu.VMEM((tb, LANE), F32)],
        compiler_params=_params(("parallel", "parallel", "parallel"),
                                2 * (tb * LANE * 4 + nb * sw2 * 4 + n * LANE * n * LANE * 2 + sw2 * n * LANE * 2
                                     + tb * LANE * 2) + 3 * nb * n * LANE * 4),
    )(proj, h_start, mm, pm, d_skip.reshape(1, w))
    return _glu(y.reshape(bsz * lp, w), w_glu.astype(BF16))


def _pad_cols(w, n):
    return jnp.pad(w, ((0, 0), (0, n - w.shape[1])))


def _scale_cols(x, s):
    return x * s


MOE_TM = 512


def _new_expert(te_ref, i):
    return (i == 0) | (te_ref[i] != te_ref[jnp.maximum(i - 1, 0)])


def _moe_up_kernel(te_ref, x_ref, wg_ref, wu_ref, s_ref, o_ref, wgb_ref, wub_ref, *, ni):
    i = pl.program_id(1)

    @pl.when(_new_expert(te_ref, i))
    def _():
        wgb_ref[...] = wg_ref[...].astype(BF16)
        wub_ref[...] = wu_ref[...].astype(BF16)

    @pl.when(i < te_ref[ni])
    def _():
        x = x_ref[...]
        g = jnp.dot(x, wgb_ref[...], preferred_element_type=F32)
        u = jnp.dot(x, wub_ref[...], preferred_element_type=F32)
        o_ref[...] = (g * jax.nn.sigmoid(g) * u * s_ref[...]).astype(o_ref.dtype)

    @pl.when(i >= te_ref[ni])
    def _():
        o_ref[...] = jnp.zeros_like(o_ref)


def _moe_down_kernel(te_ref, a_ref, w_ref, o_ref, wb_ref, *, ni):
    i = pl.program_id(1)

    @pl.when(_new_expert(te_ref, i))
    def _():
        wb_ref[...] = w_ref[...].astype(BF16)

    @pl.when(i < te_ref[ni])
    def _():
        o_ref[...] = jnp.dot(a_ref[...], wb_ref[...], preferred_element_type=F32).astype(o_ref.dtype)

    @pl.when(i >= te_ref[ni])
    def _():
        o_ref[...] = jnp.zeros_like(o_ref)


def _moe_experts(xs, row_scale, tile_info, layer, w_gate, w_up, w_down):
    r, d = xs.shape
    f = w_gate.shape[-1]
    tm = MOE_TM
    ni = r // tm
    tn = _pick(f, (512, 256, 128))
    wmap = lambda j, i, te: (layer, te[i], 0, j)
    act = pl.pallas_call(
        functools.partial(_moe_up_kernel, ni=ni),
        name="moe_up",
        grid_spec=pltpu.PrefetchScalarGridSpec(
            num_scalar_prefetch=1, grid=(f // tn, ni),
            in_specs=[pl.BlockSpec((tm, d), lambda j, i, te: (i, 0)),
                      pl.BlockSpec((None, None, d, tn), wmap), pl.BlockSpec((None, None, d, tn), wmap),
                      pl.BlockSpec((tm, 1), lambda j, i, te: (i, 0))],
            out_specs=pl.BlockSpec((tm, tn), lambda j, i, te: (i, j)),
            scratch_shapes=[pltpu.VMEM((d, tn), BF16), pltpu.VMEM((d, tn), BF16)]),
        out_shape=jax.ShapeDtypeStruct((r, f), BF16),
        compiler_params=_params(("arbitrary", "arbitrary"),
                                2 * (tm * d * 2 + 2 * d * tn * 4 + tm * tn * 2) + 2 * d * tn * 2 + 4 * tm * tn * 4),
    )(tile_info, xs, w_gate, w_up, row_scale)
    tn2 = _pick(d, (1024, 512, 256, 128))
    return pl.pallas_call(
        functools.partial(_moe_down_kernel, ni=ni),
        name="moe_down",
        grid_spec=pltpu.PrefetchScalarGridSpec(
            num_scalar_prefetch=1, grid=(d // tn2, ni),
            in_specs=[pl.BlockSpec((tm, f), lambda j, i, te: (i, 0)),
                      pl.BlockSpec((None, None, f, tn2), wmap)],
            out_specs=pl.BlockSpec((tm, tn2), lambda j, i, te: (i, j)),
            scratch_shapes=[pltpu.VMEM((f, tn2), BF16)]),
        out_shape=jax.ShapeDtypeStruct((r, d), BF16),
        compiler_params=_params(("arbitrary", "arbitrary"),
                                2 * (tm * f * 2 + f * tn2 * 4 + tm * tn2 * 4) + f * tn2 * 2 + 2 * tm * tn2 * 4),
    )(tile_info, act, w_down)


def _moe_ffn(h32, hb, w_router, b_router, layer, w_gate, w_up, w_down):
    t, d = hb.shape
    e = w_gate.shape[1]
    tm = MOE_TM
    logits = _mm(h32, _pad_cols(w_router, LANE), precision=lax.Precision.HIGHEST,
                 name="router")[:, :N_EXPERTS] + b_router
    top_vals, top_idx = lax.top_k(logits, TOP_K)
    weights = jax.nn.softmax(top_vals, axis=-1)

    n_assign = t * TOP_K
    n_rows = n_assign + e * tm
    flat_e = top_idx.reshape(n_assign).astype(jnp.int32)
    onehot = jax.nn.one_hot(flat_e, e, dtype=jnp.int32)
    seen = jnp.cumsum(onehot, axis=0)
    counts = seen[-1]
    padded = (counts + tm - 1) // tm * tm
    group_end = jnp.cumsum(padded)
    group_start = group_end - padded
    sorted_start = jnp.cumsum(counts) - counts
    dest = group_start[flat_e] + jnp.sum(seen * onehot, axis=1) - 1
    order = jnp.argsort(flat_e, stable=True).astype(jnp.int32)
    rows = jnp.arange(n_rows, dtype=jnp.int32)
    row_e = jnp.minimum(jnp.searchsorted(group_end, rows, side="right"), e - 1).astype(jnp.int32)
    rank = rows - group_start[row_e]
    used = rank < counts[row_e]
    assign = order[jnp.where(used, sorted_start[row_e] + rank, 0)]
    src_tok = jnp.where(used, assign // TOP_K, 0)
    row_scale = jnp.where(used, weights.reshape(n_assign)[assign], 0.0)
    tile_info = jnp.concatenate([row_e[::tm], (group_end[-1:] // tm).astype(jnp.int32)])

    xs = jnp.take(hb, src_tok, axis=0)
    y = _moe_experts(xs, row_scale[:, None], tile_info, layer, w_gate, w_up, w_down)
    dest = dest.reshape(t, TOP_K)
    return [jnp.take(y, dest[:, s], axis=0) for s in range(TOP_K)]


def kernel(x, meta_tokens, w_in, b_forget, w_gate_down, w_gate_up, b_gate, w_branch, w_out, ln_mix_g, ln_mix_b, ln_ffn_g, ln_ffn_b, rwkv_mu, rwkv_w_rkv, rwkv_w0, rwkv_w1, rwkv_w2, rwkv_a0, rwkv_a1, rwkv_a2, rwkv_v0, rwkv_v1, rwkv_v2, rwkv_g1, rwkv_g2, rwkv_k_k, rwkv_k_a, rwkv_r_k, rwkv_ln_w, rwkv_ln_b, s5_a_re, s5_a_im, s5_b_re, s5_b_im, s5_c_re, s5_c_im, s5_d, s5_log_step, s5_w_glu, ffn_w_gate, ffn_w_up, ffn_w_down, moe_router, moe_router_b, moe_w_gate, moe_w_up, moe_w_down):
    bsz, seq, d = x.shape
    depth = w_in.shape[0]
    w = BRANCH_WIDTH
    lp_real = N_PAD + N_META + seq
    lp = -(-lp_real // FOX_TQ) * FOX_TQ
    t = bsz * lp
    meta = jnp.broadcast_to(meta_tokens.astype(x.dtype)[None], (bsz, N_META, d))
    h = jnp.concatenate([jnp.zeros((bsz, N_PAD, d), x.dtype), meta, x,
                         jnp.zeros((bsz, lp - lp_real, d), x.dtype)], axis=1)
    hb = h.astype(BF16)
    pos = jnp.arange(lp)
    real = ((pos >= N_PAD) & (pos < lp_real))[None, :, None]
    f_lo = 7 * w
    q_scale = jnp.concatenate([jnp.full((1, w), FOX_DH ** -0.5 * LOG2E, F32), jnp.ones((1, 2 * w), F32)], axis=1)
    ret_consts = _ret_consts(lp)
    v_first = None
    for layer in range(depth):
        wl = w_in[layer]
        w_f32 = jnp.concatenate([wl[:, :4 * w], wl[:, f_lo + FOX_HEADS:]], axis=1).astype(BF16)
        w_fox = wl[:, 4 * w:f_lo].astype(BF16)
        w_small = _pad_cols(jnp.concatenate([w_gate_down[layer], wl[:, f_lo:f_lo + FOX_HEADS]], axis=1),
                            GATE_RANK + LANE).astype(BF16)
        hb2 = hb.reshape(t, d)
        proj = _mm(hb2, w_f32, name="proj").reshape(bsz, lp, 5 * w)
        qkv = _mm(hb2, w_fox, out_dtype=BF16, ep=_scale_cols, extras=((q_scale, "row", 0),),
                  name="proj_fox").reshape(bsz, lp, 3 * w)
        small = _mm(hb2, w_small, name="proj_small")
        f_logit = small[:, GATE_RANK:GATE_RANK + FOX_HEADS].reshape(bsz, lp, FOX_HEADS)

        o_ret = _retention(proj, ret_consts)

        vres = None if layer == 0 else (rwkv_v0[layer - 1], rwkv_v1[layer - 1], rwkv_v2[layer - 1])
        r, lw, k, v, a, b, gate, bonus = _rwkv_pre(
            proj, 3, v_first, vres, rwkv_mu[layer], rwkv_w_rkv[layer], rwkv_w0[layer], rwkv_w1[layer],
            rwkv_w2[layer], rwkv_a0[layer], rwkv_a1[layer], rwkv_a2[layer], rwkv_g1[layer], rwkv_g2[layer],
            rwkv_k_k[layer], rwkv_k_a[layer], rwkv_r_k[layer])
        if layer == 0:
            v_first = v
        o_rwkv = _wkv7(r, lw, k, v, a, b, gate, bonus, rwkv_ln_w[layer], rwkv_ln_b[layer])

        log_f = jnp.where(real, jax.nn.log_sigmoid(f_logit + b_forget[layer]), 0.0)
        c = jnp.cumsum(log_f, axis=1)
        o_fox = _fox(qkv, c, lp_real)

        o_s5 = _s5_branch(proj, 4, s5_a_re[layer], s5_a_im[layer], s5_b_re[layer], s5_b_im[layer],
                          s5_c_re[layer], s5_c_im[layer], s5_d[layer], s5_log_step[layer], s5_w_glu[layer])

        merged = _merge([o_ret.reshape(t, w), o_rwkv.reshape(t, w), o_fox.reshape(t, w), o_s5],
                        w_branch[layer].astype(BF16), small, w_gate_up[layer].astype(BF16), b_gate[layer])
        mix = _mm(merged, w_out, b_layer=layer, name="w_out")
        h, hb = _res_ln([mix], h, ln_mix_g[layer], ln_mix_b[layer], mask_pad=None)

        j = layer // 2
        hb2 = hb.reshape(t, d)
        if layer % 2 == 0:
            act = _swiglu_up(hb2, ffn_w_gate[j].astype(BF16), ffn_w_up[j].astype(BF16))
            ffn = [_mm(act, ffn_w_down[j].astype(BF16), name="ffn_down")]
        else:
            ffn = _moe_ffn(h.reshape(t, d), hb2, moe_router[j], moe_router_b[j], j,
                           moe_w_gate, moe_w_up, moe_w_down)
        h, hb = _res_ln(ffn, h, ln_ffn_g[layer], ln_ffn_b[layer], mask_pad=(N_PAD, lp_real))
    return h[:, N_PAD + N_META:lp_real]
```
